```python
import jax, jax.numpy as jnp
from jax import lax
import numpy as np

D_MODEL = 2048
BATCH = 4
SEQ = 4096
DEPTH = 2
DEC_BATCH = 32
DEC_SEQ = 64
PAST_LEN = 4096

CHUNK = 64
N_MIXERS = 2
CONV_W = 3
POOL_WINDOWS = (2, 4, 8, 16)
N_POOL_GROUPS = 4
POOL_GROUP = D_MODEL // N_POOL_GROUPS
POOL_STATE = max(POOL_WINDOWS) - 1
N_EXPERTS = 64
TOP_K = 8
N_EXPERT_GROUPS = 8
TOPK_GROUPS = 4
D_EXPERT = 512
D_SHARED = 512
ROUTED_SCALE = 2.5
MOE_BLOCK = 128
EPS = 1e-6

kernel_name = 'hybrid_streaming_conv_pool_moe_step'


def rms_norm(x, g):
    xf = x.astype(jnp.float32)
    y = xf * lax.rsqrt(jnp.mean(xf * xf, axis=-1, keepdims=True) + EPS)
    return (y * g.astype(jnp.float32)).astype(x.dtype)


def short_conv_mixer(h, conv_prev, w_in, conv_w, w_out):
    L = h.shape[1]
    b, c, v = jnp.split(h @ w_in, 3, axis=-1)
    u = c * v
    u_ext = jnp.concatenate([conv_prev.astype(u.dtype), u], axis=1)
    conv = sum(u_ext[:, k:k + L] * conv_w[k] for k in range(CONV_W))
    y = (b * conv) @ w_out
    return y, u_ext[:, -(CONV_W - 1):]


def pool_mixer(h, pool_prev, start_pos, w_group, scale):
    B_, L, D = h.shape
    P = POOL_STATE
    h_ext = jnp.concatenate([pool_prev.astype(h.dtype), h], axis=1)
    cs = jnp.cumsum(h_ext.astype(jnp.float32), axis=1)
    cs = jnp.pad(cs, ((0, 0), (1, 0), (0, 0)))
    pos = start_pos + jnp.arange(L)
    means = []
    for g, w in enumerate(POOL_WINDOWS):
        sl = slice(g * POOL_GROUP, (g + 1) * POOL_GROUP)
        s = cs[:, P + 1:P + 1 + L, sl] - cs[:, P + 1 - w:P + 1 - w + L, sl]
        cnt = jnp.minimum(pos + 1, w).astype(jnp.float32)
        means.append(s / cnt[None, :, None])
    pooled = jnp.concatenate(means, axis=-1)
    d = (pooled - h.astype(jnp.float32)).astype(h.dtype).reshape(B_, L, N_POOL_GROUPS, POOL_GROUP)
    y = jnp.einsum('blgc,gcd->blgd', d, w_group).reshape(B_, L, D) * scale
    return y, h_ext[:, -P:]


def route(h2, w_router, b_router):
    T = h2.shape[0]
    scores = jax.nn.sigmoid(h2.astype(jnp.float32) @ w_router.astype(jnp.float32))
    biased = scores + b_router.astype(jnp.float32)
    per_group = N_EXPERTS // N_EXPERT_GROUPS
    group_score = lax.top_k(biased.reshape(T, N_EXPERT_GROUPS, per_group), 2)[0].sum(-1)
    _, gidx = lax.top_k(group_score, TOPK_GROUPS)
    gmask = jnp.any(gidx[:, :, None] == jnp.arange(N_EXPERT_GROUPS)[None, None, :], axis=1)
    emask = jnp.repeat(gmask, per_group, axis=1)
    _, eidx = lax.top_k(jnp.where(emask, biased, -jnp.inf), TOP_K)
    wts = jnp.take_along_axis(scores, eidx, axis=1)
    wts = wts / jnp.sum(wts, axis=-1, keepdims=True) * ROUTED_SCALE
    return eidx.astype(jnp.int32), wts


def routed_experts(h2, eidx, wts, wg, wu, wd):
    T, D = h2.shape
    A = T * TOP_K
    e_flat = eidx.reshape(A)
    tok_flat = jnp.repeat(jnp.arange(T, dtype=jnp.int32), TOP_K)
    w_flat = wts.reshape(A)
    order = jnp.argsort(e_flat)
    e_sorted = e_flat[order]
    counts = jnp.bincount(e_flat, length=N_EXPERTS)
    starts = jnp.cumsum(counts) - counts
    padded = (counts + MOE_BLOCK - 1) // MOE_BLOCK * MOE_BLOCK
    pends = jnp.cumsum(padded)
    pstarts = pends - padded
    slot = pstarts[e_sorted] + jnp.arange(A) - starts[e_sorted]
    n_blocks = -(-A // MOE_BLOCK) + N_EXPERTS
    n_rows = n_blocks * MOE_BLOCK
    row_tok = jnp.full((n_rows,), T, jnp.int32).at[slot].set(tok_flat[order])
    row_w = jnp.zeros((n_rows,), jnp.float32).at[slot].set(w_flat[order])
    blk_e = jnp.minimum(jnp.searchsorted(pends, jnp.arange(n_blocks) * MOE_BLOCK, side='right'),
                        N_EXPERTS - 1).astype(jnp.int32)
    x_pad = jnp.concatenate([h2, jnp.zeros((1, D), h2.dtype)], axis=0)

    def step(acc, blk):
        toks, ws, e = blk
        xb = x_pad[toks]
        hb = jax.nn.silu(xb @ wg[e]) * (xb @ wu[e])
        ob = (hb @ wd[e]).astype(jnp.float32) * ws[:, None]
        return acc.at[toks].add(ob), None

    acc0 = jnp.zeros((T + 1, D), jnp.float32)
    acc, _ = lax.scan(step, acc0, (row_tok.reshape(n_blocks, MOE_BLOCK),
                                   row_w.reshape(n_blocks, MOE_BLOCK), blk_e))
    return acc[:T].astype(h2.dtype)


def moe_ffn(h, w_router, b_router, wg, wu, wd, swg, swu, swd):
    B_, L, D = h.shape
    h2 = h.reshape(B_ * L, D)
    eidx, wts = route(h2, w_router, b_router)
    routed = routed_experts(h2, eidx, wts, wg, wu, wd)
    shared = (jax.nn.silu(h2 @ swg) * (h2 @ swu)) @ swd
    return (routed + shared).reshape(B_, L, D)


def encoder_forward(x, c, conv_prev, pool_prev, start_pos, w_mod, b_mod, g_mix, g_ffn, g_final,
                    conv_w_in, conv_w, conv_w_out, pool_w, pool_scale, router_w, router_b,
                    exp_w_gate, exp_w_up, exp_w_down, sh_w_gate, sh_w_up, sh_w_down):
    c_act = jax.nn.silu(c)
    new_conv = conv_prev
    new_pool = pool_prev
    for i in range(DEPTH):
        mod = (c_act @ w_mod[i] + b_mod[i])[:, None, :]
        sh_t, sc_t, gt_t, sh_c, sc_c, gt_c = jnp.split(mod, 6, axis=-1)
        h = rms_norm(x, g_mix[i]) * (1 + sc_t) + sh_t
        if i % N_MIXERS == 0:
            y, new_conv = short_conv_mixer(h, conv_prev, conv_w_in, conv_w, conv_w_out)
        else:
            y, new_pool = pool_mixer(h, pool_prev, start_pos, pool_w, pool_scale)
        x = x + gt_t * y
        h = rms_norm(x, g_ffn[i]) * (1 + sc_c) + sh_c
        x = x + gt_c * moe_ffn(h, router_w[i], router_b[i], exp_w_gate[i], exp_w_up[i],
                               exp_w_down[i], sh_w_gate[i], sh_w_up[i], sh_w_down[i])
    return rms_norm(x, g_final), new_conv, new_pool


def setup_inputs(seed: int = 0) -> dict:
    key = jax.random.key(seed)
    ks = jax.random.split(key, 24)
    D = D_MODEL

    def nrm(k, shape, s):
        return jax.random.normal(k, shape, jnp.float32) * s

    return {
        'x_prompt': nrm(ks[0], (BATCH, SEQ, D), 1.0),
        'x_sample': nrm(ks[1], (DEC_BATCH, DEC_SEQ, D), 1.0),
        'state_conv': nrm(ks[2], (DEC_BATCH, CONV_W - 1, D), 1.0),
        'state_pool': nrm(ks[3], (DEC_BATCH, POOL_STATE, D), 1.0),
        'c_prompt': nrm(ks[4], (BATCH, D), 1.0),
        'c_sample': nrm(ks[5], (DEC_BATCH, D), 1.0),
        'w_mod': nrm(ks[6], (DEPTH, D, 6 * D), 0.5 * D ** -0.5),
        'b_mod': nrm(ks[7], (DEPTH, 6 * D), 0.05),
        'g_mix': 1.0 + nrm(ks[8], (DEPTH, D), 0.05),
        'g_ffn': 1.0 + nrm(ks[9], (DEPTH, D), 0.05),
        'g_final': 1.0 + nrm(ks[10], (D,), 0.05),
        'conv_w_in': nrm(ks[11], (D, 3 * D), D ** -0.5),
        'conv_w': nrm(ks[12], (CONV_W, D), CONV_W ** -0.5),
        'conv_w_out': nrm(ks[13], (D, D), D ** -0.5),
        'pool_w': nrm(ks[14], (N_POOL_GROUPS, POOL_GROUP, POOL_GROUP), POOL_GROUP ** -0.5),
        'pool_scale': 1.0 + nrm(ks[15], (D,), 0.1),
        'router_w': nrm(ks[16], (DEPTH, D, N_EXPERTS), D ** -0.5),
        'router_b': nrm(ks[17], (DEPTH, N_EXPERTS), 0.01),
        'exp_w_gate': nrm(ks[18], (DEPTH, N_EXPERTS, D, D_EXPERT), D ** -0.5),
        'exp_w_up': nrm(ks[19], (DEPTH, N_EXPERTS, D, D_EXPERT), D ** -0.5),
        'exp_w_down': nrm(ks[20], (DEPTH, N_EXPERTS, D_EXPERT, D), D_EXPERT ** -0.5),
        'sh_w_gate': nrm(ks[21], (DEPTH, D, D_SHARED), D ** -0.5),
        'sh_w_up': nrm(ks[22], (DEPTH, D, D_SHARED), D ** -0.5),
        'sh_w_down': nrm(ks[23], (DEPTH, D_SHARED, D), D_SHARED ** -0.5),
    }


def reference(x_prompt, x_sample, state_conv, state_pool, c_prompt, c_sample, w_mod, b_mod,
              g_mix, g_ffn, g_final, conv_w_in, conv_w, conv_w_out, pool_w, pool_scale,
              router_w, router_b, exp_w_gate, exp_w_up, exp_w_down, sh_w_gate, sh_w_up, sh_w_down):
    weights = (w_mod, b_mod, g_mix, g_ffn, g_final, conv_w_in, conv_w, conv_w_out, pool_w,
               pool_scale, router_w, router_b, exp_w_gate, exp_w_up, exp_w_down,
               sh_w_gate, sh_w_up, sh_w_down)
    b_p = x_prompt.shape[0]
    conv0 = jnp.zeros((b_p, CONV_W - 1, D_MODEL), x_prompt.dtype)
    pool0 = jnp.zeros((b_p, POOL_STATE, D_MODEL), x_prompt.dtype)
    y_prompt, conv_p, pool_p = encoder_forward(x_prompt, c_prompt, conv0, pool0, 0, *weights)
    y_sample, conv_s, pool_s = encoder_forward(x_sample, c_sample, state_conv, state_pool,
                                               PAST_LEN, *weights)
    return (y_prompt, y_sample, conv_p, conv_s, pool_p, pool_s)
```

```python
import functools

import numpy as np
import jax
import jax.numpy as jnp
from jax import lax
from jax.experimental import pallas as pl
from jax.experimental.pallas import tpu as pltpu

GROUP = 64
CONV_W = 3
POOL_WINDOWS = (2, 4, 8, 16)
POOL_HDR = 16
CONV_HDR = 8
TOP_K = 8
N_EXPERT_GROUPS = 8
TOPK_GROUPS = 4
ROUTED_SCALE = 2.5
PAST_LEN = 4096
EPS = 1e-6
VMEM_LIMIT_V7X = 56 * 1024 * 1024
NEG_INF = float("-inf")

_F32 = jnp.float32
_BF16 = jnp.bfloat16


def _silu(v):
    return v * jax.nn.sigmoid(v)


def _cparams(n_axes):
    return pltpu.CompilerParams(dimension_semantics=("arbitrary",) * n_axes,
                                vmem_limit_bytes=VMEM_LIMIT_V7X)


def _mod_kernel(c_ref, w_ref, b_ref, o_ref):
    c = c_ref[...]
    ca = _silu(c).astype(_BF16)
    o_ref[...] = jnp.dot(ca, w_ref[...].astype(_BF16), preferred_element_type=_F32) + b_ref[...]


def _mod_rows(c_all, w_mod, b_mod):
    depth, d, n6 = w_mod.shape
    nb = c_all.shape[0]
    tn = min(1024, n6)
    return pl.pallas_call(
        _mod_kernel,
        grid=(depth, n6 // tn),
        in_specs=[pl.BlockSpec((nb, d), lambda l, j: (0, 0)),
                  pl.BlockSpec((None, d, tn), lambda l, j: (l, 0, j)),
                  pl.BlockSpec((None, 1, tn), lambda l, j: (l, 0, j))],
        out_specs=pl.BlockSpec((None, nb, tn), lambda l, j: (l, 0, j)),
        out_shape=jax.ShapeDtypeStruct((depth, nb, n6), _F32),
        compiler_params=_cparams(2),
        name="adaln_rows",
    )(c_all, w_mod, b_mod.reshape(depth, 1, n6))


def _norm_mod_group(x_ref, mod_ref, g_ref, s, d):
    xs = x_ref[s * GROUP:(s + 1) * GROUP, :]
    ms = jnp.mean(xs * xs, axis=-1, keepdims=True)
    xn = (xs * lax.rsqrt(ms + EPS)) * g_ref[...]
    return xn * (1.0 + mod_ref[s:s + 1, d:2 * d]) + mod_ref[s:s + 1, 0:d]


def _conv_mixer_kernel(x_ref, mod_ref, st_ref, g_ref, wb_ref, wc_ref, wv_ref, cw_ref, wo_ref,
                       out_ref, tail_ref, h_scr, acc_scr, ubuf, gbuf, carry,
                       *, gp, d, n_prompt_tiles, tiles_per_seq):
    i = pl.program_id(0)
    j = pl.program_id(1)
    nj = pl.num_programs(1)

    @pl.when(j == 0)
    def _():
        for s in range(gp):
            h_scr[s * GROUP:(s + 1) * GROUP, :] = _norm_mod_group(x_ref, mod_ref, g_ref, s, d).astype(_BF16)
        acc_scr[...] = jnp.zeros_like(acc_scr)

    @pl.when(i == 0)
    def _():
        carry[j] = jnp.zeros((CONV_HDR, carry.shape[2]), _F32)

    hb = h_scr[...]
    bq = jnp.dot(hb, wb_ref[...], preferred_element_type=_F32)
    cq = jnp.dot(hb, wc_ref[...], preferred_element_type=_F32)
    vq = jnp.dot(hb, wv_ref[...], preferred_element_type=_F32)
    u = cq * vq

    is_sample = i >= n_prompt_tiles
    seq_start = is_sample | (lax.rem(i, tiles_per_seq) == 0)
    stride = GROUP + CONV_HDR
    for s in range(gp):
        prev = carry[j] if s == 0 else u[s * GROUP - CONV_HDR:s * GROUP, :]
        use_state = seq_start if s == 0 else is_sample
        ubuf[s * stride:s * stride + CONV_HDR, :] = jnp.where(use_state, st_ref[s], prev)
        ubuf[s * stride + CONV_HDR:(s + 1) * stride, :] = u[s * GROUP:(s + 1) * GROUP, :]
        tail_ref[s] = u[(s + 1) * GROUP - CONV_HDR:(s + 1) * GROUP, :]
    carry[j] = u[gp * GROUP - CONV_HDR:gp * GROUP, :]

    w0 = cw_ref[0:1, :]
    w1 = cw_ref[1:2, :]
    w2 = cw_ref[2:3, :]
    for s in range(gp):
        base = s * stride + CONV_HDR
        u0 = ubuf[base:base + GROUP, :]
        u1 = ubuf[base - 1:base - 1 + GROUP, :]
        u2 = ubuf[base - 2:base - 2 + GROUP, :]
        conv = u2 * w0 + u1 * w1 + u0 * w2
        gbuf[s * GROUP:(s + 1) * GROUP, :] = (bq[s * GROUP:(s + 1) * GROUP, :] * conv).astype(_BF16)
    acc_scr[...] += jnp.dot(gbuf[...], wo_ref[...], preferred_element_type=_F32)

    @pl.when(j == nj - 1)
    def _():
        for s in range(gp):
            sl = slice(s * GROUP, (s + 1) * GROUP)
            out_ref[sl, :] = x_ref[sl, :] + mod_ref[s:s + 1, 2 * d:3 * d] * acc_scr[sl, :]


def _conv_mixer(x, mod_g, st_tbl, g_mix, w_in_bf, conv_w, w_out_bf, *, tm, n_prompt_tiles, tiles_per_seq):
    t, d = x.shape
    gp = tm // GROUP
    tn = min(512, d)
    nj = d // tn
    kern = functools.partial(_conv_mixer_kernel, gp=gp, d=d, n_prompt_tiles=n_prompt_tiles,
                             tiles_per_seq=tiles_per_seq)
    return pl.pallas_call(
        kern,
        grid=(t // tm, nj),
        in_specs=[pl.BlockSpec((tm, d), lambda i, j: (i, 0)),
                  pl.BlockSpec((gp, 3 * d), lambda i, j: (i, 0)),
                  pl.BlockSpec((gp, CONV_HDR, tn), lambda i, j: (i, 0, j)),
                  pl.BlockSpec((1, d), lambda i, j: (0, 0)),
                  pl.BlockSpec((d, tn), lambda i, j: (0, j)),
                  pl.BlockSpec((d, tn), lambda i, j: (0, nj + j)),
                  pl.BlockSpec((d, tn), lambda i, j: (0, 2 * nj + j)),
                  pl.BlockSpec((CONV_W, tn), lambda i, j: (0, j)),
                  pl.BlockSpec((tn, d), lambda i, j: (j, 0))],
        out_specs=[pl.BlockSpec((tm, d), lambda i, j: (i, 0)),
                   pl.BlockSpec((gp, CONV_HDR, tn), lambda i, j: (i, 0, j))],
        out_shape=[jax.ShapeDtypeStruct((t, d), _F32),
                   jax.ShapeDtypeStruct((t // GROUP, CONV_HDR, d), _F32)],
        scratch_shapes=[pltpu.VMEM((tm, d), _BF16),
                        pltpu.VMEM((tm, d), _F32),
                        pltpu.VMEM((gp * (GROUP + CONV_HDR), tn), _F32),
                        pltpu.VMEM((tm, tn), _BF16),
                        pltpu.VMEM((nj, CONV_HDR, tn), _F32)],
        compiler_params=_cparams(2),
        name="conv_mixer",
    )(x, mod_g, st_tbl, g_mix.reshape(1, d), w_in_bf, w_in_bf, w_in_bf, conv_w, w_out_bf)


def _pool_mixer_kernel(x_ref, mod_ref, st_ref, g_ref, pw_ref, ps_ref, out_ref, tail_ref,
                       dbuf, carry, *, gp, d, n_prompt_tiles, tiles_per_seq, tm):
    i = pl.program_id(0)
    pg = d // len(POOL_WINDOWS)
    is_sample = i >= n_prompt_tiles
    seq_start = is_sample | (lax.rem(i, tiles_per_seq) == 0)
    row = lax.broadcasted_iota(jnp.int32, (GROUP, 1), 0)

    @pl.when(i == 0)
    def _():
        carry[...] = jnp.zeros_like(carry)

    prev = carry[...]
    for s in range(gp):
        h = _norm_mod_group(x_ref, mod_ref, g_ref, s, d)
        use_state = seq_start if s == 0 else is_sample
        hdr = jnp.where(use_state, st_ref[s], prev)
        he = jnp.concatenate([hdr, h], axis=0)
        pos0 = jnp.where(is_sample, PAST_LEN, lax.rem(i, tiles_per_seq) * tm + s * GROUP)
        pos1 = pos0 + row + 1
        for q, w in enumerate(POOL_WINDOWS):
            cs = slice(q * pg, (q + 1) * pg)
            sw = he[:, cs]
            span = 1
            while span < w:
                sw = sw + pltpu.roll(sw, span, 0)
                span *= 2
            inv_cnt = 1.0 / jnp.minimum(pos1, w).astype(_F32)
            pooled = sw[POOL_HDR:, :] * inv_cnt
            dbuf[s * GROUP:(s + 1) * GROUP, cs] = (pooled - h[:, cs]).astype(_BF16)
        prev = h[GROUP - POOL_HDR:, :]
        tail_ref[s] = prev
    carry[...] = prev
    for q in range(len(POOL_WINDOWS)):
        cs = slice(q * pg, (q + 1) * pg)
        y = jnp.dot(dbuf[:, cs], pw_ref[q], preferred_element_type=_F32) * ps_ref[:, cs]
        for s in range(gp):
            sl = slice(s * GROUP, (s + 1) * GROUP)
            out_ref[sl, cs] = x_ref[sl, cs] + mod_ref[s:s + 1, 2 * d + q * pg:2 * d + (q + 1) * pg] * y[sl, :]


def _pool_mixer(x, mod_g, st_tbl, g_mix, pool_w_bf, pool_scale, *, tm, n_prompt_tiles, tiles_per_seq):
    t, d = x.shape
    gp = tm // GROUP
    nq = len(POOL_WINDOWS)
    pg = d // nq
    kern = functools.partial(_pool_mixer_kernel, gp=gp, d=d, n_prompt_tiles=n_prompt_tiles,
                             tiles_per_seq=tiles_per_seq, tm=tm)
    return pl.pallas_call(
        kern,
        grid=(t // tm,),
        in_specs=[pl.BlockSpec((tm, d), lambda i: (i, 0)),
                  pl.BlockSpec((gp, 3 * d), lambda i: (i, 0)),
                  pl.BlockSpec((gp, POOL_HDR, d), lambda i: (i, 0, 0)),
                  pl.BlockSpec((1, d), lambda i: (0, 0)),
                  pl.BlockSpec((nq, pg, pg), lambda i: (0, 0, 0)),
                  pl.BlockSpec((1, d), lambda i: (0, 0))],
        out_specs=[pl.BlockSpec((tm, d), lambda i: (i, 0)),
                   pl.BlockSpec((gp, POOL_HDR, d), lambda i: (i, 0, 0))],
        out_shape=[jax.ShapeDtypeStruct((t, d), _F32),
                   jax.ShapeDtypeStruct((t // GROUP, POOL_HDR, d), _F32)],
        scratch_shapes=[pltpu.VMEM((tm, d), _BF16),
                        pltpu.VMEM((POOL_HDR, d), _F32)],
        compiler_params=_cparams(1),
        name="pool_mixer",
    )(x, mod_g, st_tbl, g_mix.reshape(1, d), pool_w_bf, pool_scale.reshape(1, d))


def _router_kernel(x_ref, mod_ref, g_ref, wr_ref, br_ref, tri_ref, swg_ref, swu_ref, swd_ref,
                   h_ref, sh_ref, eidx_ref, wts_ref, rank_ref, cnt_ref, run_scr, hb_scr,
                   *, gp, d, n_exp, tm):
    i = pl.program_id(0)
    per_group = n_exp // N_EXPERT_GROUPS

    @pl.when(i == 0)
    def _():
        run_scr[...] = jnp.zeros_like(run_scr)

    for s in range(gp):
        sl = slice(s * GROUP, (s + 1) * GROUP)
        h = _norm_mod_group(x_ref, mod_ref, g_ref, s, d)
        h_ref[sl, :] = h
        hb_scr[sl, :] = h.astype(_BF16)

    logits = lax.dot_general(wr_ref[...], h_ref[...], (((1,), (1,)), ((), ())),
                             precision=lax.Precision.HIGHEST, preferred_element_type=_F32)
    sub = lax.broadcasted_iota(jnp.int32, (per_group, tm), 0)
    scores, biased, eids = [], [], []
    for g in range(N_EXPERT_GROUPS):
        rs = slice(g * per_group, (g + 1) * per_group)
        sc = jax.nn.sigmoid(logits[rs, :])
        scores.append(sc)
        biased.append(sc + br_ref[rs, :])
        eids.append(sub + g * per_group)

    gscore = []
    for g in range(N_EXPERT_GROUPS):
        m1 = jnp.max(biased[g], axis=0, keepdims=True)
        i1 = jnp.min(jnp.where(biased[g] == m1, sub, per_group), axis=0, keepdims=True)
        m2 = jnp.max(jnp.where(sub == i1, NEG_INF, biased[g]), axis=0, keepdims=True)
        gscore.append(m1 + m2)
    vals = []
    for g in range(N_EXPERT_GROUPS):
        beaten = jnp.zeros((1, tm), jnp.int32)
        for g2 in range(N_EXPERT_GROUPS):
            if g2 == g:
                continue
            beat = (gscore[g2] >= gscore[g]) if g2 < g else (gscore[g2] > gscore[g])
            beaten = beaten + beat.astype(jnp.int32)
        vals.append(jnp.where(beaten < TOPK_GROUPS, biased[g], NEG_INF))

    sel = [jnp.zeros((per_group, tm), _F32) for _ in range(N_EXPERT_GROUPS)]
    idxs, picked = [], []
    for _ in range(TOP_K):
        mx = vals[0]
        for g in range(1, N_EXPERT_GROUPS):
            mx = jnp.maximum(mx, vals[g])
        m = jnp.max(mx, axis=0, keepdims=True)
        ci = jnp.where(vals[0] == m, eids[0], n_exp)
        for g in range(1, N_EXPERT_GROUPS):
            ci = jnp.minimum(ci, jnp.where(vals[g] == m, eids[g], n_exp))
        idx = jnp.min(ci, axis=0, keepdims=True)
        sc_k = jnp.zeros((1, tm), _F32)
        for g in range(N_EXPERT_GROUPS):
            oh = eids[g] == idx
            sc_k = sc_k + jnp.sum(jnp.where(oh, scores[g], 0.0), axis=0, keepdims=True)
            vals[g] = jnp.where(oh, NEG_INF, vals[g])
            sel[g] = jnp.where(oh, 1.0, sel[g])
        idxs.append(idx)
        picked.append(sc_k)
    den = picked[0]
    for k in range(1, TOP_K):
        den = den + picked[k]

    sel_all = jnp.concatenate(sel, axis=0)
    before = jnp.dot(sel_all.astype(_BF16), tri_ref[...], preferred_element_type=_F32) + run_scr[:, 0:1]
    for k in range(TOP_K):
        rk = jnp.zeros((1, tm), _F32)
        for g in range(N_EXPERT_GROUPS):
            rs = slice(g * per_group, (g + 1) * per_group)
            rk = rk + jnp.sum(jnp.where(eids[g] == idxs[k], before[rs, :], 0.0), axis=0, keepdims=True)
        eidx_ref[k:k + 1, :] = idxs[k]
        wts_ref[k:k + 1, :] = picked[k] / den * ROUTED_SCALE
        rank_ref[k:k + 1, :] = rk.astype(jnp.int32)
    run_scr[...] = run_scr[...] + jnp.sum(sel_all, axis=1, keepdims=True)
    cnt_ref[...] = run_scr[...].astype(jnp.int32)

    hb = hb_scr[...]
    sg = jnp.dot(hb, swg_ref[...], preferred_element_type=_F32)
    su = jnp.dot(hb, swu_ref[...], preferred_element_type=_F32)
    sh_ref[...] = jnp.dot((_silu(sg) * su).astype(_BF16), swd_ref[...],
                          preferred_element_type=_F32).astype(_BF16)


def _router(x, mod_g, g_ffn, wr_t, br, tri, swg_bf, swu_bf, swd_bf, *, tm):
    t, d = x.shape
    gp = tm // GROUP
    n_exp = wr_t.shape[0]
    f = swg_bf.shape[1]
    kern = functools.partial(_router_kernel, gp=gp, d=d, n_exp=n_exp, tm=tm)
    const = lambda i: (0, 0)
    return pl.pallas_call(
        kern,
        grid=(t // tm,),
        in_specs=[pl.BlockSpec((tm, d), lambda i: (i, 0)),
                  pl.BlockSpec((gp, 3 * d), lambda i: (i, 1)),
                  pl.BlockSpec((1, d), const),
                  pl.BlockSpec((n_exp, d), const),
                  pl.BlockSpec((n_exp, 1), const),
                  pl.BlockSpec((tm, tm), const),
                  pl.BlockSpec((d, f), const),
                  pl.BlockSpec((d, f), const),
                  pl.BlockSpec((f, d), const)],
        out_specs=[pl.BlockSpec((tm, d), lambda i: (i, 0)),
                   pl.BlockSpec((tm, d), lambda i: (i, 0)),
                   pl.BlockSpec((TOP_K, tm), lambda i: (0, i)),
                   pl.BlockSpec((TOP_K, tm), lambda i: (0, i)),
                   pl.BlockSpec((TOP_K, tm), lambda i: (0, i)),
                   pl.BlockSpec((n_exp, 128), const)],
        out_shape=[jax.ShapeDtypeStruct((t, d), _F32),
                   jax.ShapeDtypeStruct((t, d), _BF16),
                   jax.ShapeDtypeStruct((TOP_K, t), jnp.int32),
                   jax.ShapeDtypeStruct((TOP_K, t), _F32),
                   jax.ShapeDtypeStruct((TOP_K, t), jnp.int32),
                   jax.ShapeDtypeStruct((n_exp, 128), jnp.int32)],
        scratch_shapes=[pltpu.VMEM((n_exp, 128), _F32),
                        pltpu.VMEM((tm, d), _BF16)],
        compiler_params=_cparams(1),
        name="router_shared",
    )(x, mod_g, g_ffn.reshape(1, d), wr_t, br.reshape(n_exp, 1), tri, swg_bf, swu_bf, swd_bf)


def _ffn_kernel(be_ref, nu_ref, idx_hbm, h_hbm, wg_ref, wu_ref, wd_ref, y_hbm,
                idx_s, xbuf, ybuf, wgb, wub, wdb, sem_i, sem_g, sem_s, *, bm):
    b = pl.program_id(0)
    nu = nu_ref[0]
    slot = lax.rem(b, 2)
    oslot = 1 - slot

    def idx_copy(blk, sl):
        return pltpu.make_async_copy(idx_hbm.at[blk], idx_s.at[sl], sem_i.at[sl])

    def issue_gather(sl):
        for r in range(bm):
            pltpu.make_async_copy(h_hbm.at[pl.ds(idx_s[sl, r], 1)], xbuf.at[sl, pl.ds(r, 1)],
                                  sem_g.at[sl]).start()

    def wait_gather(sl):
        pltpu.make_async_copy(h_hbm.at[pl.ds(0, bm)], xbuf.at[sl], sem_g.at[sl]).wait()

    def issue_scatter(sl):
        for r in range(bm):
            pltpu.make_async_copy(ybuf.at[sl, pl.ds(r, 1)], y_hbm.at[pl.ds(idx_s[sl, bm + r], 1)],
                                  sem_s.at[sl]).start()

    def wait_scatter(sl):
        pltpu.make_async_copy(ybuf.at[sl], y_hbm.at[pl.ds(0, bm)], sem_s.at[sl]).wait()

    @pl.when(b < nu)
    def _():
        @pl.when(b == 0)
        def _():
            ybuf[...] = jnp.zeros_like(ybuf)
            spare0 = y_hbm.shape[0] - 2 * bm
            fills = [pltpu.make_async_copy(ybuf.at[sl], y_hbm.at[pl.ds(spare0 + sl * bm, bm)], sem_s.at[sl])
                     for sl in range(2)]
            for c in fills:
                c.start()
            for c in fills:
                c.wait()
            first = idx_copy(0, 0)
            first.start()
            first.wait()
            issue_gather(0)

            @pl.when(nu > 1)
            def _():
                idx_copy(1, 1).start()

        wait_gather(slot)

        @pl.when(b + 1 < nu)
        def _():
            idx_copy(b + 1, oslot).wait()
            issue_gather(oslot)

        new_expert = (b == 0) | (be_ref[b] != be_ref[jnp.maximum(b - 1, 0)])

        @pl.when(new_expert)
        def _():
            wgb[...] = wg_ref[...].astype(_BF16)
            wub[...] = wu_ref[...].astype(_BF16)
            wdb[...] = wd_ref[...].astype(_BF16)

        @pl.when(b >= 2)
        def _():
            wait_scatter(slot)

        x = xbuf[slot].astype(_BF16)
        hg = jnp.dot(x, wgb[...], preferred_element_type=_F32)
        hu = jnp.dot(x, wub[...], preferred_element_type=_F32)
        hb = (_silu(hg) * hu).astype(_BF16)
        ybuf[slot] = jnp.dot(hb, wdb[...], preferred_element_type=_F32)
        issue_scatter(slot)

        @pl.when(b + 2 < nu)
        def _():
            idx_copy(b + 2, slot).start()

        @pl.when(b == nu - 1)
        def _():
            @pl.when(b >= 1)
            def _():
                wait_scatter(oslot)
            wait_scatter(slot)


def _routed_ffn(blk_e, n_used, idx, h2, wg, wu, wd, *, bm, n_out_rows):
    nb = idx.shape[0]
    t, d = h2.shape
    f = wg.shape[2]
    kern = functools.partial(_ffn_kernel, bm=bm)
    grid_spec = pltpu.PrefetchScalarGridSpec(
        num_scalar_prefetch=2,
        grid=(nb,),
        in_specs=[pl.BlockSpec(memory_space=pl.ANY),
                  pl.BlockSpec(memory_space=pl.ANY),
                  pl.BlockSpec((None, d, f), lambda b, be, nu: (be[b], 0, 0)),
                  pl.BlockSpec((None, d, f), lambda b, be, nu: (be[b], 0, 0)),
                  pl.BlockSpec((None, f, d), lambda b, be, nu: (be[b], 0, 0))],
        out_specs=pl.BlockSpec(memory_space=pl.ANY),
        scratch_shapes=[pltpu.SMEM((2, 2 * bm), jnp.int32),
                        pltpu.VMEM((2, bm, d), _F32),
                        pltpu.VMEM((2, bm, d), _F32),
                        pltpu.VMEM((d, f), _BF16),
                        pltpu.VMEM((d, f), _BF16),
                        pltpu.VMEM((f, d), _BF16),
                        pltpu.SemaphoreType.DMA((2,)),
                        pltpu.SemaphoreType.DMA((2,)),
                        pltpu.SemaphoreType.DMA((2,))])
    return pl.pallas_call(
        kern,
        grid_spec=grid_spec,
        out_shape=jax.ShapeDtypeStruct((n_out_rows, d), _F32),
        compiler_params=_cparams(1),
        name="routed_ffn",
    )(blk_e, n_used, idx, h2, wg, wu, wd)


def _combine_kernel(*refs, d, gpt, final_norm):
    x_ref, sh_ref, w_ref, mod_ref, gf_ref = refs[:5]
    y_refs = refs[5:5 + TOP_K]
    out_ref = refs[5 + TOP_K]
    i = pl.program_id(0)
    acc = sh_ref[...].astype(_F32)
    for k in range(TOP_K):
        acc = acc + w_ref[:, k:k + 1] * y_refs[k][...]
    groups_per_mod = mod_ref.shape[0]
    g0 = lax.rem(i, groups_per_mod // gpt) * gpt
    for s in range(gpt):
        sl = slice(s * GROUP, (s + 1) * GROUP)
        gate = mod_ref[pl.ds(g0 + s, 1), 2 * d:3 * d]
        xo = x_ref[sl, :] + gate * acc[sl, :]
        if final_norm:
            ms = jnp.mean(xo * xo, axis=-1, keepdims=True)
            xo = (xo * lax.rsqrt(ms + EPS)) * gf_ref[...]
        out_ref[sl, :] = xo


def _combine(x, shared, wts, mod_g, g_final, y, *, tm, final_norm):
    t, d = x.shape
    tmc = min(256, tm)
    gpt = tmc // GROUP
    gp = tm // GROUP
    nt = t // tmc
    kern = functools.partial(_combine_kernel, d=d, gpt=gpt, final_norm=final_norm)
    y_specs = [pl.BlockSpec((tmc, d), functools.partial(lambda i, k: (k * nt + i, 0), k=k))
               for k in range(TOP_K)]
    return pl.pallas_call(
        kern,
        grid=(nt,),
        in_specs=[pl.BlockSpec((tmc, d), lambda i: (i, 0)),
                  pl.BlockSpec((tmc, d), lambda i: (i, 0)),
                  pl.BlockSpec((tmc, TOP_K), lambda i: (i, 0)),
                  pl.BlockSpec((gp, 3 * d), lambda i: (i // (tm // tmc), 1)),
                  pl.BlockSpec((1, d), lambda i: (0, 0))] + y_specs,
        out_specs=pl.BlockSpec((tmc, d), lambda i: (i, 0)),
        out_shape=jax.ShapeDtypeStruct((t, d), _F32),
        compiler_params=_cparams(1),
        name="combine_final" if final_norm else "combine",
    )(x, shared, wts, mod_g, g_final.reshape(1, d), *([y] * TOP_K))


def _dispatch_tables(eidx, rank, counts, *, bm, nb, t):
    n_exp = counts.shape[0]
    padded = (counts + bm - 1) // bm * bm
    pends = jnp.cumsum(padded)
    pstarts = pends - padded
    slot = (pstarts[eidx] + rank).reshape(-1)
    tok = jnp.broadcast_to(jnp.arange(t, dtype=jnp.int32)[None, :], (TOP_K, t))
    dest = (jnp.arange(TOP_K, dtype=jnp.int32)[:, None] * t + tok).reshape(-1)
    r = jnp.arange(nb * bm, dtype=jnp.int32)
    src0 = r % t
    dst0 = TOP_K * t + ((r // bm) % 2) * bm + (r % bm)
    row_src = src0.at[slot].set(tok.reshape(-1), unique_indices=True)
    row_dst = dst0.at[slot].set(dest, unique_indices=True)
    idx = jnp.concatenate([row_src.reshape(nb, bm), row_dst.reshape(nb, bm)], axis=1)
    blk_e = jnp.minimum(jnp.searchsorted(pends, jnp.arange(nb, dtype=jnp.int32) * bm, side="right"),
                        n_exp - 1).astype(jnp.int32)
    n_used = (pends[-1] // bm).astype(jnp.int32).reshape(1)
    return blk_e, n_used, idx


def _moe(x, mod_g, g_ffn, g_final, router_w, router_b, wg, wu, wd, swg, swu, swd, tri,
         *, tm, bm, final_norm):
    t, d = x.shape
    n_exp = router_w.shape[1]
    h2, shared, eidx, wts, rank, cnt = _router(
        x, mod_g, g_ffn, router_w.T, router_b, tri,
        swg.astype(_BF16), swu.astype(_BF16), swd.astype(_BF16), tm=tm)
    nb = (t * TOP_K) // bm + n_exp
    blk_e, n_used, idx = _dispatch_tables(eidx, rank, cnt[:, 0], bm=bm, nb=nb, t=t)
    y = _routed_ffn(blk_e, n_used, idx, h2, wg, wu, wd, bm=bm, n_out_rows=TOP_K * t + 2 * bm)
    return _combine(x, shared, wts.T, mod_g, g_final, y, tm=tm, final_norm=final_norm)


def _tile_rows(seq, dec_rows):
    for tm in (512, 256, 128, 64):
        if seq % tm == 0 and dec_rows % tm == 0:
            return tm
    raise ValueError("sequence lengths must be multiples of 64")


def kernel(x_prompt, x_sample, state_conv, state_pool, c_prompt, c_sample, w_mod, b_mod, g_mix, g_ffn, g_final, conv_w_in, conv_w, conv_w_out, pool_w, pool_scale, router_w, router_b, exp_w_gate, exp_w_up, exp_w_down, sh_w_gate, sh_w_up, sh_w_down):
    bp, seq, d = x_prompt.shape
    bs, dseq, _ = x_sample.shape
    depth = w_mod.shape[0]
    n_exp = router_w.shape[2]
    f = exp_w_gate.shape[3]
    assert dseq == GROUP and seq % GROUP == 0 and depth == 2
    assert state_conv.shape[1] == CONV_W - 1 and state_pool.shape[1] == POOL_HDR - 1
    assert n_exp % N_EXPERT_GROUPS == 0 and d % (128 * len(POOL_WINDOWS)) == 0
    tp, ts = bp * seq, bs * dseq
    t = tp + ts
    tm = _tile_rows(seq, ts)
    bm = min(256, t * TOP_K // n_exp)
    assert (t * TOP_K) % bm == 0
    n_prompt_tiles, tiles_per_seq = tp // tm, seq // tm
    ng_p, ng = tp // GROUP, t // GROUP

    x = jnp.concatenate([x_prompt.reshape(tp, d), x_sample.reshape(ts, d)], axis=0)

    nb_pad = -(-(bp + bs) // 8) * 8
    c_all = jnp.concatenate([c_prompt, c_sample, jnp.zeros((nb_pad - bp - bs, d), _F32)], axis=0)
    mods = _mod_rows(c_all, w_mod, b_mod)
    gps = seq // GROUP
    mod_p = jnp.broadcast_to(mods[:, :bp, None, :], (depth, bp, gps, 6 * d)).reshape(depth, bp * gps, 6 * d)
    mod_g = jnp.concatenate([mod_p, mods[:, bp:bp + bs, :]], axis=1)

    conv_tbl = jnp.zeros((ng, CONV_HDR, d), _F32).at[ng_p:, CONV_HDR - (CONV_W - 1):, :].set(state_conv)
    pool_tbl = jnp.zeros((ng, POOL_HDR, d), _F32).at[ng_p:, 1:, :].set(state_pool)
    tri = (np.arange(tm)[:, None] < np.arange(tm)[None, :]).astype(np.float32)
    tri = jnp.asarray(tri, _BF16)

    statics = dict(tm=tm, n_prompt_tiles=n_prompt_tiles, tiles_per_seq=tiles_per_seq)
    x, conv_tail = _conv_mixer(x, mod_g[0], conv_tbl, g_mix[0], conv_w_in.astype(_BF16), conv_w,
                               conv_w_out.astype(_BF16), **statics)
    x = _moe(x, mod_g[0], g_ffn[0], g_final, router_w[0], router_b[0], exp_w_gate[0], exp_w_up[0],
             exp_w_down[0], sh_w_gate[0], sh_w_up[0], sh_w_down[0], tri, tm=tm, bm=bm, final_norm=False)
    x, pool_tail = _pool_mixer(x, mod_g[1], pool_tbl, g_mix[1], pool_w.astype(_BF16), pool_scale, **statics)
    x = _moe(x, mod_g[1], g_ffn[1], g_final, router_w[1], router_b[1], exp_w_gate[1], exp_w_up[1],
             exp_w_down[1], sh_w_gate[1], sh_w_up[1], sh_w_down[1], tri, tm=tm, bm=bm, final_norm=True)

    conv_keep =conv_tail[:, CONV_HDR - (CONV_W - 1):, :]
    pool_keep = pool_tail[:, 1:, :]
    return (x[:tp].reshape(bp, seq, d),
            x[tp:].reshape(bs, dseq, d),
            conv_keep[gps - 1:ng_p:gps],
            conv_keep[ng_p:],
            pool_keep[gps - 1:ng_p:gps],
            pool_keep[ng_p:])
```

```python
import functools

import jax
import jax.numpy as jnp
from jax import lax
from jax.experimental import pallas as pl
from jax.experimental.pallas import tpu as pltpu

GROUP = 64
CONV_W = 3
POOL_WINDOWS = (2, 4, 8, 16)
POOL_HDR = 16
CONV_HDR = 8
TOP_K = 8
N_EXPERT_GROUPS = 8
TOPK_GROUPS = 4
ROUTED_SCALE = 2.5
PAST_LEN = 4096
EPS = 1e-6
VMEM_LIMIT_V7X = 56 * 1024 * 1024
NEG_INF = float("-inf")

_F32 = jnp.float32
_BF16 = jnp.bfloat16


def _silu(v):
    return v * jax.nn.sigmoid(v)


def _cparams(n_axes):
    return pltpu.CompilerParams(dimension_semantics=("arbitrary",) * n_axes,
                                vmem_limit_bytes=VMEM_LIMIT_V7X)


def _mod_kernel(c_ref, w_ref, b_ref, o_ref):
    c = c_ref[...]
    ca = _silu(c).astype(_BF16)
    o_ref[...] = jnp.dot(ca, w_ref[...].astype(_BF16), preferred_element_type=_F32) + b_ref[...]


def _mod_rows(c_all, w_mod, b_mod):
    depth, d, n6 = w_mod.shape
    nb = c_all.shape[0]
    tn = min(1024, n6)
    return pl.pallas_call(
        _mod_kernel,
        grid=(depth, n6 // tn),
        in_specs=[pl.BlockSpec((nb, d), lambda l, j: (0, 0)),
                  pl.BlockSpec((None, d, tn), lambda l, j: (l, 0, j)),
                  pl.BlockSpec((None, 1, tn), lambda l, j: (l, 0, j))],
        out_specs=pl.BlockSpec((None, nb, tn), lambda l, j: (l, 0, j)),
        out_shape=jax.ShapeDtypeStruct((depth, nb, n6), _F32),
        compiler_params=_cparams(2),
        name="adaln_rows",
    )(c_all, w_mod, b_mod.reshape(depth, 1, n6))


def _norm_mod_group(xs, mod_ref, g_ref, s, d):
    ms = jnp.mean(xs * xs, axis=-1, keepdims=True)
    xn = (xs * lax.rsqrt(ms + EPS)) * g_ref[...]
    return xn * (1.0 + mod_ref[s:s + 1, d:2 * d]) + mod_ref[s:s + 1, 0:d]


def _conv_mixer_kernel(xp_ref, xs_ref, mod_ref, st_ref, g_ref, wb_ref, wc_ref, wv_ref, cw_ref, wo_ref,
                       out_ref, tail_ref, h_scr, acc_scr, ubuf, gbuf, carry,
                       *, gp, d, n_prompt_tiles, tiles_per_seq):
    i = pl.program_id(0)
    j = pl.program_id(1)
    nj = pl.num_programs(1)
    is_sample = i >= n_prompt_tiles

    def x_rows(s):
        sl = slice(s * GROUP, (s + 1) * GROUP)
        return jnp.where(is_sample, xs_ref[sl, :], xp_ref[sl, :])

    @pl.when(j == 0)
    def _():
        for s in range(gp):
            h_scr[s * GROUP:(s + 1) * GROUP, :] = _norm_mod_group(x_rows(s), mod_ref, g_ref, s, d).astype(_BF16)
        acc_scr[...] = jnp.zeros_like(acc_scr)

    @pl.when(i == 0)
    def _():
        carry[j] = jnp.zeros((CONV_HDR, carry.shape[2]), _F32)

    hb = h_scr[...]
    bq = jnp.dot(hb, wb_ref[...], preferred_element_type=_F32)
    cq = jnp.dot(hb, wc_ref[...], preferred_element_type=_F32)
    vq = jnp.dot(hb, wv_ref[...], preferred_element_type=_F32)
    u = cq * vq

    stride = GROUP + CONV_HDR
    for s in range(gp):
        if s == 0:
            prev = jnp.where(lax.rem(i, tiles_per_seq) == 0, 0.0, carry[j])
        else:
            prev = u[s * GROUP - CONV_HDR:s * GROUP, :]
        ubuf[s * stride:s * stride + CONV_HDR, :] = jnp.where(is_sample, st_ref[s], prev)
        ubuf[s * stride + CONV_HDR:(s + 1) * stride, :] = u[s * GROUP:(s + 1) * GROUP, :]
        tail_ref[s] = u[(s + 1) * GROUP - (CONV_W - 1):(s + 1) * GROUP, :]
    carry[j] = u[gp * GROUP - CONV_HDR:gp * GROUP, :]

    w0 = cw_ref[0:1, :]
    w1 = cw_ref[1:2, :]
    w2 = cw_ref[2:3, :]
    for s in range(gp):
        base = s * stride + CONV_HDR
        u0 = ubuf[base:base + GROUP, :]
        u1 = ubuf[base - 1:base - 1 + GROUP, :]
        u2 = ubuf[base - 2:base - 2 + GROUP, :]
        conv = u2 * w0 + u1 * w1 + u0 * w2
        gbuf[s * GROUP:(s + 1) * GROUP, :] = (bq[s * GROUP:(s + 1) * GROUP, :] * conv).astype(_BF16)
    acc_scr[...] += jnp.dot(gbuf[...], wo_ref[...], preferred_element_type=_F32)

    @pl.when(j == nj - 1)
    def _():
        for s in range(gp):
            sl = slice(s * GROUP, (s + 1) * GROUP)
            out_ref[sl, :] = x_rows(s) + mod_ref[s:s + 1, 2 * d:3 * d] * acc_scr[sl, :]


def _conv_mixer(xp, xs, mod_g, st_tbl, g_mix, w_in_bf, conv_w, w_out_bf, *, tm, n_prompt_tiles, tiles_per_seq):
    d = xp.shape[1]
    t = xp.shape[0] + xs.shape[0]
    npt = n_prompt_tiles
    gp = tm // GROUP
    tn = min(512, d)
    nj = d // tn
    kern = functools.partial(_conv_mixer_kernel, gp=gp, d=d, n_prompt_tiles=n_prompt_tiles,
                             tiles_per_seq=tiles_per_seq)
    return pl.pallas_call(
        kern,
        grid=(t // tm, nj),
        in_specs=[pl.BlockSpec((tm, d), lambda i, j: (jnp.minimum(i, npt - 1), 0)),
                  pl.BlockSpec((tm, d), lambda i, j: (jnp.maximum(i - npt, 0), 0)),
                  pl.BlockSpec((gp, 3 * d), lambda i, j: (i, 0)),
                  pl.BlockSpec((gp, CONV_HDR, tn), lambda i, j: (jnp.maximum(i - npt, 0), 0, j)),
                  pl.BlockSpec((1, d), lambda i, j: (0, 0)),
                  pl.BlockSpec((d, tn), lambda i, j: (0, j)),
                  pl.BlockSpec((d, tn), lambda i, j: (0, nj + j)),
                  pl.BlockSpec((d, tn), lambda i, j: (0, 2 * nj + j)),
                  pl.BlockSpec((CONV_W, tn), lambda i, j: (0, j)),
                  pl.BlockSpec((tn, d), lambda i, j: (j, 0))],
        out_specs=[pl.BlockSpec((tm, d), lambda i, j: (i, 0)),
                   pl.BlockSpec((gp, CONV_W - 1, tn), lambda i, j: (i, 0, j))],
        out_shape=[jax.ShapeDtypeStruct((t, d), _F32),
                   jax.ShapeDtypeStruct((t // GROUP, CONV_W - 1, d), _F32)],
        scratch_shapes=[pltpu.VMEM((tm, d), _BF16),
                        pltpu.VMEM((tm, d), _F32),
                        pltpu.VMEM((gp * (GROUP + CONV_HDR), tn), _F32),
                        pltpu.VMEM((tm, tn), _BF16),
                        pltpu.VMEM((nj, CONV_HDR, tn), _F32)],
        compiler_params=_cparams(2),
        name="conv_mixer",
    )(xp, xs, mod_g, st_tbl, g_mix.reshape(1, d), w_in_bf, w_in_bf, w_in_bf, conv_w, w_out_bf)


def _pool_mixer_kernel(x_ref, mod_ref, st_ref, g_ref, pw_ref, ps_ref, out_ref, tail_ref,
                       dbuf, carry, *, gp, d, n_prompt_tiles, tiles_per_seq, tm):
    i = pl.program_id(0)
    pg = d // len(POOL_WINDOWS)
    is_sample = i >= n_prompt_tiles
    row = lax.broadcasted_iota(jnp.int32, (GROUP, 1), 0)

    @pl.when(i == 0)
    def _():
        carry[...] = jnp.zeros_like(carry)

    prev = jnp.where(lax.rem(i, tiles_per_seq) == 0, 0.0, carry[...])
    for s in range(gp):
        h = _norm_mod_group(x_ref[s * GROUP:(s + 1) * GROUP, :], mod_ref, g_ref, s, d)
        hdr = jnp.where(is_sample, st_ref[s], prev)
        he = jnp.concatenate([hdr, h], axis=0)
        pos0 = jnp.where(is_sample, PAST_LEN, lax.rem(i, tiles_per_seq) * tm + s * GROUP)
        pos1 = pos0 + row + 1
        for q, w in enumerate(POOL_WINDOWS):
            cs = slice(q * pg, (q + 1) * pg)
            sw = he[:, cs]
            span = 1
            while span < w:
                sw = sw + pltpu.roll(sw, span, 0)
                span *= 2
            inv_cnt = 1.0 / jnp.minimum(pos1, w).astype(_F32)
            pooled = sw[POOL_HDR:, :] * inv_cnt
            dbuf[s * GROUP:(s + 1) * GROUP, cs] = (pooled - h[:, cs]).astype(_BF16)
        prev = h[GROUP - POOL_HDR:, :]
        tail_ref[s] = h[GROUP - (POOL_HDR - 1):, :]
    carry[...] = prev
    for q in range(len(POOL_WINDOWS)):
        cs = slice(q * pg, (q + 1) * pg)
        y = jnp.dot(dbuf[:, cs], pw_ref[q], preferred_element_type=_F32) * ps_ref[:, cs]
        for s in range(gp):
            sl = slice(s * GROUP, (s + 1) * GROUP)
            out_ref[sl, cs] = x_ref[sl, cs] + mod_ref[s:s + 1, 2 * d + q * pg:2 * d + (q + 1) * pg] * y[sl, :]


def _pool_mixer(x, mod_g, st_tbl, g_mix, pool_w_bf, pool_scale, *, tm, n_prompt_tiles, tiles_per_seq):
    t, d = x.shape
    gp = tm // GROUP
    nq = len(POOL_WINDOWS)
    pg = d // nq
    kern = functools.partial(_pool_mixer_kernel, gp=gp, d=d, n_prompt_tiles=n_prompt_tiles,
                             tiles_per_seq=tiles_per_seq, tm=tm)
    return pl.pallas_call(
        kern,
        grid=(t // tm,),
        in_specs=[pl.BlockSpec((tm, d), lambda i: (i, 0)),
                  pl.BlockSpec((gp, 3 * d), lambda i: (i, 0)),
                  pl.BlockSpec((gp, POOL_HDR, d), lambda i: (jnp.maximum(i - n_prompt_tiles, 0), 0, 0)),
                  pl.BlockSpec((1, d), lambda i: (0, 0)),
                  pl.BlockSpec((nq, pg, pg), lambda i: (0, 0, 0)),
                  pl.BlockSpec((1, d), lambda i: (0, 0))],
        out_specs=[pl.BlockSpec((tm, d), lambda i: (i, 0)),
                   pl.BlockSpec((gp, POOL_HDR - 1, d), lambda i: (i, 0, 0))],
        out_shape=[jax.ShapeDtypeStruct((t, d), _F32),
                   jax.ShapeDtypeStruct((t // GROUP, POOL_HDR - 1, d), _F32)],
        scratch_shapes=[pltpu.VMEM((tm, d), _BF16),
                        pltpu.VMEM((POOL_HDR, d), _F32)],
        compiler_params=_cparams(1),
        name="pool_mixer",
    )(x, mod_g, st_tbl, g_mix.reshape(1, d), pool_w_bf, pool_scale.reshape(1, d))


def _router_kernel(x_ref, mod_ref, g_ref, wr_ref, br_ref, swg_ref, swu_ref, swd_ref,
                   h_ref, sh_ref, eidx_ref, wts_ref, cnt_ref, run_scr, hb_scr,
                   *, gp, d, n_exp, tm):
    i = pl.program_id(0)
    per_group = n_exp // N_EXPERT_GROUPS

    @pl.when(i == 0)
    def _():
        run_scr[...] = jnp.zeros_like(run_scr)

    for s in range(gp):
        sl = slice(s * GROUP, (s + 1) * GROUP)
        h = _norm_mod_group(x_ref[sl, :], mod_ref, g_ref, s, d)
        h_ref[sl, :] = h
        hb_scr[sl, :] = h.astype(_BF16)

    logits = lax.dot_general(wr_ref[...], h_ref[...], (((1,), (1,)), ((), ())),
                             precision=lax.Precision.HIGHEST, preferred_element_type=_F32)
    sub = lax.broadcasted_iota(jnp.int32, (per_group, tm), 0)
    scores, biased, eids = [], [], []
    for g in range(N_EXPERT_GROUPS):
        rs = slice(g * per_group, (g + 1) * per_group)
        sc = jax.nn.sigmoid(logits[rs, :])
        scores.append(sc)
        biased.append(sc + br_ref[rs, :])
        eids.append(sub + g * per_group)

    gscore = []
    for g in range(N_EXPERT_GROUPS):
        m1 = jnp.max(biased[g], axis=0, keepdims=True)
        i1 = jnp.min(jnp.where(biased[g] == m1, sub, per_group), axis=0, keepdims=True)
        m2 = jnp.max(jnp.where(sub == i1, NEG_INF, biased[g]), axis=0, keepdims=True)
        gscore.append(m1 + m2)
    vals = []
    for g in range(N_EXPERT_GROUPS):
        beaten = jnp.zeros((1, tm), jnp.int32)
        for g2 in range(N_EXPERT_GROUPS):
            if g2 == g:
                continue
            beat = (gscore[g2] >= gscore[g]) if g2 < g else (gscore[g2] > gscore[g])
            beaten = beaten + beat.astype(jnp.int32)
        vals.append(jnp.where(beaten < TOPK_GROUPS, biased[g], NEG_INF))

    sel = [jnp.zeros((per_group, tm), _F32) for _ in range(N_EXPERT_GROUPS)]
    idxs, picked = [], []
    for _ in range(TOP_K):
        mx = vals[0]
        for g in range(1, N_EXPERT_GROUPS):
            mx = jnp.maximum(mx, vals[g])
        m = jnp.max(mx, axis=0, keepdims=True)
        ci = jnp.where(vals[0] == m, eids[0], n_exp)
        for g in range(1, N_EXPERT_GROUPS):
            ci = jnp.minimum(ci, jnp.where(vals[g] == m, eids[g], n_exp))
        idx = jnp.min(ci, axis=0, keepdims=True)
        sc_k = jnp.zeros((1, tm), _F32)
        for g in range(N_EXPERT_GROUPS):
            oh = eids[g] == idx
            sc_k = sc_k + jnp.sum(jnp.where(oh, scores[g], 0.0), axis=0, keepdims=True)
            vals[g] = jnp.where(oh, NEG_INF, vals[g])
            sel[g] = jnp.where(oh, 1.0, sel[g])
        idxs.append(idx)
        picked.append(sc_k)
    den = picked[0]
    for k in range(1, TOP_K):
        den = den + picked[k]

    for k in range(TOP_K):
        eidx_ref[k:k + 1, :] = idxs[k]
        wts_ref[k:k + 1, :] = picked[k] / den * ROUTED_SCALE
    sel_all = jnp.concatenate(sel, axis=0)
    run_scr[...] = run_scr[...] + jnp.sum(sel_all, axis=1, keepdims=True)
    cnt_ref[...] = run_scr[...].astype(jnp.int32)

    hb = hb_scr[...]
    sg = jnp.dot(hb, swg_ref[...], preferred_element_type=_F32)
    su = jnp.dot(hb, swu_ref[...], preferred_element_type=_F32)
    sh_ref[...] = jnp.dot((_silu(sg) * su).astype(_BF16), swd_ref[...],
                          preferred_element_type=_F32).astype(_BF16)


def _router(x, mod_g, g_ffn, wr_t, br, swg_bf, swu_bf, swd_bf, *, tm):
    t, d = x.shape
    gp = tm // GROUP
    n_exp = wr_t.shape[0]
    f = swg_bf.shape[1]
    kern = functools.partial(_router_kernel, gp=gp, d=d, n_exp=n_exp, tm=tm)
    const = lambda i: (0, 0)
    return pl.pallas_call(
        kern,
        grid=(t // tm,),
        in_specs=[pl.BlockSpec((tm, d), lambda i: (i, 0)),
                  pl.BlockSpec((gp, 3 * d), lambda i: (i, 1)),
                  pl.BlockSpec((1, d), const),
                  pl.BlockSpec((n_exp, d), const),
                  pl.BlockSpec((n_exp, 1), const),
                  pl.BlockSpec((d, f), const),
                  pl.BlockSpec((d, f), const),
                  pl.BlockSpec((f, d), const)],
        out_specs=[pl.BlockSpec((tm, d), lambda i: (i, 0)),
                   pl.BlockSpec((tm, d), lambda i: (i, 0)),
                   pl.BlockSpec((TOP_K, tm), lambda i: (0, i)),
                   pl.BlockSpec((TOP_K, tm), lambda i: (0, i)),
                   pl.BlockSpec((n_exp, 128), const)],
        out_shape=[jax.ShapeDtypeStruct((t, d), _F32),
                   jax.ShapeDtypeStruct((t, d), _BF16),
                   jax.ShapeDtypeStruct((TOP_K, t), jnp.int32),
                   jax.ShapeDtypeStruct((TOP_K, t), _F32),
                   jax.ShapeDtypeStruct((n_exp, 128), jnp.int32)],
        scratch_shapes=[pltpu.VMEM((n_exp, 128), _F32),
                        pltpu.VMEM((tm, d), _BF16)],
        compiler_params=_cparams(1),
        name="router_shared",
    )(x, mod_g, g_ffn.reshape(1, d), wr_t, br.reshape(n_exp, 1), swg_bf, swu_bf, swd_bf)


def _ffn_kernel(be_ref, nu_ref, idx_hbm, h_hbm, wg_ref, wu_ref, wd_ref, y_hbm,
                idx_s, xbuf, ybuf, wgb, wub, wdb, sem_i, sem_g, sem_s, *, bm):
    b = pl.program_id(0)
    nu = nu_ref[0]
    slot = lax.rem(b, 2)
    oslot = 1 - slot

    def idx_copy(blk, sl):
        return pltpu.make_async_copy(idx_hbm.at[blk], idx_s.at[sl], sem_i.at[sl])

    def issue_gather(sl):
        for r in range(bm):
            pltpu.make_async_copy(h_hbm.at[pl.ds(idx_s[sl, r], 1)], xbuf.at[sl, pl.ds(r, 1)],
                                  sem_g.at[sl]).start()

    def wait_gather(sl):
        pltpu.make_async_copy(h_hbm.at[pl.ds(0, bm)], xbuf.at[sl], sem_g.at[sl]).wait()

    def issue_scatter(sl):
        for r in range(bm):
            pltpu.make_async_copy(ybuf.at[sl, pl.ds(r, 1)], y_hbm.at[pl.ds(idx_s[sl, bm + r], 1)],
                                  sem_s.at[sl]).start()

    def wait_scatter(sl):
        pltpu.make_async_copy(ybuf.at[sl], y_hbm.at[pl.ds(0, bm)], sem_s.at[sl]).wait()

    @pl.when(b < nu)
    def _():
        @pl.when(b == 0)
        def _():
            ybuf[...] = jnp.zeros_like(ybuf)
            spare0 = y_hbm.shape[0] - 2 * bm
            fills = [pltpu.make_async_copy(ybuf.at[sl], y_hbm.at[pl.ds(spare0 + sl * bm, bm)], sem_s.at[sl])
                     for sl in range(2)]
            for c in fills:
                c.start()
            for c in fills:
                c.wait()
            first = idx_copy(0, 0)
            first.start()
            first.wait()
            issue_gather(0)

            @pl.when(nu > 1)
            def _():
                idx_copy(1, 1).start()

        wait_gather(slot)

        @pl.when(b + 1 < nu)
        def _():
            idx_copy(b + 1, oslot).wait()
            issue_gather(oslot)

        new_expert = (b == 0) | (be_ref[b] != be_ref[jnp.maximum(b - 1, 0)])

        @pl.when(new_expert)
        def _():
            wgb[...] = wg_ref[...].astype(_BF16)
            wub[...] = wu_ref[...].astype(_BF16)
            wdb[...] = wd_ref[...].astype(_BF16)

        @pl.when(b >= 2)
        def _():
            wait_scatter(slot)

        x = xbuf[slot].astype(_BF16)
        hg = jnp.dot(x, wgb[...], preferred_element_type=_F32)
        hu = jnp.dot(x, wub[...], preferred_element_type=_F32)
        hb = (_silu(hg) * hu).astype(_BF16)
        ybuf[slot] = jnp.dot(hb, wdb[...], preferred_element_type=_F32)
        issue_scatter(slot)

        @pl.when(b + 2 < nu)
        def _():
            idx_copy(b + 2, slot).start()

        @pl.when(b == nu - 1)
        def _():
            @pl.when(b >= 1)
            def _():
                wait_scatter(oslot)
            wait_scatter(slot)


def _routed_ffn(blk_e, n_used, idx, h2, wg, wu, wd, *, bm, n_out_rows):
    nb = idx.shape[0]
    t, d = h2.shape
    f = wg.shape[2]
    kern = functools.partial(_ffn_kernel, bm=bm)
    grid_spec = pltpu.PrefetchScalarGridSpec(
        num_scalar_prefetch=2,
        grid=(nb,),
        in_specs=[pl.BlockSpec(memory_space=pl.ANY),
                  pl.BlockSpec(memory_space=pl.ANY),
                  pl.BlockSpec((None, d, f), lambda b, be, nu: (be[b], 0, 0)),
                  pl.BlockSpec((None, d, f), lambda b, be, nu: (be[b], 0, 0)),
                  pl.BlockSpec((None, f, d), lambda b, be, nu: (be[b], 0, 0))],
        out_specs=pl.BlockSpec(memory_space=pl.ANY),
        scratch_shapes=[pltpu.SMEM((2, 2 * bm), jnp.int32),
                        pltpu.VMEM((2, bm, d), _F32),
                        pltpu.VMEM((2, bm, d), _F32),
                        pltpu.VMEM((d, f), _BF16),
                        pltpu.VMEM((d, f), _BF16),
                        pltpu.VMEM((f, d), _BF16),
                        pltpu.SemaphoreType.DMA((2,)),
                        pltpu.SemaphoreType.DMA((2,)),
                        pltpu.SemaphoreType.DMA((2,))])
    return pl.pallas_call(
        kern,
        grid_spec=grid_spec,
        out_shape=jax.ShapeDtypeStruct((n_out_rows, d), _F32),
        compiler_params=_cparams(1),
        name="routed_ffn",
    )(blk_e, n_used, idx, h2, wg, wu, wd)


def _combine_kernel(*refs, d, gpt, n_prompt_tiles):
    x_ref, sh_ref, w_ref, mod_ref, gf_ref = refs[:5]
    y_refs = refs[5:5 + TOP_K]
    out_refs = refs[5 + TOP_K:]
    i = pl.program_id(0)
    acc = sh_ref[...].astype(_F32)
    for k in range(TOP_K):
        acc = acc + w_ref[:, k:k + 1] * y_refs[k][...]
    groups_per_mod = mod_ref.shape[0]
    g0 = lax.rem(i, groups_per_mod // gpt) * gpt
    rows = []
    for s in range(gpt):
        sl = slice(s * GROUP, (s + 1) * GROUP)
        gate = mod_ref[pl.ds(g0 + s, 1), 2 * d:3 * d]
        xo = x_ref[sl, :] + gate * acc[sl, :]
        if n_prompt_tiles is not None:
            ms = jnp.mean(xo * xo, axis=-1, keepdims=True)
            xo = (xo * lax.rsqrt(ms + EPS)) * gf_ref[...]
        rows.append(xo)
    if n_prompt_tiles is None:
        for s in range(gpt):
            out_refs[0][s * GROUP:(s + 1) * GROUP, :] = rows[s]
    else:
        for o_ref, active in ((out_refs[0], i < n_prompt_tiles), (out_refs[1], i >= n_prompt_tiles)):
            @pl.when(active)
            def _(o_ref=o_ref):
                for s in range(gpt):
                    o_ref[s * GROUP:(s + 1) * GROUP, :] = rows[s]


def _combine(x, shared, wts, mod_g, g_final, y, *, tm, split_rows=None):
    t, d = x.shape
    tmc = min(256, tm)
    gpt = tmc // GROUP
    gp = tm // GROUP
    nt = t // tmc
    row_spec = pl.BlockSpec((tmc, d), lambda i: (i, 0))
    if split_rows is None:
        npt = None
        out_specs = row_spec
        out_shape = jax.ShapeDtypeStruct((t, d), _F32)
    else:
        npt = split_rows[0] // tmc
        out_specs = [pl.BlockSpec((tmc, d), lambda i: (jnp.minimum(i, npt - 1), 0)),
                     pl.BlockSpec((tmc, d), lambda i: (jnp.maximum(i - npt, 0), 0))]
        out_shape = [jax.ShapeDtypeStruct((rows, d), _F32) for rows in split_rows]
    kern = functools.partial(_combine_kernel, d=d, gpt=gpt, n_prompt_tiles=npt)
    y_specs = [pl.BlockSpec((tmc, d), functools.partial(lambda i, k: (k * nt + i, 0), k=k))
               for k in range(TOP_K)]
    return pl.pallas_call(
        kern,
        grid=(nt,),
        in_specs=[row_spec,
                  row_spec,
                  pl.BlockSpec((tmc, TOP_K), lambda i: (i, 0)),
                  pl.BlockSpec((gp, 3 * d), lambda i: (i // (tm // tmc), 1)),
                  pl.BlockSpec((1, d), lambda i: (0, 0))] + y_specs,
        out_specs=out_specs,
        out_shape=out_shape,
        compiler_params=_cparams(1),
        name="combine" if split_rows is None else "combine_final",
    )(x, shared, wts, mod_g, g_final.reshape(1, d), *([y] * TOP_K))


def _dispatch_tables(eidx, counts, *, bm, nb, t):
    n_exp = counts.shape[0]
    low_bits = (t * TOP_K).bit_length()
    low_mask = (1 << low_bits) - 1
    assert t * TOP_K < low_mask and (n_exp + 1) << low_bits < 2 ** 31
    tok_k = jnp.arange(t, dtype=jnp.int32)[None, :] * TOP_K + jnp.arange(TOP_K, dtype=jnp.int32)[:, None]
    real = (eidx << low_bits) + tok_k
    n_pad = (-counts) % bm
    e_col = jnp.arange(n_exp, dtype=jnp.int32)[:, None]
    pad = jnp.where(jnp.arange(bm, dtype=jnp.int32)[None, :] < n_pad[:, None],
                    (e_col << low_bits) + low_mask, n_exp << low_bits)
    keys = jnp.sort(jnp.concatenate([real.reshape(-1), pad.reshape(-1)]))
    low = keys & low_mask
    is_pad = low == low_mask
    tok = low // TOP_K
    r = jnp.arange(nb * bm, dtype=jnp.int32)
    row_src = jnp.where(is_pad, r % t, tok)
    row_dst = jnp.where(is_pad, TOP_K * t + ((r // bm) % 2) * bm + (r % bm),
                        (low % TOP_K) * t + tok)
    idx = jnp.concatenate([row_src.reshape(nb, bm), row_dst.reshape(nb, bm)], axis=1)
    blk_e = jnp.minimum(keys.reshape(nb, bm)[:, 0] >> low_bits, n_exp - 1)
    n_used = (jnp.sum(counts + n_pad) // bm).astype(jnp.int32).reshape(1)
    return blk_e, n_used, idx


def _moe(x, mod_g, g_ffn, g_final, router_w, router_b, wg, wu, wd, swg, swu, swd,
         *, tm, bm, split_rows=None):
    t, d = x.shape
    n_exp = router_w.shape[1]
    h2, shared, eidx, wts, cnt = _router(
        x, mod_g, g_ffn, router_w.T, router_b,
        swg.astype(_BF16), swu.astype(_BF16), swd.astype(_BF16), tm=tm)
    nb = (t * TOP_K) // bm + n_exp
    blk_e, n_used, idx = _dispatch_tables(eidx, cnt[:, 0], bm=bm, nb=nb, t=t)
    y = _routed_ffn(blk_e, n_used, idx, h2, wg, wu, wd, bm=bm, n_out_rows=TOP_K * t + 2 * bm)
    return _combine(x, shared, wts.T, mod_g, g_final, y, tm=tm, split_rows=split_rows)


def _tile_rows(seq, dec_rows):
    for tm in (512, 256, 128, 64):
        if seq % tm == 0 and dec_rows % tm == 0:
            return tm
    raise ValueError("sequence lengths must be multiples of 64")


def kernel(x_prompt, x_sample, state_conv, state_pool, c_prompt, c_sample, w_mod, b_mod, g_mix, g_ffn, g_final, conv_w_in, conv_w, conv_w_out, pool_w, pool_scale, router_w, router_b, exp_w_gate, exp_w_up, exp_w_down, sh_w_gate, sh_w_up, sh_w_down):
    bp, seq, d = x_prompt.shape
    bs, dseq, _ = x_sample.shape
    depth = w_mod.shape[0]
    n_exp = router_w.shape[2]
    assert dseq == GROUP and seq % GROUP == 0 and depth == 2
    assert state_conv.shape[1] == CONV_W - 1 and state_pool.shape[1] == POOL_HDR - 1
    assert n_exp % N_EXPERT_GROUPS == 0 and d % (128 * len(POOL_WINDOWS)) == 0
    tp, ts = bp * seq, bs * dseq
    t = tp + ts
    tm = _tile_rows(seq, ts)
    bm = min(256, t * TOP_K // n_exp)
    assert (t * TOP_K) % bm == 0
    n_prompt_tiles, tiles_per_seq = tp // tm, seq // tm
    ng_p, ng = tp // GROUP, t // GROUP

    nb_pad = -(-(bp + bs) // 8) * 8
    c_all = jnp.concatenate([c_prompt, c_sample, jnp.zeros((nb_pad - bp - bs, d), _F32)], axis=0)
    mods = _mod_rows(c_all, w_mod, b_mod)
    gps = seq // GROUP
    mod_p = jnp.broadcast_to(mods[:, :bp, None, :], (depth, bp, gps, 6 * d)).reshape(depth, bp * gps, 6 * d)
    mod_g = jnp.concatenate([mod_p, mods[:, bp:bp + bs, :]], axis=1)

    conv_tbl = jnp.pad(state_conv, ((0, 0), (CONV_HDR - (CONV_W - 1), 0), (0, 0)))
    pool_tbl = jnp.pad(state_pool, ((0, 0), (1, 0), (0, 0)))

    statics = dict(tm=tm, n_prompt_tiles=n_prompt_tiles, tiles_per_seq=tiles_per_seq)
    x, conv_tail = _conv_mixer(x_prompt.reshape(tp, d), x_sample.reshape(ts, d), mod_g[0], conv_tbl, g_mix[0],
                               conv_w_in.astype(_BF16), conv_w, conv_w_out.astype(_BF16), **statics)
    x = _moe(x, mod_g[0], g_ffn[0], g_final, router_w[0], router_b[0], exp_w_gate[0], exp_w_up[0],
             exp_w_down[0], sh_w_gate[0], sh_w_up[0], sh_w_down[0], tm=tm, bm=bm)
    x, pool_tail = _pool_mixer(x, mod_g[1], pool_tbl, g_mix[1], pool_w.astype(_BF16), pool_scale, **statics)
    y_p, y_s = _moe(x, mod_g[1], g_ffn[1], g_final, router_w[1], router_b[1], exp_w_gate[1], exp_w_up[1],
                    exp_w_down[1], sh_w_gate[1], sh_w_up[1], sh_w_down[1], tm=tm, bm=bm, split_rows=(tp, ts))

    return (y_p.reshape(bp, seq, d),
            y_s.reshape(bs, dseq, d),
            conv_tail[gps - 1:ng_p:gps],
            conv_tail[ng_p:],
            pool_tail[gps - 1:ng_p:gps],
            pool_tail[ng_p:])
```

```python
import functools

import jax
import jax.numpy as jnp
from jax import lax
from jax.experimental import pallas as pl
from jax.experimental.pallas import tpu as pltpu

GROUP = 64
CONV_W = 3
POOL_WINDOWS = (2, 4, 8, 16)
POOL_HDR = 16
CONV_HDR = 8
TOP_K = 8
N_EXPERT_GROUPS = 8
TOPK_GROUPS = 4
ROUTED_SCALE = 2.5
PAST_LEN = 4096
EPS = 1e-6
VMEM_LIMIT_V7X = 56 * 1024 * 1024
NEG_INF = float("-inf")

_F32 = jnp.float32
_BF16 = jnp.bfloat16


def _silu(v):
    return v * jax.nn.sigmoid(v)


_U32 = jnp.uint32
_HI_MASK = 0xFFFF0000


def _pack_pair(lo, hi):
    lo_b = lax.bitcast_convert_type(lo.astype(_BF16).astype(_F32), _U32)
    hi_b = lax.bitcast_convert_type(hi.astype(_BF16).astype(_F32), _U32)
    return hi_b | (lo_b >> 16)


def _unpack_pair(p):
    lo = lax.bitcast_convert_type(p << 16, _F32)
    hi = lax.bitcast_convert_type(p & _U32(_HI_MASK), _F32)
    return lo, hi


def _store_slabs(ref, row0, v, n_rows, ns):
    for j in range(ns):
        packed = _pack_pair(v[:, 128 * j:128 * (j + 1)], v[:, 128 * (j + ns):128 * (j + ns + 1)])
        ref[pl.ds(row0 * ns + j, n_rows, stride=ns), :] = packed


def _load_slabs(ref, row0, n_rows, ns):
    chunks = [_unpack_pair(ref[pl.ds(row0 * ns + j, n_rows, stride=ns), :]) for j in range(ns)]
    return [c[0] for c in chunks] + [c[1] for c in chunks]


def _cparams(n_axes):
    return pltpu.CompilerParams(dimension_semantics=("arbitrary",) * n_axes,
                                vmem_limit_bytes=VMEM_LIMIT_V7X)


def _mod_kernel(c_ref, w_ref, b_ref, o_ref):
    c = c_ref[...]
    ca = _silu(c).astype(_BF16)
    o_ref[...] = jnp.dot(ca, w_ref[...].astype(_BF16), preferred_element_type=_F32) + b_ref[...]


def _mod_rows(c_all, w_mod, b_mod):
    depth, d, n6 = w_mod.shape
    nb = c_all.shape[0]
    tn = min(1024, n6)
    return pl.pallas_call(
        _mod_kernel,
        grid=(depth, n6 // tn),
        in_specs=[pl.BlockSpec((nb, d), lambda l, j: (0, 0)),
                  pl.BlockSpec((None, d, tn), lambda l, j: (l, 0, j)),
                  pl.BlockSpec((None, 1, tn), lambda l, j: (l, 0, j))],
        out_specs=pl.BlockSpec((None, nb, tn), lambda l, j: (l, 0, j)),
        out_shape=jax.ShapeDtypeStruct((depth, nb, n6), _F32),
        compiler_params=_cparams(2),
        name="adaln_rows",
    )(c_all, w_mod, b_mod.reshape(depth, 1, n6))


def _norm_mod_group(xs, mod_ref, g_ref, s, d):
    ms = jnp.mean(xs * xs, axis=-1, keepdims=True)
    xn = (xs * lax.rsqrt(ms + EPS)) * g_ref[...]
    return xn * (1.0 + mod_ref[s:s + 1, d:2 * d]) + mod_ref[s:s + 1, 0:d]


def _conv_mixer_kernel(xp_ref, xs_ref, mod_ref, st_ref, g_ref, wb_ref, wc_ref, wv_ref, cw_ref, wo_ref,
                       out_ref, tail_ref, h_scr, acc_scr, ubuf, gbuf, carry,
                       *, gp, d, n_prompt_tiles, tiles_per_seq):
    i = pl.program_id(0)
    j = pl.program_id(1)
    nj = pl.num_programs(1)
    is_sample = i >= n_prompt_tiles

    def x_rows(s):
        sl = slice(s * GROUP, (s + 1) * GROUP)
        return jnp.where(is_sample, xs_ref[sl, :], xp_ref[sl, :])

    @pl.when(j == 0)
    def _():
        for s in range(gp):
            h_scr[s * GROUP:(s + 1) * GROUP, :] = _norm_mod_group(x_rows(s), mod_ref, g_ref, s, d).astype(_BF16)
        acc_scr[...] = jnp.zeros_like(acc_scr)

    @pl.when(i == 0)
    def _():
        carry[j] = jnp.zeros((CONV_HDR, carry.shape[2]), _F32)

    hb = h_scr[...]
    bq = jnp.dot(hb, wb_ref[...], preferred_element_type=_F32)
    cq = jnp.dot(hb, wc_ref[...], preferred_element_type=_F32)
    vq = jnp.dot(hb, wv_ref[...], preferred_element_type=_F32)
    u = cq * vq

    stride = GROUP + CONV_HDR
    for s in range(gp):
        if s == 0:
            prev = jnp.where(lax.rem(i, tiles_per_seq) == 0, 0.0, carry[j])
        else:
            prev = u[s * GROUP - CONV_HDR:s * GROUP, :]
        ubuf[s * stride:s * stride + CONV_HDR, :] = jnp.where(is_sample, st_ref[s], prev)
        ubuf[s * stride + CONV_HDR:(s + 1) * stride, :] = u[s * GROUP:(s + 1) * GROUP, :]
        tail_ref[s] = u[(s + 1) * GROUP - (CONV_W - 1):(s + 1) * GROUP, :]
    carry[j] = u[gp * GROUP - CONV_HDR:gp * GROUP, :]

    w0 = cw_ref[0:1, :]
    w1 = cw_ref[1:2, :]
    w2 = cw_ref[2:3, :]
    for s in range(gp):
        base = s * stride + CONV_HDR
        u0 = ubuf[base:base + GROUP, :]
        u1 = ubuf[base - 1:base - 1 + GROUP, :]
        u2 = ubuf[base - 2:base - 2 + GROUP, :]
        conv = u2 * w0 + u1 * w1 + u0 * w2
        gbuf[s * GROUP:(s + 1) * GROUP, :] = (bq[s * GROUP:(s + 1) * GROUP, :] * conv).astype(_BF16)
    acc_scr[...] += jnp.dot(gbuf[...], wo_ref[...], preferred_element_type=_F32)

    @pl.when(j == nj - 1)
    def _():
        for s in range(gp):
            sl = slice(s * GROUP, (s + 1) * GROUP)
            out_ref[sl, :] = x_rows(s) + mod_ref[s:s + 1, 2 * d:3 * d] * acc_scr[sl, :]


def _conv_mixer(xp, xs, mod_g, st_tbl, g_mix, w_in_bf, conv_w, w_out_bf, *, tm, n_prompt_tiles, tiles_per_seq):
    d = xp.shape[1]
    t = xp.shape[0] + xs.shape[0]
    npt = n_prompt_tiles
    gp = tm // GROUP
    tn = min(512, d)
    nj = d // tn
    kern = functools.partial(_conv_mixer_kernel, gp=gp, d=d, n_prompt_tiles=n_prompt_tiles,
                             tiles_per_seq=tiles_per_seq)
    return pl.pallas_call(
        kern,
        grid=(t // tm, nj),
        in_specs=[pl.BlockSpec((tm, d), lambda i, j: (jnp.minimum(i, npt - 1), 0)),
                  pl.BlockSpec((tm, d), lambda i, j: (jnp.maximum(i - npt, 0), 0)),
                  pl.BlockSpec((gp, 3 * d), lambda i, j: (i, 0)),
                  pl.BlockSpec((gp, CONV_HDR, tn), lambda i, j: (jnp.maximum(i - npt, 0), 0, j)),
                  pl.BlockSpec((1, d), lambda i, j: (0, 0)),
                  pl.BlockSpec((d, tn), lambda i, j: (0, j)),
                  pl.BlockSpec((d, tn), lambda i, j: (0, nj + j)),
                  pl.BlockSpec((d, tn), lambda i, j: (0, 2 * nj + j)),
                  pl.BlockSpec((CONV_W, tn), lambda i, j: (0, j)),
                  pl.BlockSpec((tn, d), lambda i, j: (j, 0))],
        out_specs=[pl.BlockSpec((tm, d), lambda i, j: (i, 0)),
                   pl.BlockSpec((gp, CONV_W - 1, tn), lambda i, j: (i, 0, j))],
        out_shape=[jax.ShapeDtypeStruct((t, d), _F32),
                   jax.ShapeDtypeStruct((t // GROUP, CONV_W - 1, d), _F32)],
        scratch_shapes=[pltpu.VMEM((tm, d), _BF16),
                        pltpu.VMEM((tm, d), _F32),
                        pltpu.VMEM((gp * (GROUP + CONV_HDR), tn), _F32),
                        pltpu.VMEM((tm, tn), _BF16),
                        pltpu.VMEM((nj, CONV_HDR, tn), _F32)],
        compiler_params=_cparams(2),
        name="conv_mixer",
    )(xp, xs, mod_g, st_tbl, g_mix.reshape(1, d), w_in_bf, w_in_bf, w_in_bf, conv_w, w_out_bf)


def _pool_mixer_kernel(x_ref, mod_ref, st_ref, g_ref, pw_ref, ps_ref, out_ref, tail_ref,
                       dbuf, carry, *, gp, d, n_prompt_tiles, tiles_per_seq, tm):
    i = pl.program_id(0)
    pg = d // len(POOL_WINDOWS)
    is_sample = i >= n_prompt_tiles
    row = lax.broadcasted_iota(jnp.int32, (GROUP, 1), 0)

    @pl.when(i == 0)
    def _():
        carry[...] = jnp.zeros_like(carry)

    prev = jnp.where(lax.rem(i, tiles_per_seq) == 0, 0.0, carry[...])
    for s in range(gp):
        h = _norm_mod_group(x_ref[s * GROUP:(s + 1) * GROUP, :], mod_ref, g_ref, s, d)
        hdr = jnp.where(is_sample, st_ref[s], prev)
        he = jnp.concatenate([hdr, h], axis=0)
        pos0 = jnp.where(is_sample, PAST_LEN, lax.rem(i, tiles_per_seq) * tm + s * GROUP)
        pos1 = pos0 + row + 1
        for q, w in enumerate(POOL_WINDOWS):
            cs = slice(q * pg, (q + 1) * pg)
            sw = he[:, cs]
            span = 1
            while span < w:
                sw = sw + pltpu.roll(sw, span, 0)
                span *= 2
            inv_cnt = 1.0 / jnp.minimum(pos1, w).astype(_F32)
            pooled = sw[POOL_HDR:, :] * inv_cnt
            dbuf[s * GROUP:(s + 1) * GROUP, cs] = (pooled - h[:, cs]).astype(_BF16)
        prev = h[GROUP - POOL_HDR:, :]
        tail_ref[s] = h[GROUP - (POOL_HDR - 1):, :]
    carry[...] = prev
    for q in range(len(POOL_WINDOWS)):
        cs = slice(q * pg, (q + 1) * pg)
        y = jnp.dot(dbuf[:, cs], pw_ref[q], preferred_element_type=_F32) * ps_ref[:, cs]
        for s in range(gp):
            sl = slice(s * GROUP, (s + 1) * GROUP)
            out_ref[sl, cs] = x_ref[sl, cs] + mod_ref[s:s + 1, 2 * d + q * pg:2 * d + (q + 1) * pg] * y[sl, :]


def _pool_mixer(x, mod_g, st_tbl, g_mix, pool_w_bf, pool_scale, *, tm, n_prompt_tiles, tiles_per_seq):
    t, d = x.shape
    gp = tm // GROUP
    nq = len(POOL_WINDOWS)
    pg = d // nq
    kern = functools.partial(_pool_mixer_kernel, gp=gp, d=d, n_prompt_tiles=n_prompt_tiles,
                             tiles_per_seq=tiles_per_seq, tm=tm)
    return pl.pallas_call(
        kern,
        grid=(t // tm,),
        in_specs=[pl.BlockSpec((tm, d), lambda i: (i, 0)),
                  pl.BlockSpec((gp, 3 * d), lambda i: (i, 0)),
                  pl.BlockSpec((gp, POOL_HDR, d), lambda i: (jnp.maximum(i - n_prompt_tiles, 0), 0, 0)),
                  pl.BlockSpec((1, d), lambda i: (0, 0)),
                  pl.BlockSpec((nq, pg, pg), lambda i: (0, 0, 0)),
                  pl.BlockSpec((1, d), lambda i: (0, 0))],
        out_specs=[pl.BlockSpec((tm, d), lambda i: (i, 0)),
                   pl.BlockSpec((gp, POOL_HDR - 1, d), lambda i: (i, 0, 0))],
        out_shape=[jax.ShapeDtypeStruct((t, d), _F32),
                   jax.ShapeDtypeStruct((t // GROUP, POOL_HDR - 1, d), _F32)],
        scratch_shapes=[pltpu.VMEM((tm, d), _BF16),
                        pltpu.VMEM((POOL_HDR, d), _F32)],
        compiler_params=_cparams(1),
        name="pool_mixer",
    )(x, mod_g, st_tbl, g_mix.reshape(1, d), pool_w_bf, pool_scale.reshape(1, d))


def _router_kernel(x_ref, mod_ref, g_ref, wr_ref, br_ref, swg_ref, swu_ref, swd_ref,
                   hs_ref, sh_ref, eidx_ref, wts_ref, cnt_ref, run_scr, hb_scr, hf_scr,
                   *, gp, d, n_exp, tm):
    i = pl.program_id(0)
    per_group = n_exp // N_EXPERT_GROUPS
    ns = d // 256

    @pl.when(i == 0)
    def _():
        run_scr[...] = jnp.zeros_like(run_scr)

    for s in range(gp):
        sl = slice(s * GROUP, (s + 1) * GROUP)
        h = _norm_mod_group(x_ref[sl, :], mod_ref, g_ref, s, d)
        hf_scr[sl, :] = h
        hb_scr[sl, :] = h.astype(_BF16)
        _store_slabs(hs_ref, s * GROUP, h, GROUP, ns)

    logits = lax.dot_general(wr_ref[...], hf_scr[...], (((1,), (1,)), ((), ())),
                             precision=lax.Precision.HIGHEST, preferred_element_type=_F32)
    sub = lax.broadcasted_iota(jnp.int32, (per_group, tm), 0)
    scores, biased, eids = [], [], []
    for g in range(N_EXPERT_GROUPS):
        rs = slice(g * per_group, (g + 1) * per_group)
        sc = jax.nn.sigmoid(logits[rs, :])
        scores.append(sc)
        biased.append(sc + br_ref[rs, :])
        eids.append(sub + g * per_group)

    gscore = []
    for g in range(N_EXPERT_GROUPS):
        m1 = jnp.max(biased[g], axis=0, keepdims=True)
        i1 = jnp.min(jnp.where(biased[g] == m1, sub, per_group), axis=0, keepdims=True)
        m2 = jnp.max(jnp.where(sub == i1, NEG_INF, biased[g]), axis=0, keepdims=True)
        gscore.append(m1 + m2)
    vals = []
    for g in range(N_EXPERT_GROUPS):
        beaten = jnp.zeros((1, tm), jnp.int32)
        for g2 in range(N_EXPERT_GROUPS):
            if g2 == g:
                continue
            beat = (gscore[g2] >= gscore[g]) if g2 < g else (gscore[g2] > gscore[g])
            beaten = beaten + beat.astype(jnp.int32)
        vals.append(jnp.where(beaten < TOPK_GROUPS, biased[g], NEG_INF))

    sel = [jnp.zeros((per_group, tm), _F32) for _ in range(N_EXPERT_GROUPS)]
    idxs, picked = [], []
    for _ in range(TOP_K):
        mx = vals[0]
        for g in range(1, N_EXPERT_GROUPS):
            mx = jnp.maximum(mx, vals[g])
        m = jnp.max(mx, axis=0, keepdims=True)
        ci = jnp.where(vals[0] == m, eids[0], n_exp)
        for g in range(1, N_EXPERT_GROUPS):
            ci = jnp.minimum(ci, jnp.where(vals[g] == m, eids[g], n_exp))
        idx = jnp.min(ci, axis=0, keepdims=True)
        sc_k = jnp.zeros((1, tm), _F32)
        for g in range(N_EXPERT_GROUPS):
            oh = eids[g] == idx
            sc_k = sc_k + jnp.sum(jnp.where(oh, scores[g], 0.0), axis=0, keepdims=True)
            vals[g] = jnp.where(oh, NEG_INF, vals[g])
            sel[g] = jnp.where(oh, 1.0, sel[g])
        idxs.append(idx)
        picked.append(sc_k)
    den = picked[0]
    for k in range(1, TOP_K):
        den = den + picked[k]

    for k in range(TOP_K):
        eidx_ref[k:k + 1, :] = idxs[k]
        wts_ref[k:k + 1, :] = picked[k] / den * ROUTED_SCALE
    sel_all = jnp.concatenate(sel, axis=0)
    run_scr[...] = run_scr[...] + jnp.sum(sel_all, axis=1, keepdims=True)
    cnt_ref[...] = run_scr[...].astype(jnp.int32)

    hb = hb_scr[...]
    sg = jnp.dot(hb, swg_ref[...], preferred_element_type=_F32)
    su = jnp.dot(hb, swu_ref[...], preferred_element_type=_F32)
    sh_ref[...] = jnp.dot((_silu(sg) * su).astype(_BF16), swd_ref[...],
                          preferred_element_type=_F32).astype(_BF16)


def _router(x, mod_g, g_ffn, wr_t, br, swg_bf, swu_bf, swd_bf, *, tm):
    t, d = x.shape
    gp = tm // GROUP
    n_exp = wr_t.shape[0]
    f = swg_bf.shape[1]
    ns = d // 256
    kern = functools.partial(_router_kernel, gp=gp, d=d, n_exp=n_exp, tm=tm)
    const = lambda i: (0, 0)
    return pl.pallas_call(
        kern,
        grid=(t // tm,),
        in_specs=[pl.BlockSpec((tm, d), lambda i: (i, 0)),
                  pl.BlockSpec((gp, 3 * d), lambda i: (i, 1)),
                  pl.BlockSpec((1, d), const),
                  pl.BlockSpec((n_exp, d), const),
                  pl.BlockSpec((n_exp, 1), const),
                  pl.BlockSpec((d, f), const),
                  pl.BlockSpec((d, f), const),
                  pl.BlockSpec((f, d), const)],
        out_specs=[pl.BlockSpec((tm * ns, 128), lambda i: (i, 0)),
                   pl.BlockSpec((tm, d), lambda i: (i, 0)),
                   pl.BlockSpec((TOP_K, tm), lambda i: (0, i)),
                   pl.BlockSpec((TOP_K, tm), lambda i: (0, i)),
                   pl.BlockSpec((n_exp, 128), const)],
        out_shape=[jax.ShapeDtypeStruct((t * ns, 128), _U32),
                   jax.ShapeDtypeStruct((t, d), _BF16),
                   jax.ShapeDtypeStruct((TOP_K, t), jnp.int32),
                   jax.ShapeDtypeStruct((TOP_K, t), _F32),
                   jax.ShapeDtypeStruct((n_exp, 128), jnp.int32)],
        scratch_shapes=[pltpu.VMEM((n_exp, 128), _F32),
                        pltpu.VMEM((tm, d), _BF16),
                        pltpu.VMEM((tm, d), _F32)],
        compiler_params=_cparams(1),
        name="router_shared",
    )(x, mod_g, g_ffn.reshape(1, d), wr_t, br.reshape(n_exp, 1), swg_bf, swu_bf, swd_bf)


def _ffn_kernel(be_ref, nu_ref, idx_hbm, h_hbm, wg_ref, wu_ref, wd_ref, y_hbm,
                idx_s, xflat, yflat, xmat, wgb, wub, wdb, sem_i, sem_g, sem_s, *, bm, ns):
    b = pl.program_id(0)
    nu = nu_ref[0]
    slot = lax.rem(b, 2)
    oslot = 1 - slot
    rows = bm * ns

    def idx_copy(blk, sl):
        return pltpu.make_async_copy(idx_hbm.at[blk], idx_s.at[sl], sem_i.at[sl])

    def row_gather(sl, r):
        src = pl.multiple_of(idx_s[sl, r] * ns, ns)
        dst = pl.multiple_of((sl * bm + r) * ns, ns)
        return pltpu.make_async_copy(h_hbm.at[pl.ds(src, ns)], xflat.at[pl.ds(dst, ns)], sem_g.at[sl])

    def row_scatter(sl, r):
        src = pl.multiple_of((sl * bm + r) * ns, ns)
        dst = pl.multiple_of(idx_s[sl, bm + r] * ns, ns)
        return pltpu.make_async_copy(yflat.at[pl.ds(src, ns)], y_hbm.at[pl.ds(dst, ns)], sem_s.at[sl])

    def block_gather(sl):
        return pltpu.make_async_copy(h_hbm.at[pl.ds(0, rows)],
                                     xflat.at[pl.ds(pl.multiple_of(sl * rows, rows), rows)], sem_g.at[sl])

    def block_scatter(sl, dst0=0):
        return pltpu.make_async_copy(yflat.at[pl.ds(pl.multiple_of(sl * rows, rows), rows)],
                                     y_hbm.at[pl.ds(dst0, rows)], sem_s.at[sl])

    @pl.when(b < nu)
    def _():
        @pl.when(b == 0)
        def _():
            yflat[...] = jnp.zeros_like(yflat)
            spare0 = y_hbm.shape[0] - 2 * rows
            for sl in range(2):
                block_scatter(sl, spare0 + sl * rows).start()
            first = idx_copy(0, 0)
            first.start()
            first.wait()
            for r in range(bm):
                row_gather(0, r).start()
            idx_copy(1, 1).start()

        new_expert = (b == 0) | (be_ref[b] != be_ref[jnp.maximum(b - 1, 0)])

        @pl.when(new_expert)
        def _():
            wgb[...] = wg_ref[...].astype(_BF16)
            wub[...] = wu_ref[...].astype(_BF16)
            wdb[...] = wd_ref[...].astype(_BF16)

        block_gather(slot).wait()
        idx_copy(b + 1, oslot).wait()
        for r in range(bm):
            row_gather(oslot, r).start()
        block_scatter(slot).wait()

        cols = _load_slabs(xflat, slot * bm, bm, ns)
        for c, v in enumerate(cols):
            xmat[:, 128 * c:128 * (c + 1)] = v.astype(_BF16)
        x = xmat[...]
        hg = jnp.dot(x, wgb[...], preferred_element_type=_F32)
        hu = jnp.dot(x, wub[...], preferred_element_type=_F32)
        hb = (_silu(hg) * hu).astype(_BF16)
        _store_slabs(yflat, slot * bm, jnp.dot(hb, wdb[...], preferred_element_type=_F32), bm, ns)
        for r in range(bm):
            row_scatter(slot, r).start()
        idx_copy(b + 2, slot).start()

        @pl.when(b == nu - 1)
        def _():
            block_gather(oslot).wait()
            idx_copy(b + 2, slot).wait()
            block_scatter(oslot).wait()
            block_scatter(slot).wait()


def _routed_ffn(blk_e, n_used, idx, h_slabs, wg, wu, wd, *, layer, bm, n_out_rows):
    nb = idx.shape[0] - 1
    _, _, d, f = wg.shape
    ns = d // 256
    kern = functools.partial(_ffn_kernel, bm=bm, ns=ns)
    grid_spec = pltpu.PrefetchScalarGridSpec(
        num_scalar_prefetch=2,
        grid=(nb,),
        in_specs=[pl.BlockSpec(memory_space=pl.ANY),
                  pl.BlockSpec(memory_space=pl.ANY),
                  pl.BlockSpec((None, None, d, f), lambda b, be, nu: (layer, be[b], 0, 0)),
                  pl.BlockSpec((None, None, d, f), lambda b, be, nu: (layer, be[b], 0, 0)),
                  pl.BlockSpec((None, None, f, d), lambda b, be, nu: (layer, be[b], 0, 0))],
        out_specs=pl.BlockSpec(memory_space=pl.ANY),
        scratch_shapes=[pltpu.SMEM((2, 2 * bm), jnp.int32),
                        pltpu.VMEM((2 * bm * ns, 128), _U32),
                        pltpu.VMEM((2 * bm * ns, 128), _U32),
                        pltpu.VMEM((bm, d), _BF16),
                        pltpu.VMEM((d, f), _BF16),
                        pltpu.VMEM((d, f), _BF16),
                        pltpu.VMEM((f, d), _BF16),
                        pltpu.SemaphoreType.DMA((2,)),
                        pltpu.SemaphoreType.DMA((2,)),
                        pltpu.SemaphoreType.DMA((2,))])
    return pl.pallas_call(
        kern,
        grid_spec=grid_spec,
        out_shape=jax.ShapeDtypeStruct((n_out_rows * ns, 128), _U32),
        compiler_params=_cparams(1),
        name="routed_ffn",
    )(blk_e, n_used, idx, h_slabs, wg, wu, wd)


def _combine_kernel(*refs, d, gpt, n_prompt_tiles):
    x_ref, sh_ref, w_ref, mod_ref, gf_ref = refs[:5]
    y_refs = refs[5:5 + TOP_K]
    n_out = 1 if n_prompt_tiles is None else 2
    out_refs = refs[5 + TOP_K:5 + TOP_K + n_out]
    lo_scr, hi_scr = refs[5 + TOP_K + n_out:]
    i = pl.program_id(0)
    ns = d // 256
    tmc = x_ref.shape[0]
    acc_lo = jnp.zeros(lo_scr.shape, _F32)
    acc_hi = jnp.zeros(hi_scr.shape, _F32)
    for k in range(TOP_K):
        lo, hi = _unpack_pair(y_refs[k][...])
        wk = w_ref[:, k:k + 1]
        acc_lo = acc_lo + wk * lo
        acc_hi = acc_hi + wk * hi
    lo_scr[...] = acc_lo
    hi_scr[...] = acc_hi
    cols = ([lo_scr[pl.ds(j, tmc, stride=ns), :] for j in range(ns)]
            + [hi_scr[pl.ds(j, tmc, stride=ns), :] for j in range(ns)])
    acc = sh_ref[...].astype(_F32) + jnp.concatenate(cols, axis=1)
    groups_per_mod = mod_ref.shape[0]
    g0 = lax.rem(i, groups_per_mod // gpt) * gpt
    rows = []
    for s in range(gpt):
        sl = slice(s * GROUP, (s + 1) * GROUP)
        gate = mod_ref[pl.ds(g0 + s, 1), 2 * d:3 * d]
        xo = x_ref[sl, :] + gate * acc[sl, :]
        if n_prompt_tiles is not None:
            ms = jnp.mean(xo * xo, axis=-1, keepdims=True)
            xo = (xo * lax.rsqrt(ms + EPS)) * gf_ref[...]
        rows.append(xo)
    if n_prompt_tiles is None:
        for s in range(gpt):
            out_refs[0][s * GROUP:(s + 1) * GROUP, :] = rows[s]
    else:
        for o_ref, active in ((out_refs[0], i < n_prompt_tiles), (out_refs[1], i >= n_prompt_tiles)):
            @pl.when(active)
            def _(o_ref=o_ref):
                for s in range(gpt):
                    o_ref[s * GROUP:(s + 1) * GROUP, :] = rows[s]


def _combine(x, shared, wts, mod_g, g_final, y, *, tm, split_rows=None):
    t, d = x.shape
    tmc = min(256, tm)
    gpt = tmc // GROUP
    gp = tm // GROUP
    nt = t // tmc
    row_spec = pl.BlockSpec((tmc, d), lambda i: (i, 0))
    if split_rows is None:
        npt = None
        out_specs = row_spec
        out_shape = jax.ShapeDtypeStruct((t, d), _F32)
    else:
        npt = split_rows[0] // tmc
        out_specs = [pl.BlockSpec((tmc, d), lambda i: (jnp.minimum(i, npt - 1), 0)),
                     pl.BlockSpec((tmc, d), lambda i: (jnp.maximum(i - npt, 0), 0))]
        out_shape = [jax.ShapeDtypeStruct((rows, d), _F32) for rows in split_rows]
    kern = functools.partial(_combine_kernel, d=d, gpt=gpt, n_prompt_tiles=npt)
    ns = d // 256
    y_specs = [pl.BlockSpec((tmc * ns, 128), functools.partial(lambda i, k: (k * nt + i, 0), k=k))
               for k in range(TOP_K)]
    return pl.pallas_call(
        kern,
        grid=(nt,),
        scratch_shapes=[pltpu.VMEM((tmc * ns, 128), _F32), pltpu.VMEM((tmc * ns, 128), _F32)],
        in_specs=[row_spec,
                  row_spec,
                  pl.BlockSpec((tmc * ns, TOP_K), lambda i: (i, 0)),
                  pl.BlockSpec((gp, 3 * d), lambda i: (i // (tm // tmc), 1)),
                  pl.BlockSpec((1, d), lambda i: (0, 0))] + y_specs,
        out_specs=out_specs,
        out_shape=out_shape,
        compiler_params=_cparams(1),
        name="combine" if split_rows is None else "combine_final",
    )(x, shared, wts, mod_g, g_final.reshape(1, d), *([y] * TOP_K))


def _dispatch_tables(eidx, counts, *, bm, nb, t):
    n_exp = counts.shape[0]
    low_bits = (t * TOP_K).bit_length()
    low_mask = (1 << low_bits) - 1
    assert t * TOP_K < low_mask and (n_exp + 1) << low_bits < 2 ** 31
    tok_k = jnp.arange(t, dtype=jnp.int32)[None, :] * TOP_K + jnp.arange(TOP_K, dtype=jnp.int32)[:, None]
    real = (eidx << low_bits) + tok_k
    n_pad = (-counts) % bm
    e_col = jnp.arange(n_exp, dtype=jnp.int32)[:, None]
    pad = jnp.where(jnp.arange(bm, dtype=jnp.int32)[None, :] < n_pad[:, None],
                    (e_col << low_bits) + low_mask, (n_exp << low_bits) + low_mask)
    keys = jnp.sort(jnp.concatenate([real.reshape(-1), pad.reshape(-1)]))
    low = keys & low_mask
    is_pad = low == low_mask
    tok = low // TOP_K
    r = jnp.arange(nb * bm, dtype=jnp.int32)
    row_src = jnp.where(is_pad, r % t, tok)
    row_dst = jnp.where(is_pad, TOP_K * t + ((r // bm) % 2) * bm + (r % bm),
                        (low % TOP_K) * t + tok)
    idx = jnp.concatenate([row_src.reshape(nb, bm), row_dst.reshape(nb, bm)], axis=1)
    idx = jnp.concatenate([idx, jnp.zeros((1, 2 * bm), jnp.int32)], axis=0)
    blk_e = jnp.minimum(keys.reshape(nb, bm)[:, 0] >> low_bits, n_exp - 1)
    n_used = (jnp.sum(counts + n_pad) // bm).astype(jnp.int32).reshape(1)
    return blk_e, n_used, idx


def _moe(x, mod_g, g_ffn, g_final, router_w, router_b, wg, wu, wd, swg, swu, swd,
         *, layer, tm, bm, split_rows=None):
    t, d = x.shape
    n_exp = router_w.shape[1]
    h_slabs, shared, eidx, wts, cnt = _router(
        x, mod_g, g_ffn, router_w.T, router_b,
        swg.astype(_BF16), swu.astype(_BF16), swd.astype(_BF16), tm=tm)
    nb = (t * TOP_K) // bm + n_exp
    blk_e, n_used, idx = _dispatch_tables(eidx, cnt[:, 0], bm=bm, nb=nb, t=t)
    y = _routed_ffn(blk_e, n_used, idx, h_slabs, wg, wu, wd, layer=layer, bm=bm,
                    n_out_rows=TOP_K * t + 2 * bm)
    w_rows = jnp.repeat(wts.T, d // 256, axis=0)
    return _combine(x, shared, w_rows, mod_g, g_final, y, tm=tm, split_rows=split_rows)


def _tile_rows(seq, dec_rows):
    for tm in (512, 256, 128, 64):
        if seq % tm == 0 and dec_rows % tm == 0:
            return tm
    raise ValueError("sequence lengths must be multiples of 64")


def kernel(x_prompt, x_sample, state_conv, state_pool, c_prompt, c_sample, w_mod, b_mod, g_mix, g_ffn, g_final, conv_w_in, conv_w, conv_w_out, pool_w, pool_scale, router_w, router_b, exp_w_gate, exp_w_up, exp_w_down, sh_w_gate, sh_w_up, sh_w_down):
    bp, seq, d = x_prompt.shape
    bs, dseq, _ = x_sample.shape
    depth = w_mod.shape[0]
    n_exp = router_w.shape[2]
    assert dseq == GROUP and seq % GROUP == 0 and depth == 2
    assert state_conv.shape[1] == CONV_W - 1 and state_pool.shape[1] == POOL_HDR - 1
    assert n_exp % N_EXPERT_GROUPS == 0 and d % (128 * len(POOL_WINDOWS)) == 0
    tp, ts = bp * seq, bs * dseq
    t = tp + ts
    tm = _tile_rows(seq, ts)
    bm = min(256, t * TOP_K // n_exp)
    assert (t * TOP_K) % bm == 0
    n_prompt_tiles, tiles_per_seq = tp // tm, seq // tm
    ng_p, ng = tp // GROUP, t // GROUP

    nb_pad = -(-(bp + bs) // 8) * 8
    c_all = jnp.concatenate([c_prompt, c_sample, jnp.zeros((nb_pad - bp - bs, d), _F32)], axis=0)
    mods = _mod_rows(c_all, w_mod, b_mod)
    gps = seq // GROUP
    mod_p = jnp.broadcast_to(mods[:, :bp, None, :], (depth, bp, gps, 6 * d)).reshape(depth, bp * gps, 6 * d)
    mod_g = jnp.concatenate([mod_p, mods[:, bp:bp + bs, :]], axis=1)

    conv_tbl = jnp.pad(state_conv, ((0, 0), (CONV_HDR - (CONV_W - 1), 0), (0, 0)))
    pool_tbl = jnp.pad(state_pool, ((0, 0), (1, 0), (0, 0)))

    statics = dict(tm=tm, n_prompt_tiles=n_prompt_tiles, tiles_per_seq=tiles_per_seq)
    x, conv_tail = _conv_mixer(x_prompt.reshape(tp, d), x_sample.reshape(ts, d), mod_g[0], conv_tbl, g_mix[0],
                               conv_w_in.astype(_BF16), conv_w, conv_w_out.astype(_BF16), **statics)
    x = _moe(x, mod_g[0], g_ffn[0], g_final, router_w[0], router_b[0], exp_w_gate, exp_w_up, exp_w_down,
             sh_w_gate[0], sh_w_up[0], sh_w_down[0], layer=0, tm=tm, bm=bm)
    x, pool_tail = _pool_mixer(x, mod_g[1], pool_tbl, g_mix[1], pool_w.astype(_BF16), pool_scale, **statics)
    y_p, y_s = _moe(x, mod_g[1], g_ffn[1], g_final, router_w[1], router_b[1], exp_w_gate, exp_w_up, exp_w_down,
                    sh_w_gate[1], sh_w_up[1], sh_w_down[1], layer=1, tm=tm, bm=bm, split_rows=(tp, ts))

    return (y_p.reshape(bp, seq, d),
            y_s.reshape(bs, dseq, d),
            conv_tail[gps - 1:ng_p:gps],
            conv_tail[ng_p:],
            pool_tail[gps - 1:ng_p:gps],
            pool_tail[ng_p:])
```

```python
import functools

import jax
import jax.numpy as jnp
from jax import lax
from jax.experimental import pallas as pl
from jax.experimental.pallas import tpu as pltpu

GROUP = 64
CONV_W = 3
POOL_WINDOWS = (2, 4, 8, 16)
POOL_HDR = 16
CONV_HDR = 8
TOP_K = 8
N_EXPERT_GROUPS = 8
TOPK_GROUPS = 4
ROUTED_SCALE = 2.5
PAST_LEN = 4096
EPS = 1e-6
VMEM_LIMIT_V7X = 56 * 1024 * 1024
NEG_INF = float("-inf")

_F32 = jnp.float32
_BF16 = jnp.bfloat16


def _silu(v):
    return v * jax.nn.sigmoid(v)


_U32 = jnp.uint32
_HI_MASK = 0xFFFF0000


def _pack_pair(lo, hi):
    lo_b = lax.bitcast_convert_type(lo.astype(_BF16).astype(_F32), _U32)
    hi_b = lax.bitcast_convert_type(hi.astype(_BF16).astype(_F32), _U32)
    return hi_b | (lo_b >> 16)


def _unpack_pair(p):
    lo = lax.bitcast_convert_type(p << 16, _F32)
    hi = lax.bitcast_convert_type(p & _U32(_HI_MASK), _F32)
    return lo, hi


def _store_slabs(ref, row0, v, n_rows, ns):
    for j in range(ns):
        packed = _pack_pair(v[:, 128 * j:128 * (j + 1)], v[:, 128 * (j + ns):128 * (j + ns + 1)])
        ref[pl.ds(row0 * ns + j, n_rows, stride=ns), :] = packed


def _load_slabs(ref, row0, n_rows, ns):
    chunks = [_unpack_pair(ref[pl.ds(row0 * ns + j, n_rows, stride=ns), :]) for j in range(ns)]
    return [c[0] for c in chunks] + [c[1] for c in chunks]


def _cparams(n_axes):
    return pltpu.CompilerParams(dimension_semantics=("arbitrary",) * n_axes,
                                vmem_limit_bytes=VMEM_LIMIT_V7X)


def _mod_kernel(c_ref, w_ref, b_ref, o_ref):
    c = c_ref[...]
    ca = _silu(c).astype(_BF16)
    o_ref[...] = jnp.dot(ca, w_ref[...].astype(_BF16), preferred_element_type=_F32) + b_ref[...]


def _mod_rows(c_all, w_mod, b_mod):
    depth, d, n6 = w_mod.shape
    nb = c_all.shape[0]
    tn = min(1024, n6)
    return pl.pallas_call(
        _mod_kernel,
        grid=(depth, n6 // tn),
        in_specs=[pl.BlockSpec((nb, d), lambda l, j: (0, 0)),
                  pl.BlockSpec((None, d, tn), lambda l, j: (l, 0, j)),
                  pl.BlockSpec((None, 1, tn), lambda l, j: (l, 0, j))],
        out_specs=pl.BlockSpec((None, nb, tn), lambda l, j: (l, 0, j)),
        out_shape=jax.ShapeDtypeStruct((depth, nb, n6), _F32),
        compiler_params=_cparams(2),
        name="adaln_rows",
    )(c_all, w_mod, b_mod.reshape(depth, 1, n6))


def _norm_mod_group(xs, mod_ref, g_ref, s, d):
    ms = jnp.mean(xs * xs, axis=-1, keepdims=True)
    xn = (xs * lax.rsqrt(ms + EPS)) * g_ref[...]
    return xn * (1.0 + mod_ref[s:s + 1, d:2 * d]) + mod_ref[s:s + 1, 0:d]


def _conv_mixer_kernel(xp_ref, xs_ref, mod_ref, st_ref, g_ref, wb_ref, wc_ref, wv_ref, cw_ref, wo_ref,
                       out_ref, tail_ref, h_scr, acc_scr, ubuf, gbuf, carry,
                       *, gp, d, n_prompt_tiles, tiles_per_seq):
    i = pl.program_id(0)
    j = pl.program_id(1)
    nj = pl.num_programs(1)
    is_sample = i >= n_prompt_tiles

    def x_rows(s):
        sl = slice(s * GROUP, (s + 1) * GROUP)
        return jnp.where(is_sample, xs_ref[sl, :], xp_ref[sl, :])

    @pl.when(j == 0)
    def _():
        for s in range(gp):
            h_scr[s * GROUP:(s + 1) * GROUP, :] = _norm_mod_group(x_rows(s), mod_ref, g_ref, s, d).astype(_BF16)
        acc_scr[...] = jnp.zeros_like(acc_scr)

    @pl.when(i == 0)
    def _():
        carry[j] = jnp.zeros((CONV_HDR, carry.shape[2]), _F32)

    hb = h_scr[...]
    bq = jnp.dot(hb, wb_ref[...], preferred_element_type=_F32)
    cq = jnp.dot(hb, wc_ref[...], preferred_element_type=_F32)
    vq = jnp.dot(hb, wv_ref[...], preferred_element_type=_F32)
    u = cq * vq

    stride = GROUP + CONV_HDR
    for s in range(gp):
        if s == 0:
            prev = jnp.where(lax.rem(i, tiles_per_seq) == 0, 0.0, carry[j])
        else:
            prev = u[s * GROUP - CONV_HDR:s * GROUP, :]
        ubuf[s * stride:s * stride + CONV_HDR, :] = jnp.where(is_sample, st_ref[s], prev)
        ubuf[s * stride + CONV_HDR:(s + 1) * stride, :] = u[s * GROUP:(s + 1) * GROUP, :]
        tail_ref[s] = u[(s + 1) * GROUP - (CONV_W - 1):(s + 1) * GROUP, :]
    carry[j] = u[gp * GROUP - CONV_HDR:gp * GROUP, :]

    w0 = cw_ref[0:1, :]
    w1 = cw_ref[1:2, :]
    w2 = cw_ref[2:3, :]
    for s in range(gp):
        base = s * stride + CONV_HDR
        u0 = ubuf[base:base + GROUP, :]
        u1 = ubuf[base - 1:base - 1 + GROUP, :]
        u2 = ubuf[base - 2:base - 2 + GROUP, :]
        conv = u2 * w0 + u1 * w1 + u0 * w2
        gbuf[s * GROUP:(s + 1) * GROUP, :] = (bq[s * GROUP:(s + 1) * GROUP, :] * conv).astype(_BF16)
    acc_scr[...] += jnp.dot(gbuf[...], wo_ref[...], preferred_element_type=_F32)

    @pl.when(j == nj - 1)
    def _():
        for s in range(gp):
            sl = slice(s * GROUP, (s + 1) * GROUP)
            out_ref[sl, :] = x_rows(s) + mod_ref[s:s + 1, 2 * d:3 * d] * acc_scr[sl, :]


def _conv_mixer(xp, xs, mod_g, st_tbl, g_mix, w_in_bf, conv_w, w_out_bf, *, tm, n_prompt_tiles, tiles_per_seq):
    d = xp.shape[1]
    t = xp.shape[0] + xs.shape[0]
    npt = n_prompt_tiles
    gp = tm // GROUP
    tn = min(512, d)
    nj = d // tn
    kern = functools.partial(_conv_mixer_kernel, gp=gp, d=d, n_prompt_tiles=n_prompt_tiles,
                             tiles_per_seq=tiles_per_seq)
    return pl.pallas_call(
        kern,
        grid=(t // tm, nj),
        in_specs=[pl.BlockSpec((tm, d), lambda i, j: (jnp.minimum(i, npt - 1), 0)),
                  pl.BlockSpec((tm, d), lambda i, j: (jnp.maximum(i - npt, 0), 0)),
                  pl.BlockSpec((gp, 3 * d), lambda i, j: (i, 0)),
                  pl.BlockSpec((gp, CONV_HDR, tn), lambda i, j: (jnp.maximum(i - npt, 0), 0, j)),
                  pl.BlockSpec((1, d), lambda i, j: (0, 0)),
                  pl.BlockSpec((d, tn), lambda i, j: (0, j)),
                  pl.BlockSpec((d, tn), lambda i, j: (0, nj + j)),
                  pl.BlockSpec((d, tn), lambda i, j: (0, 2 * nj + j)),
                  pl.BlockSpec((CONV_W, tn), lambda i, j: (0, j)),
                  pl.BlockSpec((tn, d), lambda i, j: (j, 0))],
        out_specs=[pl.BlockSpec((tm, d), lambda i, j: (i, 0)),
                   pl.BlockSpec((gp, CONV_W - 1, tn), lambda i, j: (i, 0, j))],
        out_shape=[jax.ShapeDtypeStruct((t, d), _F32),
                   jax.ShapeDtypeStruct((t // GROUP, CONV_W - 1, d), _F32)],
        scratch_shapes=[pltpu.VMEM((tm, d), _BF16),
                        pltpu.VMEM((tm, d), _F32),
                        pltpu.VMEM((gp * (GROUP + CONV_HDR), tn), _F32),
                        pltpu.VMEM((tm, tn), _BF16),
                        pltpu.VMEM((nj, CONV_HDR, tn), _F32)],
        compiler_params=_cparams(2),
        name="conv_mixer",
    )(xp, xs, mod_g, st_tbl, g_mix.reshape(1, d), w_in_bf, w_in_bf, w_in_bf, conv_w, w_out_bf)


def _pool_mixer_kernel(x_ref, mod_ref, st_ref, g_ref, pw_ref, ps_ref, out_ref, tail_ref,
                       dbuf, carry, *, gp, d, n_prompt_tiles, tiles_per_seq, tm):
    i = pl.program_id(0)
    pg = d // len(POOL_WINDOWS)
    is_sample = i >= n_prompt_tiles
    row = lax.broadcasted_iota(jnp.int32, (GROUP, 1), 0)

    @pl.when(i == 0)
    def _():
        carry[...] = jnp.zeros_like(carry)

    prev = jnp.where(lax.rem(i, tiles_per_seq) == 0, 0.0, carry[...])
    for s in range(gp):
        h = _norm_mod_group(x_ref[s * GROUP:(s + 1) * GROUP, :], mod_ref, g_ref, s, d)
        hdr = jnp.where(is_sample, st_ref[s], prev)
        he = jnp.concatenate([hdr, h], axis=0)
        pos0 = jnp.where(is_sample, PAST_LEN, lax.rem(i, tiles_per_seq) * tm + s * GROUP)
        pos1 = pos0 + row + 1
        for q, w in enumerate(POOL_WINDOWS):
            cs = slice(q * pg, (q + 1) * pg)
            sw = he[:, cs]
            span = 1
            while span < w:
                sw = sw + pltpu.roll(sw, span, 0)
                span *= 2
            inv_cnt = 1.0 / jnp.minimum(pos1, w).astype(_F32)
            pooled = sw[POOL_HDR:, :] * inv_cnt
            dbuf[s * GROUP:(s + 1) * GROUP, cs] = (pooled - h[:, cs]).astype(_BF16)
        prev = h[GROUP - POOL_HDR:, :]
        tail_ref[s] = h[GROUP - (POOL_HDR - 1):, :]
    carry[...] = prev
    for q in range(len(POOL_WINDOWS)):
        cs = slice(q * pg, (q + 1) * pg)
        y = jnp.dot(dbuf[:, cs], pw_ref[q], preferred_element_type=_F32) * ps_ref[:, cs]
        for s in range(gp):
            sl = slice(s * GROUP, (s + 1) * GROUP)
            out_ref[sl, cs] = x_ref[sl, cs] + mod_ref[s:s + 1, 2 * d + q * pg:2 * d + (q + 1) * pg] * y[sl, :]


def _pool_mixer(x, mod_g, st_tbl, g_mix, pool_w_bf, pool_scale, *, tm, n_prompt_tiles, tiles_per_seq):
    t, d = x.shape
    gp = tm // GROUP
    nq = len(POOL_WINDOWS)
    pg = d // nq
    kern = functools.partial(_pool_mixer_kernel, gp=gp, d=d, n_prompt_tiles=n_prompt_tiles,
                             tiles_per_seq=tiles_per_seq, tm=tm)
    return pl.pallas_call(
        kern,
        grid=(t // tm,),
        in_specs=[pl.BlockSpec((tm, d), lambda i: (i, 0)),
                  pl.BlockSpec((gp, 3 * d), lambda i: (i, 0)),
                  pl.BlockSpec((gp, POOL_HDR, d), lambda i: (jnp.maximum(i - n_prompt_tiles, 0), 0, 0)),
                  pl.BlockSpec((1, d), lambda i: (0, 0)),
                  pl.BlockSpec((nq, pg, pg), lambda i: (0, 0, 0)),
                  pl.BlockSpec((1, d), lambda i: (0, 0))],
        out_specs=[pl.BlockSpec((tm, d), lambda i: (i, 0)),
                   pl.BlockSpec((gp, POOL_HDR - 1, d), lambda i: (i, 0, 0))],
        out_shape=[jax.ShapeDtypeStruct((t, d), _F32),
                   jax.ShapeDtypeStruct((t // GROUP, POOL_HDR - 1, d), _F32)],
        scratch_shapes=[pltpu.VMEM((tm, d), _BF16),
                        pltpu.VMEM((POOL_HDR, d), _F32)],
        compiler_params=_cparams(1),
        name="pool_mixer",
    )(x, mod_g, st_tbl, g_mix.reshape(1, d), pool_w_bf, pool_scale.reshape(1, d))


def _router_kernel(x_ref, mod_ref, g_ref, wr_ref, br_ref, swg_ref, swu_ref, swd_ref,
                   hs_ref, sh_ref, eidx_ref, wts_ref, cnt_ref, run_scr, hb_scr, hf_scr,
                   *, gp, d, n_exp, tm):
    i = pl.program_id(0)
    per_group = n_exp // N_EXPERT_GROUPS
    ns = d // 256

    @pl.when(i == 0)
    def _():
        run_scr[...] = jnp.zeros_like(run_scr)

    for s in range(gp):
        sl = slice(s * GROUP, (s + 1) * GROUP)
        h = _norm_mod_group(x_ref[sl, :], mod_ref, g_ref, s, d)
        hf_scr[sl, :] = h
        hb_scr[sl, :] = h.astype(_BF16)
        _store_slabs(hs_ref, s * GROUP, h, GROUP, ns)

    logits = lax.dot_general(wr_ref[...], hf_scr[...], (((1,), (1,)), ((), ())),
                             precision=lax.Precision.HIGHEST, preferred_element_type=_F32)
    sub = lax.broadcasted_iota(jnp.int32, (per_group, tm), 0)
    scores, biased, eids = [], [], []
    for g in range(N_EXPERT_GROUPS):
        rs = slice(g * per_group, (g + 1) * per_group)
        sc = jax.nn.sigmoid(logits[rs, :])
        scores.append(sc)
        biased.append(sc + br_ref[rs, :])
        eids.append(sub + g * per_group)

    gscore = []
    for g in range(N_EXPERT_GROUPS):
        m1 = jnp.max(biased[g], axis=0, keepdims=True)
        i1 = jnp.min(jnp.where(biased[g] == m1, sub, per_group), axis=0, keepdims=True)
        m2 = jnp.max(jnp.where(sub == i1, NEG_INF, biased[g]), axis=0, keepdims=True)
        gscore.append(m1 + m2)
    vals = []
    for g in range(N_EXPERT_GROUPS):
        beaten = jnp.zeros((1, tm), jnp.int32)
        for g2 in range(N_EXPERT_GROUPS):
            if g2 == g:
                continue
            beat = (gscore[g2] >= gscore[g]) if g2 < g else (gscore[g2] > gscore[g])
            beaten = beaten + beat.astype(jnp.int32)
        vals.append(jnp.where(beaten < TOPK_GROUPS, biased[g], NEG_INF))

    sel = [jnp.zeros((per_group, tm), _F32) for _ in range(N_EXPERT_GROUPS)]
    idxs, picked = [], []
    for _ in range(TOP_K):
        mx = vals[0]
        for g in range(1, N_EXPERT_GROUPS):
            mx = jnp.maximum(mx, vals[g])
        m = jnp.max(mx, axis=0, keepdims=True)
        ci = jnp.where(vals[0] == m, eids[0], n_exp)
        for g in range(1, N_EXPERT_GROUPS):
            ci = jnp.minimum(ci, jnp.where(vals[g] == m, eids[g], n_exp))
        idx = jnp.min(ci, axis=0, keepdims=True)
        sc_k = jnp.zeros((1, tm), _F32)
        for g in range(N_EXPERT_GROUPS):
            oh = eids[g] == idx
            sc_k = sc_k + jnp.sum(jnp.where(oh, scores[g], 0.0), axis=0, keepdims=True)
            vals[g] = jnp.where(oh, NEG_INF, vals[g])
            sel[g] = jnp.where(oh, 1.0, sel[g])
        idxs.append(idx)
        picked.append(sc_k)
    den = picked[0]
    for k in range(1, TOP_K):
        den = den + picked[k]

    for k in range(TOP_K):
        eidx_ref[k:k + 1, :] = idxs[k]
        wts_ref[k:k + 1, :] = picked[k] / den * ROUTED_SCALE
    sel_all = jnp.concatenate(sel, axis=0)
    run_scr[...] = run_scr[...] + jnp.sum(sel_all, axis=1, keepdims=True)
    cnt_ref[...] = run_scr[...].astype(jnp.int32)

    hb = hb_scr[...]
    sg = jnp.dot(hb, swg_ref[...], preferred_element_type=_F32)
    su = jnp.dot(hb, swu_ref[...], preferred_element_type=_F32)
    sh_ref[...] = jnp.dot((_silu(sg) * su).astype(_BF16), swd_ref[...],
                          preferred_element_type=_F32).astype(_BF16)


def _router(x, mod_g, g_ffn, wr_t, br, swg_bf, swu_bf, swd_bf, *, tm):
    t, d = x.shape
    gp = tm // GROUP
    n_exp = wr_t.shape[0]
    f = swg_bf.shape[1]
    ns = d // 256
    kern = functools.partial(_router_kernel, gp=gp, d=d, n_exp=n_exp, tm=tm)
    const = lambda i: (0, 0)
    return pl.pallas_call(
        kern,
        grid=(t // tm,),
        in_specs=[pl.BlockSpec((tm, d), lambda i: (i, 0)),
                  pl.BlockSpec((gp, 3 * d), lambda i: (i, 1)),
                  pl.BlockSpec((1, d), const),
                  pl.BlockSpec((n_exp, d), const),
                  pl.BlockSpec((n_exp, 1), const),
                  pl.BlockSpec((d, f), const),
                  pl.BlockSpec((d, f), const),
                  pl.BlockSpec((f, d), const)],
        out_specs=[pl.BlockSpec((tm * ns, 128), lambda i: (i, 0)),
                   pl.BlockSpec((tm, d), lambda i: (i, 0)),
                   pl.BlockSpec((TOP_K, tm), lambda i: (0, i)),
                   pl.BlockSpec((TOP_K, tm), lambda i: (0, i)),
                   pl.BlockSpec((n_exp, 128), const)],
        out_shape=[jax.ShapeDtypeStruct((t * ns, 128), _U32),
                   jax.ShapeDtypeStruct((t, d), _BF16),
                   jax.ShapeDtypeStruct((TOP_K, t), jnp.int32),
                   jax.ShapeDtypeStruct((TOP_K, t), _F32),
                   jax.ShapeDtypeStruct((n_exp, 128), jnp.int32)],
        scratch_shapes=[pltpu.VMEM((n_exp, 128), _F32),
                        pltpu.VMEM((tm, d), _BF16),
                        pltpu.VMEM((tm, d), _F32)],
        compiler_params=_cparams(1),
        name="router_shared",
    )(x, mod_g, g_ffn.reshape(1, d), wr_t, br.reshape(n_exp, 1), swg_bf, swu_bf, swd_bf)


def _ffn_kernel(be_ref, nu_ref, idx_hbm, h_hbm, wg_ref, wu_ref, wd_ref, y_hbm,
                idx_s, xflat, yflat, xmat, wgb, wub, wdb, sem_i, sem_g, sem_s, *, bm, ns):
    b = pl.program_id(0)
    nu = nu_ref[0]
    slot = lax.rem(b, 2)
    oslot = 1 - slot
    rows = bm * ns

    def idx_copy(blk, sl):
        return pltpu.make_async_copy(idx_hbm.at[blk], idx_s.at[sl], sem_i.at[sl])

    def row_gather(sl, r):
        src = pl.multiple_of(idx_s[sl, r] * ns, ns)
        dst = pl.multiple_of((sl * bm + r) * ns, ns)
        return pltpu.make_async_copy(h_hbm.at[pl.ds(src, ns)], xflat.at[pl.ds(dst, ns)], sem_g.at[sl])

    def row_scatter(sl, r):
        src = pl.multiple_of((sl * bm + r) * ns, ns)
        dst = pl.multiple_of(idx_s[sl, bm + r] * ns, ns)
        return pltpu.make_async_copy(yflat.at[pl.ds(src, ns)], y_hbm.at[pl.ds(dst, ns)], sem_s.at[sl])

    def block_gather(sl):
        return pltpu.make_async_copy(h_hbm.at[pl.ds(0, rows)],
                                     xflat.at[pl.ds(pl.multiple_of(sl * rows, rows), rows)], sem_g.at[sl])

    def block_scatter(sl, dst0=0):
        return pltpu.make_async_copy(yflat.at[pl.ds(pl.multiple_of(sl * rows, rows), rows)],
                                     y_hbm.at[pl.ds(dst0, rows)], sem_s.at[sl])

    @pl.when(b < nu)
    def _():
        @pl.when(b == 0)
        def _():
            yflat[...] = jnp.zeros_like(yflat)
            spare0 = y_hbm.shape[0] - 2 * rows
            for sl in range(2):
                block_scatter(sl, spare0 + sl * rows).start()
            first = idx_copy(0, 0)
            first.start()
            first.wait()
            for r in range(bm):
                row_gather(0, r).start()
            idx_copy(1, 1).start()

        new_expert = (b == 0) | (be_ref[b] != be_ref[jnp.maximum(b - 1, 0)])

        @pl.when(new_expert)
        def _():
            wgb[...] = wg_ref[...].astype(_BF16)
            wub[...] = wu_ref[...].astype(_BF16)
            wdb[...] = wd_ref[...].astype(_BF16)

        block_gather(slot).wait()
        idx_copy(b + 1, oslot).wait()
        for r in range(bm):
            row_gather(oslot, r).start(priority=r % 2)
        block_scatter(slot).wait()

        cols = _load_slabs(xflat, slot * bm, bm, ns)
        for c, v in enumerate(cols):
            xmat[:, 128 * c:128 * (c + 1)] = v.astype(_BF16)
        x = xmat[...]
        hg = jnp.dot(x, wgb[...], preferred_element_type=_F32)
        hu = jnp.dot(x, wub[...], preferred_element_type=_F32)
        hb = (_silu(hg) * hu).astype(_BF16)
        _store_slabs(yflat, slot * bm, jnp.dot(hb, wdb[...], preferred_element_type=_F32), bm, ns)
        for r in range(bm):
            row_scatter(slot, r).start(priority=r % 2)
        idx_copy(b + 2, slot).start()

        @pl.when(b == nu - 1)
        def _():
            block_gather(oslot).wait()
            idx_copy(b + 2, slot).wait()
            block_scatter(oslot).wait()
            block_scatter(slot).wait()


def _routed_ffn(blk_e, n_used, idx, h_slabs, wg, wu, wd, *, layer, bm, n_out_rows):
    nb = idx.shape[0] - 1
    _, _, d, f = wg.shape
    ns = d // 256
    kern = functools.partial(_ffn_kernel, bm=bm, ns=ns)
    grid_spec = pltpu.PrefetchScalarGridSpec(
        num_scalar_prefetch=2,
        grid=(nb,),
        in_specs=[pl.BlockSpec(memory_space=pl.ANY),
                  pl.BlockSpec(memory_space=pl.ANY),
                  pl.BlockSpec((None, None, d, f), lambda b, be, nu: (layer, be[b], 0, 0)),
                  pl.BlockSpec((None, None, d, f), lambda b, be, nu: (layer, be[b], 0, 0)),
                  pl.BlockSpec((None, None, f, d), lambda b, be, nu: (layer, be[b], 0, 0))],
        out_specs=pl.BlockSpec(memory_space=pl.ANY),
        scratch_shapes=[pltpu.SMEM((2, 2 * bm), jnp.int32),
                        pltpu.VMEM((2 * bm * ns, 128), _U32),
                        pltpu.VMEM((2 * bm * ns, 128), _U32),
                        pltpu.VMEM((bm, d), _BF16),
                        pltpu.VMEM((d, f), _BF16),
                        pltpu.VMEM((d, f), _BF16),
                        pltpu.VMEM((f, d), _BF16),
                        pltpu.SemaphoreType.DMA((2,)),
                        pltpu.SemaphoreType.DMA((2,)),
                        pltpu.SemaphoreType.DMA((2,))])
    return pl.pallas_call(
        kern,
        grid_spec=grid_spec,
        out_shape=jax.ShapeDtypeStruct((n_out_rows * ns, 128), _U32),
        compiler_params=_cparams(1),
        name="routed_ffn",
    )(blk_e, n_used, idx, h_slabs, wg, wu, wd)


def _combine_kernel(*refs, d, gpt, n_prompt_tiles):
    x_ref, sh_ref, w_ref, mod_ref, gf_ref = refs[:5]
    y_refs = refs[5:5 + TOP_K]
    n_out = 1 if n_prompt_tiles is None else 2
    out_refs = refs[5 + TOP_K:5 + TOP_K + n_out]
    lo_scr, hi_scr = refs[5 + TOP_K + n_out:]
    i = pl.program_id(0)
    ns = d // 256
    tmc = x_ref.shape[0]
    acc_lo = jnp.zeros(lo_scr.shape, _F32)
    acc_hi = jnp.zeros(hi_scr.shape, _F32)
    for k in range(TOP_K):
        lo, hi = _unpack_pair(y_refs[k][...])
        wk = w_ref[:, k:k + 1]
        acc_lo = acc_lo + wk * lo
        acc_hi = acc_hi + wk * hi
    lo_scr[...] = acc_lo
    hi_scr[...] = acc_hi
    cols = ([lo_scr[pl.ds(j, tmc, stride=ns), :] for j in range(ns)]
            + [hi_scr[pl.ds(j, tmc, stride=ns), :] for j in range(ns)])
    acc = sh_ref[...].astype(_F32) + jnp.concatenate(cols, axis=1)
    groups_per_mod = mod_ref.shape[0]
    g0 = lax.rem(i, groups_per_mod // gpt) * gpt
    rows = []
    for s in range(gpt):
        sl = slice(s * GROUP, (s + 1) * GROUP)
        gate = mod_ref[pl.ds(g0 + s, 1), 2 * d:3 * d]
        xo = x_ref[sl, :] + gate * acc[sl, :]
        if n_prompt_tiles is not None:
            ms = jnp.mean(xo * xo, axis=-1, keepdims=True)
            xo = (xo * lax.rsqrt(ms + EPS)) * gf_ref[...]
        rows.append(xo)
    if n_prompt_tiles is None:
        for s in range(gpt):
            out_refs[0][s * GROUP:(s + 1) * GROUP, :] = rows[s]
    else:
        for o_ref, active in ((out_refs[0], i < n_prompt_tiles), (out_refs[1], i >= n_prompt_tiles)):
            @pl.when(active)
            def _(o_ref=o_ref):
                for s in range(gpt):
                    o_ref[s * GROUP:(s + 1) * GROUP, :] = rows[s]


def _combine(x, shared, wts, mod_g, g_final, y, *, tm, split_rows=None):
    t, d = x.shape
    tmc = min(256, tm)
    gpt = tmc // GROUP
    gp = tm // GROUP
    nt = t // tmc
    row_spec = pl.BlockSpec((tmc, d), lambda i: (i, 0))
    if split_rows is None:
        npt = None
        out_specs = row_spec
        out_shape = jax.ShapeDtypeStruct((t, d), _F32)
    else:
        npt = split_rows[0] // tmc
        out_specs = [pl.BlockSpec((tmc, d), lambda i: (jnp.minimum(i, npt - 1), 0)),
                     pl.BlockSpec((tmc, d), lambda i: (jnp.maximum(i - npt, 0), 0))]
        out_shape = [jax.ShapeDtypeStruct((rows, d), _F32) for rows in split_rows]
    kern = functools.partial(_combine_kernel, d=d, gpt=gpt, n_prompt_tiles=npt)
    ns = d // 256
    y_specs = [pl.BlockSpec((tmc * ns, 128), functools.partial(lambda i, k: (k * nt + i, 0), k=k))
               for k in range(TOP_K)]
    return pl.pallas_call(
        kern,
        grid=(nt,),
        scratch_shapes=[pltpu.VMEM((tmc * ns, 128), _F32), pltpu.VMEM((tmc * ns, 128), _F32)],
        in_specs=[row_spec,
                  row_spec,
                  pl.BlockSpec((tmc * ns, TOP_K), lambda i: (i, 0)),
                  pl.BlockSpec((gp, 3 * d), lambda i: (i // (tm // tmc), 1)),
                  pl.BlockSpec((1, d), lambda i: (0, 0))] + y_specs,
        out_specs=out_specs,
        out_shape=out_shape,
        compiler_params=_cparams(1),
        name="combine" if split_rows is None else "combine_final",
    )(x, shared, wts, mod_g, g_final.reshape(1, d), *([y] * TOP_K))


def _dispatch_tables(eidx, counts, *, bm, nb, t):
    n_exp = counts.shape[0]
    low_bits = (t * TOP_K).bit_length()
    low_mask = (1 << low_bits) - 1
    assert t * TOP_K < low_mask and (n_exp + 1) << low_bits < 2 ** 31
    tok_k = jnp.arange(t, dtype=jnp.int32)[None, :] * TOP_K + jnp.arange(TOP_K, dtype=jnp.int32)[:, None]
    real = (eidx << low_bits) + tok_k
    n_pad = (-counts) % bm
    e_col = jnp.arange(n_exp, dtype=jnp.int32)[:, None]
    pad = jnp.where(jnp.arange(bm, dtype=jnp.int32)[None, :] < n_pad[:, None],
                    (e_col << low_bits) + low_mask, (n_exp << low_bits) + low_mask)
    keys = jnp.sort(jnp.concatenate([real.reshape(-1), pad.reshape(-1)]))
    low = keys & low_mask
    is_pad = low == low_mask
    tok = low // TOP_K
    r = jnp.arange(nb * bm, dtype=jnp.int32)
    row_src = jnp.where(is_pad, r % t, tok)
    row_dst = jnp.where(is_pad, TOP_K * t + ((r // bm) % 2) * bm + (r % bm),
                        (low % TOP_K) * t + tok)
    idx = jnp.concatenate([row_src.reshape(nb, bm), row_dst.reshape(nb, bm)], axis=1)
    idx = jnp.concatenate([idx, jnp.zeros((1, 2 * bm), jnp.int32)], axis=0)
    blk_e = jnp.minimum(keys.reshape(nb, bm)[:, 0] >> low_bits, n_exp - 1)
    n_used = (jnp.sum(counts + n_pad) // bm).astype(jnp.int32).reshape(1)
    return blk_e, n_used, idx


def _moe(x, mod_g, g_ffn, g_final, router_w, router_b, wg, wu, wd, swg, swu, swd,
         *, layer, tm, bm, split_rows=None):
    t, d = x.shape
    n_exp = router_w.shape[1]
    h_slabs, shared, eidx, wts, cnt = _router(
        x, mod_g, g_ffn, router_w.T, router_b,
        swg.astype(_BF16), swu.astype(_BF16), swd.astype(_BF16), tm=tm)
    nb = (t * TOP_K) // bm + n_exp
    blk_e, n_used, idx = _dispatch_tables(eidx, cnt[:, 0], bm=bm, nb=nb, t=t)
    y = _routed_ffn(blk_e, n_used, idx, h_slabs, wg, wu, wd, layer=layer, bm=bm,
                    n_out_rows=TOP_K * t + 2 * bm)
    w_rows = jnp.repeat(wts.T, d // 256, axis=0)
    return _combine(x, shared, w_rows, mod_g, g_final, y, tm=tm, split_rows=split_rows)


def _tile_rows(seq, dec_rows):
    for tm in (512, 256, 128, 64):
        if seq % tm == 0 and dec_rows % tm == 0:
            return tm
    raise ValueError("sequence lengths must be multiples of 64")


def kernel(x_prompt, x_sample, state_conv, state_pool, c_prompt, c_sample, w_mod, b_mod, g_mix, g_ffn, g_final, conv_w_in, conv_w, conv_w_out, pool_w, pool_scale, router_w, router_b, exp_w_gate, exp_w_up, exp_w_down, sh_w_gate, sh_w_up, sh_w_down):
    bp, seq, d = x_prompt.shape
    bs, dseq, _ = x_sample.shape
    depth = w_mod.shape[0]
    n_exp = router_w.shape[2]
    assert dseq == GROUP and seq % GROUP == 0 and depth == 2
    assert state_conv.shape[1] == CONV_W - 1 and state_pool.shape[1] == POOL_HDR - 1
    assert n_exp % N_EXPERT_GROUPS == 0 and d % (128 * len(POOL_WINDOWS)) == 0
    tp, ts = bp * seq, bs * dseq
    t = tp + ts
    tm = _tile_rows(seq, ts)
    bm = min(256, t * TOP_K // n_exp)
    assert (t * TOP_K) % bm == 0
    n_prompt_tiles, tiles_per_seq = tp // tm, seq // tm
    ng_p, ng = tp // GROUP, t // GROUP

    nb_pad = -(-(bp + bs) // 8) * 8
    c_all = jnp.concatenate([c_prompt, c_sample, jnp.zeros((nb_pad - bp - bs, d), _F32)], axis=0)
    mods = _mod_rows(c_all, w_mod, b_mod)
    gps = seq // GROUP
    mod_p = jnp.broadcast_to(mods[:, :bp, None, :], (depth, bp, gps, 6 * d)).reshape(depth, bp * gps, 6 * d)
    mod_g = jnp.concatenate([mod_p, mods[:, bp:bp + bs, :]], axis=1)

    conv_tbl = jnp.pad(state_conv, ((0, 0), (CONV_HDR - (CONV_W - 1), 0), (0, 0)))
    pool_tbl = jnp.pad(state_pool, ((0, 0), (1, 0), (0, 0)))

    statics = dict(tm=tm, n_prompt_tiles=n_prompt_tiles, tiles_per_seq=tiles_per_seq)
    x, conv_tail = _conv_mixer(x_prompt.reshape(tp, d), x_sample.reshape(ts, d), mod_g[0], conv_tbl, g_mix[0],
                               conv_w_in.astype(_BF16), conv_w, conv_w_out.astype(_BF16), **statics)
    x = _moe(x, mod_g[0], g_ffn[0], g_final, router_w[0], router_b[0], exp_w_gate, exp_w_up, exp_w_down,
             sh_w_gate[0], sh_w_up[0], sh_w_down[0], layer=0, tm=tm, bm=bm)
    x, pool_tail = _pool_mixer(x, mod_g[1], pool_tbl, g_mix[1], pool_w.astype(_BF16), pool_scale, **statics)
    y_p, y_s = _moe(x, mod_g[1], g_ffn[1], g_final, router_w[1], router_b[1], exp_w_gate, exp_w_up, exp_w_down,
                    sh_w_gate[1], sh_w_up[1], sh_w_down[1], layer=1, tm=tm, bm=bm, split_rows=(tp, ts))

    return (y_p.reshape(bp, seq, d),
            y_s.reshape(bs, dseq, d),
            conv_tail[gps - 1:ng_p:gps],
            conv_tail[ng_p:],
            pool_tail[gps - 1:ng_p:gps],
            pool_tail[ng_p:])
```

```python
import functools

import jax
import jax.numpy as jnp
from jax import lax
from jax.experimental import pallas as pl
from jax.experimental.pallas import tpu as pltpu

GROUP = 64
CONV_W = 3
POOL_WINDOWS = (2, 4, 8, 16)
POOL_HDR = 16
CONV_HDR = 8
TOP_K = 8
N_EXPERT_GROUPS = 8
TOPK_GROUPS = 4
ROUTED_SCALE = 2.5
PAST_LEN = 4096
EPS = 1e-6
VMEM_LIMIT_V7X = 56 * 1024 * 1024
NEG_INF = float("-inf")

_F32 = jnp.float32
_BF16 = jnp.bfloat16


def _silu(v):
    return v * jax.nn.sigmoid(v)


_U32 = jnp.uint32
_HI_MASK = 0xFFFF0000
SLAB_COLS = 256


def _pack_pair(lo, hi):
    lo_b = lax.bitcast_convert_type(lo.astype(_BF16).astype(_F32), _U32)
    hi_b = lax.bitcast_convert_type(hi.astype(_BF16).astype(_F32), _U32)
    return hi_b | (lo_b >> 16)


def _unpack_pair(p):
    lo = lax.bitcast_convert_type(p << 16, _F32)
    hi = lax.bitcast_convert_type(p & _U32(_HI_MASK), _F32)
    return lo, hi


def _store_slab_cols(ref, row0, j, v, n_rows, ns):
    ref[pl.ds(row0 * ns + j, n_rows, stride=ns), :] = _pack_pair(v[:, :128], v[:, 128:])


def _load_slab_cols(ref, row0, j, n_rows, ns):
    return _unpack_pair(ref[pl.ds(row0 * ns + j, n_rows, stride=ns), :])


def _cparams(n_axes):
    return pltpu.CompilerParams(dimension_semantics=("arbitrary",) * n_axes,
                                vmem_limit_bytes=VMEM_LIMIT_V7X)


def _mod_kernel(c_ref, w_ref, b_ref, o_ref):
    c = c_ref[...]
    ca = _silu(c).astype(_BF16)
    o_ref[...] = jnp.dot(ca, w_ref[...].astype(_BF16), preferred_element_type=_F32) + b_ref[...]


def _mod_rows(c_all, w_mod, b_mod):
    depth, d, n6 = w_mod.shape
    nb = c_all.shape[0]
    tn = min(1024, n6)
    return pl.pallas_call(
        _mod_kernel,
        grid=(depth, n6 // tn),
        in_specs=[pl.BlockSpec((nb, d), lambda l, j: (0, 0)),
                  pl.BlockSpec((None, d, tn), lambda l, j: (l, 0, j)),
                  pl.BlockSpec((None, 1, tn), lambda l, j: (l, 0, j))],
        out_specs=pl.BlockSpec((None, nb, tn), lambda l, j: (l, 0, j)),
        out_shape=jax.ShapeDtypeStruct((depth, nb, n6), _F32),
        compiler_params=_cparams(2),
        name="adaln_rows",
    )(c_all, w_mod, b_mod.reshape(depth, 1, n6))


def _norm_mod_group(xs, mod_ref, g_ref, s, d):
    ms = jnp.mean(xs * xs, axis=-1, keepdims=True)
    xn = (xs * lax.rsqrt(ms + EPS)) * g_ref[...]
    return xn * (1.0 + mod_ref[s:s + 1, d:2 * d]) + mod_ref[s:s + 1, 0:d]


def _conv_mixer_kernel(xp_ref, xs_ref, mod_ref, st_ref, g_ref, wb_ref, wc_ref, wv_ref, cw_ref, wo_ref,
                       out_ref, tail_ref, h_scr, acc_scr, ubuf, gbuf, carry,
                       *, gp, d, n_prompt_tiles, tiles_per_seq):
    i = pl.program_id(0)
    j = pl.program_id(1)
    nj = pl.num_programs(1)
    is_sample = i >= n_prompt_tiles

    def x_rows(s):
        sl = slice(s * GROUP, (s + 1) * GROUP)
        return jnp.where(is_sample, xs_ref[sl, :], xp_ref[sl, :])

    @pl.when(j == 0)
    def _():
        for s in range(gp):
            h_scr[s * GROUP:(s + 1) * GROUP, :] = _norm_mod_group(x_rows(s), mod_ref, g_ref, s, d).astype(_BF16)
        acc_scr[...] = jnp.zeros_like(acc_scr)

    @pl.when(i == 0)
    def _():
        carry[j] = jnp.zeros((CONV_HDR, carry.shape[2]), _F32)

    hb = h_scr[...]
    bq = jnp.dot(hb, wb_ref[...], preferred_element_type=_F32)
    cq = jnp.dot(hb, wc_ref[...], preferred_element_type=_F32)
    vq = jnp.dot(hb, wv_ref[...], preferred_element_type=_F32)
    u = cq * vq

    stride = GROUP + CONV_HDR
    for s in range(gp):
        if s == 0:
            prev = jnp.where(lax.rem(i, tiles_per_seq) == 0, 0.0, carry[j])
        else:
            prev = u[s * GROUP - CONV_HDR:s * GROUP, :]
        ubuf[s * stride:s * stride + CONV_HDR, :] = jnp.where(is_sample, st_ref[s], prev)
        ubuf[s * stride + CONV_HDR:(s + 1) * stride, :] = u[s * GROUP:(s + 1) * GROUP, :]
        tail_ref[s] = u[(s + 1) * GROUP - (CONV_W - 1):(s + 1) * GROUP, :]
    carry[j] = u[gp * GROUP - CONV_HDR:gp * GROUP, :]

    w0 = cw_ref[0:1, :]
    w1 = cw_ref[1:2, :]
    w2 = cw_ref[2:3, :]
    for s in range(gp):
        base = s * stride + CONV_HDR
        u0 = ubuf[base:base + GROUP, :]
        u1 = ubuf[base - 1:base - 1 + GROUP, :]
        u2 = ubuf[base - 2:base - 2 + GROUP, :]
        conv = u2 * w0 + u1 * w1 + u0 * w2
        gbuf[s * GROUP:(s + 1) * GROUP, :] = (bq[s * GROUP:(s + 1) * GROUP, :] * conv).astype(_BF16)
    acc_scr[...] += jnp.dot(gbuf[...], wo_ref[...], preferred_element_type=_F32)

    @pl.when(j == nj - 1)
    def _():
        for s in range(gp):
            sl = slice(s * GROUP, (s + 1) * GROUP)
            out_ref[sl, :] = x_rows(s) + mod_ref[s:s + 1, 2 * d:3 * d] * acc_scr[sl, :]


def _conv_mixer(xp, xs, mod_g, st_tbl, g_mix, w_in_bf, conv_w, w_out_bf, *, tm, n_prompt_tiles, tiles_per_seq):
    d = xp.shape[1]
    t = xp.shape[0] + xs.shape[0]
    npt = n_prompt_tiles
    gp = tm // GROUP
    tn = min(512, d)
    nj = d // tn
    kern = functools.partial(_conv_mixer_kernel, gp=gp, d=d, n_prompt_tiles=n_prompt_tiles,
                             tiles_per_seq=tiles_per_seq)
    return pl.pallas_call(
        kern,
        grid=(t // tm, nj),
        in_specs=[pl.BlockSpec((tm, d), lambda i, j: (jnp.minimum(i, npt - 1), 0)),
                  pl.BlockSpec((tm, d), lambda i, j: (jnp.maximum(i - npt, 0), 0)),
                  pl.BlockSpec((gp, 3 * d), lambda i, j: (i, 0)),
                  pl.BlockSpec((gp, CONV_HDR, tn), lambda i, j: (jnp.maximum(i - npt, 0), 0, j)),
                  pl.BlockSpec((1, d), lambda i, j: (0, 0)),
                  pl.BlockSpec((d, tn), lambda i, j: (0, j)),
                  pl.BlockSpec((d, tn), lambda i, j: (0, nj + j)),
                  pl.BlockSpec((d, tn), lambda i, j: (0, 2 * nj + j)),
                  pl.BlockSpec((CONV_W, tn), lambda i, j: (0, j)),
                  pl.BlockSpec((tn, d), lambda i, j: (j, 0))],
        out_specs=[pl.BlockSpec((tm, d), lambda i, j: (i, 0)),
                   pl.BlockSpec((gp, CONV_W - 1, tn), lambda i, j: (i, 0, j))],
        out_shape=[jax.ShapeDtypeStruct((t, d), _F32),
                   jax.ShapeDtypeStruct((t // GROUP, CONV_W - 1, d), _F32)],
        scratch_shapes=[pltpu.VMEM((tm, d), _BF16),
                        pltpu.VMEM((tm, d), _F32),
                        pltpu.VMEM((gp * (GROUP + CONV_HDR), tn), _F32),
                        pltpu.VMEM((tm, tn), _BF16),
                        pltpu.VMEM((nj, CONV_HDR, tn), _F32)],
        compiler_params=_cparams(2),
        name="conv_mixer",
    )(xp, xs, mod_g, st_tbl, g_mix.reshape(1, d), w_in_bf, w_in_bf, w_in_bf, conv_w, w_out_bf)


def _pool_mixer_kernel(x_ref, mod_ref, st_ref, g_ref, pw_ref, ps_ref, out_ref, tail_ref,
                       dbuf, carry, *, gp, d, n_prompt_tiles, tiles_per_seq, tm):
    i = pl.program_id(0)
    pg = d // len(POOL_WINDOWS)
    is_sample = i >= n_prompt_tiles
    row = lax.broadcasted_iota(jnp.int32, (GROUP, 1), 0)

    @pl.when(i == 0)
    def _():
        carry[...] = jnp.zeros_like(carry)

    prev = jnp.where(lax.rem(i, tiles_per_seq) == 0, 0.0, carry[...])
    for s in range(gp):
        h = _norm_mod_group(x_ref[s * GROUP:(s + 1) * GROUP, :], mod_ref, g_ref, s, d)
        hdr = jnp.where(is_sample, st_ref[s], prev)
        he = jnp.concatenate([hdr, h], axis=0)
        pos0 = jnp.where(is_sample, PAST_LEN, lax.rem(i, tiles_per_seq) * tm + s * GROUP)
        pos1 = pos0 + row + 1
        for q, w in enumerate(POOL_WINDOWS):
            cs = slice(q * pg, (q + 1) * pg)
            sw = he[:, cs]
            span = 1
            while span < w:
                sw = sw + pltpu.roll(sw, span, 0)
                span *= 2
            inv_cnt = 1.0 / jnp.minimum(pos1, w).astype(_F32)
            pooled = sw[POOL_HDR:, :] * inv_cnt
            dbuf[s * GROUP:(s + 1) * GROUP, cs] = (pooled - h[:, cs]).astype(_BF16)
        prev = h[GROUP - POOL_HDR:, :]
        tail_ref[s] = h[GROUP - (POOL_HDR - 1):, :]
    carry[...] = prev
    for q in range(len(POOL_WINDOWS)):
        cs = slice(q * pg, (q + 1) * pg)
        y = jnp.dot(dbuf[:, cs], pw_ref[q], preferred_element_type=_F32) * ps_ref[:, cs]
        for s in range(gp):
            sl = slice(s * GROUP, (s + 1) * GROUP)
            out_ref[sl, cs] = x_ref[sl, cs] + mod_ref[s:s + 1, 2 * d + q * pg:2 * d + (q + 1) * pg] * y[sl, :]


def _pool_mixer(x, mod_g, st_tbl, g_mix, pool_w_bf, pool_scale, *, tm, n_prompt_tiles, tiles_per_seq):
    t, d = x.shape
    gp = tm // GROUP
    nq = len(POOL_WINDOWS)
    pg = d // nq
    kern = functools.partial(_pool_mixer_kernel, gp=gp, d=d, n_prompt_tiles=n_prompt_tiles,
                             tiles_per_seq=tiles_per_seq, tm=tm)
    return pl.pallas_call(
        kern,
        grid=(t // tm,),
        in_specs=[pl.BlockSpec((tm, d), lambda i: (i, 0)),
                  pl.BlockSpec((gp, 3 * d), lambda i: (i, 0)),
                  pl.BlockSpec((gp, POOL_HDR, d), lambda i: (jnp.maximum(i - n_prompt_tiles, 0), 0, 0)),
                  pl.BlockSpec((1, d), lambda i: (0, 0)),
                  pl.BlockSpec((nq, pg, pg), lambda i: (0, 0, 0)),
                  pl.BlockSpec((1, d), lambda i: (0, 0))],
        out_specs=[pl.BlockSpec((tm, d), lambda i: (i, 0)),
                   pl.BlockSpec((gp, POOL_HDR - 1, d), lambda i: (i, 0, 0))],
        out_shape=[jax.ShapeDtypeStruct((t, d), _F32),
                   jax.ShapeDtypeStruct((t // GROUP, POOL_HDR - 1, d), _F32)],
        scratch_shapes=[pltpu.VMEM((tm, d), _BF16),
                        pltpu.VMEM((POOL_HDR, d), _F32)],
        compiler_params=_cparams(1),
        name="pool_mixer",
    )(x, mod_g, st_tbl, g_mix.reshape(1, d), pool_w_bf, pool_scale.reshape(1, d))


def _router_kernel(x_ref, mod_ref, g_ref, wr_ref, br_ref, swg_ref, swu_ref, swd_ref,
                   hs_ref, sh_ref, eidx_ref, wts_ref, cnt_ref, run_scr, hb_scr, hf_scr,
                   *, gp, d, n_exp, tm):
    i = pl.program_id(0)
    per_group = n_exp // N_EXPERT_GROUPS
    ns = d // 256

    @pl.when(i == 0)
    def _():
        run_scr[...] = jnp.zeros_like(run_scr)

    for s in range(gp):
        sl = slice(s * GROUP, (s + 1) * GROUP)
        h = _norm_mod_group(x_ref[sl, :], mod_ref, g_ref, s, d)
        hf_scr[sl, :] = h
        hb_scr[sl, :] = h.astype(_BF16)
        for j in range(ns):
            _store_slab_cols(hs_ref, s * GROUP, j, h[:, SLAB_COLS * j:SLAB_COLS * (j + 1)], GROUP, ns)

    logits = lax.dot_general(wr_ref[...], hf_scr[...], (((1,), (1,)), ((), ())),
                             precision=lax.Precision.HIGHEST, preferred_element_type=_F32)
    sub = lax.broadcasted_iota(jnp.int32, (per_group, tm), 0)
    scores, biased, eids = [], [], []
    for g in range(N_EXPERT_GROUPS):
        rs = slice(g * per_group, (g + 1) * per_group)
        sc = jax.nn.sigmoid(logits[rs, :])
        scores.append(sc)
        biased.append(sc + br_ref[rs, :])
        eids.append(sub + g * per_group)

    gscore = []
    for g in range(N_EXPERT_GROUPS):
        m1 = jnp.max(biased[g], axis=0, keepdims=True)
        i1 = jnp.min(jnp.where(biased[g] == m1, sub, per_group), axis=0, keepdims=True)
        m2 = jnp.max(jnp.where(sub == i1, NEG_INF, biased[g]), axis=0, keepdims=True)
        gscore.append(m1 + m2)
    vals = []
    for g in range(N_EXPERT_GROUPS):
        beaten = jnp.zeros((1, tm), jnp.int32)
        for g2 in range(N_EXPERT_GROUPS):
            if g2 == g:
                continue
            beat = (gscore[g2] >= gscore[g]) if g2 < g else (gscore[g2] > gscore[g])
            beaten = beaten + beat.astype(jnp.int32)
        vals.append(jnp.where(beaten < TOPK_GROUPS, biased[g], NEG_INF))

    sel = [jnp.zeros((per_group, tm), _F32) for _ in range(N_EXPERT_GROUPS)]
    idxs, picked = [], []
    for _ in range(TOP_K):
        mx = vals[0]
        for g in range(1, N_EXPERT_GROUPS):
            mx = jnp.maximum(mx, vals[g])
        m = jnp.max(mx, axis=0, keepdims=True)
        ci = jnp.where(vals[0] == m, eids[0], n_exp)
        for g in range(1, N_EXPERT_GROUPS):
            ci = jnp.minimum(ci, jnp.where(vals[g] == m, eids[g], n_exp))
        idx = jnp.min(ci, axis=0, keepdims=True)
        sc_k = jnp.zeros((1, tm), _F32)
        for g in range(N_EXPERT_GROUPS):
            oh = eids[g] == idx
            sc_k = sc_k + jnp.sum(jnp.where(oh, scores[g], 0.0), axis=0, keepdims=True)
            vals[g] = jnp.where(oh, NEG_INF, vals[g])
            sel[g] = jnp.where(oh, 1.0, sel[g])
        idxs.append(idx)
        picked.append(sc_k)
    den = picked[0]
    for k in range(1, TOP_K):
        den = den + picked[k]

    for k in range(TOP_K):
        eidx_ref[k:k + 1, :] = idxs[k]
        wts_ref[k:k + 1, :] = picked[k] / den * ROUTED_SCALE
    sel_all = jnp.concatenate(sel, axis=0)
    run_scr[...] = run_scr[...] + jnp.sum(sel_all, axis=1, keepdims=True)
    cnt_ref[...] = run_scr[...].astype(jnp.int32)

    hb = hb_scr[...]
    sg = jnp.dot(hb, swg_ref[...], preferred_element_type=_F32)
    su = jnp.dot(hb, swu_ref[...], preferred_element_type=_F32)
    sh_ref[...] = jnp.dot((_silu(sg) * su).astype(_BF16), swd_ref[...],
                          preferred_element_type=_F32).astype(_BF16)


def _router(x, mod_g, g_ffn, wr_t, br, swg_bf, swu_bf, swd_bf, *, tm):
    t, d = x.shape
    gp = tm // GROUP
    n_exp = wr_t.shape[0]
    f = swg_bf.shape[1]
    ns = d // 256
    kern = functools.partial(_router_kernel, gp=gp, d=d, n_exp=n_exp, tm=tm)
    const = lambda i: (0, 0)
    return pl.pallas_call(
        kern,
        grid=(t // tm,),
        in_specs=[pl.BlockSpec((tm, d), lambda i: (i, 0)),
                  pl.BlockSpec((gp, 3 * d), lambda i: (i, 1)),
                  pl.BlockSpec((1, d), const),
                  pl.BlockSpec((n_exp, d), const),
                  pl.BlockSpec((n_exp, 1), const),
                  pl.BlockSpec((d, f), const),
                  pl.BlockSpec((d, f), const),
                  pl.BlockSpec((f, d), const)],
        out_specs=[pl.BlockSpec((tm * ns, 128), lambda i: (i, 0)),
                   pl.BlockSpec((tm, d), lambda i: (i, 0)),
                   pl.BlockSpec((TOP_K, tm), lambda i: (0, i)),
                   pl.BlockSpec((TOP_K, tm), lambda i: (0, i)),
                   pl.BlockSpec((n_exp, 128), const)],
        out_shape=[jax.ShapeDtypeStruct((t * ns, 128), _U32),
                   jax.ShapeDtypeStruct((t, d), _BF16),
                   jax.ShapeDtypeStruct((TOP_K, t), jnp.int32),
                   jax.ShapeDtypeStruct((TOP_K, t), _F32),
                   jax.ShapeDtypeStruct((n_exp, 128), jnp.int32)],
        scratch_shapes=[pltpu.VMEM((n_exp, 128), _F32),
                        pltpu.VMEM((tm, d), _BF16),
                        pltpu.VMEM((tm, d), _F32)],
        compiler_params=_cparams(1),
        name="router_shared",
    )(x, mod_g, g_ffn.reshape(1, d), wr_t, br.reshape(n_exp, 1), swg_bf, swu_bf, swd_bf)


def _ffn_kernel(be_ref, nu_ref, idx_hbm, h_hbm, wg_ref, wu_ref, wd_ref, y_hbm,
                idx_s, xflat, yflat, xmat, hbuf, wgb, wub, wdb, sem_i, sem_g, sem_s, *, bm, ns):
    b = pl.program_id(0)
    nu = nu_ref[0]
    nb = pl.num_programs(0)
    slot = lax.rem(b, 2)
    oslot = 1 - slot
    rows = bm * ns
    f = wgb.shape[1]

    def idx_copy(blk, sl):
        return pltpu.make_async_copy(idx_hbm.at[blk], idx_s.at[sl], sem_i.at[sl])

    def row_gather(sl, isl, r):
        src = pl.multiple_of(idx_s[isl, r] * ns, ns)
        dst = pl.multiple_of((sl * bm + r) * ns, ns)
        return pltpu.make_async_copy(h_hbm.at[pl.ds(src, ns)], xflat.at[pl.ds(dst, ns)], sem_g.at[sl])

    def row_scatter(sl, isl, r):
        src = pl.multiple_of((sl * bm + r) * ns, ns)
        dst = pl.multiple_of(idx_s[isl, bm + r] * ns, ns)
        return pltpu.make_async_copy(yflat.at[pl.ds(src, ns)], y_hbm.at[pl.ds(dst, ns)], sem_s.at[sl])

    def block_gather(sl):
        return pltpu.make_async_copy(h_hbm.at[pl.ds(0, rows)],
                                     xflat.at[pl.ds(pl.multiple_of(sl * rows, rows), rows)], sem_g.at[sl])

    def block_scatter(sl, dst0=0):
        return pltpu.make_async_copy(yflat.at[pl.ds(pl.multiple_of(sl * rows, rows), rows)],
                                     y_hbm.at[pl.ds(dst0, rows)], sem_s.at[sl])

    @pl.when(b < nu)
    def _():
        @pl.when(b == 0)
        def _():
            yflat[...] = jnp.zeros_like(yflat)
            block_scatter(0, y_hbm.shape[0] - 2 * rows).start()
            before = idx_copy(nb + 1, 3)
            first = idx_copy(0, 0)
            before.start()
            first.start()
            before.wait()
            first.wait()
            for r in range(bm):
                row_gather(0, 0, r).start()
            idx_copy(1, 1).start()

        new_expert = (b == 0) | (be_ref[b] != be_ref[jnp.maximum(b - 1, 0)])

        @pl.when(new_expert)
        def _():
            wgb[...] = wg_ref[...].astype(_BF16)
            wub[...] = wu_ref[...].astype(_BF16)
            wdb[...] = wd_ref[...].astype(_BF16)

        i_next = lax.rem(b + 1, 4)
        i_prev = lax.rem(b + 3, 4)
        block_gather(slot).wait()
        idx_copy(b + 1, i_next).wait()

        per_piece = bm // ns
        for j in range(ns):
            lo, hi = _load_slab_cols(xflat, slot * bm, j, bm, ns)
            xmat[:, SLAB_COLS * j:SLAB_COLS * j + 128] = lo.astype(_BF16)
            xmat[:, SLAB_COLS * j + 128:SLAB_COLS * (j + 1)] = hi.astype(_BF16)
            for r in range(j * per_piece, (j + 1) * per_piece):
                row_gather(oslot, i_next, r).start()

        x = xmat[...]
        n_fc = f // SLAB_COLS
        per_piece = bm // n_fc
        for c in range(n_fc):
            cs = slice(SLAB_COLS * c, SLAB_COLS * (c + 1))
            hg = jnp.dot(x, wgb[:, cs], preferred_element_type=_F32)
            hu = jnp.dot(x, wub[:, cs], preferred_element_type=_F32)
            hbuf[:, cs] = (_silu(hg) * hu).astype(_BF16)
            for r in range(c * per_piece, (c + 1) * per_piece):
                row_scatter(oslot, i_prev, r).start()

        block_scatter(slot).wait()
        hb = hbuf[...]
        for j in range(ns):
            o = jnp.dot(hb, wdb[:, SLAB_COLS * j:SLAB_COLS * (j + 1)], preferred_element_type=_F32)
            _store_slab_cols(yflat, slot * bm, j, o, bm, ns)
        idx_copy(b + 2, lax.rem(b + 2, 4)).start()

        @pl.when(b == nu - 1)
        def _():
            for r in range(bm):
                row_scatter(slot, lax.rem(b, 4), r).start()
            block_gather(oslot).wait()
            idx_copy(b + 2, lax.rem(b + 2, 4)).wait()
            block_scatter(oslot).wait()
            block_scatter(slot).wait()


def _routed_ffn(blk_e, n_used, idx, h_slabs, wg, wu, wd, *, layer, bm, n_out_rows):
    nb = idx.shape[0] - 2
    _, _, d, f = wg.shape
    ns = d // 256
    kern = functools.partial(_ffn_kernel, bm=bm, ns=ns)
    grid_spec = pltpu.PrefetchScalarGridSpec(
        num_scalar_prefetch=2,
        grid=(nb,),
        in_specs=[pl.BlockSpec(memory_space=pl.ANY),
                  pl.BlockSpec(memory_space=pl.ANY),
                  pl.BlockSpec((None, None, d, f), lambda b, be, nu: (layer, be[b], 0, 0)),
                  pl.BlockSpec((None, None, d, f), lambda b, be, nu: (layer, be[b], 0, 0)),
                  pl.BlockSpec((None, None, f, d), lambda b, be, nu: (layer, be[b], 0, 0))],
        out_specs=pl.BlockSpec(memory_space=pl.ANY),
        scratch_shapes=[pltpu.SMEM((4, 2 * bm), jnp.int32),
                        pltpu.VMEM((2 * bm * ns, 128), _U32),
                        pltpu.VMEM((2 * bm * ns, 128), _U32),
                        pltpu.VMEM((bm, d), _BF16),
                        pltpu.VMEM((bm, f), _BF16),
                        pltpu.VMEM((d, f), _BF16),
                        pltpu.VMEM((d, f), _BF16),
                        pltpu.VMEM((f, d), _BF16),
                        pltpu.SemaphoreType.DMA((4,)),
                        pltpu.SemaphoreType.DMA((2,)),
                        pltpu.SemaphoreType.DMA((2,))])
    return pl.pallas_call(
        kern,
        grid_spec=grid_spec,
        out_shape=jax.ShapeDtypeStruct((n_out_rows * ns, 128), _U32),
        compiler_params=_cparams(1),
        name="routed_ffn",
    )(blk_e, n_used, idx, h_slabs, wg, wu, wd)


def _combine_kernel(*refs, d, gpt, n_prompt_tiles):
    x_ref, sh_ref, w_ref, mod_ref, gf_ref = refs[:5]
    y_refs = refs[5:5 + TOP_K]
    n_out = 1 if n_prompt_tiles is None else 2
    out_refs = refs[5 + TOP_K:5 + TOP_K + n_out]
    lo_scr, hi_scr = refs[5 + TOP_K + n_out:]
    i = pl.program_id(0)
    ns = d // 256
    tmc = x_ref.shape[0]
    acc_lo = jnp.zeros(lo_scr.shape, _F32)
    acc_hi = jnp.zeros(hi_scr.shape, _F32)
    for k in range(TOP_K):
        lo, hi = _unpack_pair(y_refs[k][...])
        wk = w_ref[:, k:k + 1]
        acc_lo = acc_lo + wk * lo
        acc_hi = acc_hi + wk * hi
    lo_scr[...] = acc_lo
    hi_scr[...] = acc_hi
    cols = []
    for j in range(ns):
        cols += [lo_scr[pl.ds(j, tmc, stride=ns), :], hi_scr[pl.ds(j, tmc, stride=ns), :]]
    acc = sh_ref[...].astype(_F32) + jnp.concatenate(cols, axis=1)
    groups_per_mod = mod_ref.shape[0]
    g0 = lax.rem(i, groups_per_mod // gpt) * gpt
    rows = []
    for s in range(gpt):
        sl = slice(s * GROUP, (s + 1) * GROUP)
        gate = mod_ref[pl.ds(g0 + s, 1), 2 * d:3 * d]
        xo = x_ref[sl, :] + gate * acc[sl, :]
        if n_prompt_tiles is not None:
            ms = jnp.mean(xo * xo, axis=-1, keepdims=True)
            xo = (xo * lax.rsqrt(ms + EPS)) * gf_ref[...]
        rows.append(xo)
    if n_prompt_tiles is None:
        for s in range(gpt):
            out_refs[0][s * GROUP:(s + 1) * GROUP, :] = rows[s]
    else:
        for o_ref, active in ((out_refs[0], i < n_prompt_tiles), (out_refs[1], i >= n_prompt_tiles)):
            @pl.when(active)
            def _(o_ref=o_ref):
                for s in range(gpt):
                    o_ref[s * GROUP:(s + 1) * GROUP, :] = rows[s]


def _combine(x, shared, wts, mod_g, g_final, y, *, tm, split_rows=None):
    t, d = x.shape
    tmc = min(256, tm)
    gpt = tmc // GROUP
    gp = tm // GROUP
    nt = t // tmc
    row_spec = pl.BlockSpec((tmc, d), lambda i: (i, 0))
    if split_rows is None:
        npt = None
        out_specs = row_spec
        out_shape = jax.ShapeDtypeStruct((t, d), _F32)
    else:
        npt = split_rows[0] // tmc
        out_specs = [pl.BlockSpec((tmc, d), lambda i: (jnp.minimum(i, npt - 1), 0)),
                     pl.BlockSpec((tmc, d), lambda i: (jnp.maximum(i - npt, 0), 0))]
        out_shape = [jax.ShapeDtypeStruct((rows, d), _F32) for rows in split_rows]
    kern = functools.partial(_combine_kernel, d=d, gpt=gpt, n_prompt_tiles=npt)
    ns = d // 256
    y_specs = [pl.BlockSpec((tmc * ns, 128), functools.partial(lambda i, k: (k * nt + i, 0), k=k))
               for k in range(TOP_K)]
    return pl.pallas_call(
        kern,
        grid=(nt,),
        scratch_shapes=[pltpu.VMEM((tmc * ns, 128), _F32), pltpu.VMEM((tmc * ns, 128), _F32)],
        in_specs=[row_spec,
                  row_spec,
                  pl.BlockSpec((tmc * ns, TOP_K), lambda i: (i, 0)),
                  pl.BlockSpec((gp, 3 * d), lambda i: (i // (tm // tmc), 1)),
                  pl.BlockSpec((1, d), lambda i: (0, 0))] + y_specs,
        out_specs=out_specs,
        out_shape=out_shape,
        compiler_params=_cparams(1),
        name="combine" if split_rows is None else "combine_final",
    )(x, shared, wts, mod_g, g_final.reshape(1, d), *([y] * TOP_K))


def _dispatch_tables(eidx, counts, *, bm, nb, t):
    n_exp = counts.shape[0]
    low_bits = (t * TOP_K).bit_length()
    low_mask = (1 << low_bits) - 1
    assert t * TOP_K < low_mask and (n_exp + 1) << low_bits < 2 ** 31
    tok_k = jnp.arange(t, dtype=jnp.int32)[None, :] * TOP_K + jnp.arange(TOP_K, dtype=jnp.int32)[:, None]
    real = (eidx << low_bits) + tok_k
    n_pad = (-counts) % bm
    e_col = jnp.arange(n_exp, dtype=jnp.int32)[:, None]
    pad = jnp.where(jnp.arange(bm, dtype=jnp.int32)[None, :] < n_pad[:, None],
                    (e_col << low_bits) + low_mask, (n_exp << low_bits) + low_mask)
    keys = jnp.sort(jnp.concatenate([real.reshape(-1), pad.reshape(-1)]))
    low = keys & low_mask
    is_pad = low == low_mask
    tok = low // TOP_K
    r = jnp.arange(nb * bm, dtype=jnp.int32)
    row_src = jnp.where(is_pad, r % t, tok)
    row_dst = jnp.where(is_pad, TOP_K * t + ((r // bm) % 2) * bm + (r % bm),
                        (low % TOP_K) * t + tok)
    idx = jnp.concatenate([row_src.reshape(nb, bm), row_dst.reshape(nb, bm)], axis=1)
    before = jnp.concatenate([jnp.zeros((bm,), jnp.int32), TOP_K * t + bm + jnp.arange(bm, dtype=jnp.int32)])
    idx = jnp.concatenate([idx, jnp.zeros((1, 2 * bm), jnp.int32), before[None, :]], axis=0)
    blk_e = jnp.minimum(keys.reshape(nb, bm)[:, 0] >> low_bits, n_exp - 1)
    n_used = (jnp.sum(counts + n_pad) // bm).astype(jnp.int32).reshape(1)
    return blk_e, n_used, idx


def _moe(x, mod_g, g_ffn, g_final, router_w, router_b, wg, wu, wd, swg, swu, swd,
         *, layer, tm, bm, split_rows=None):
    t, d = x.shape
    n_exp = router_w.shape[1]
    h_slabs, shared, eidx, wts, cnt = _router(
        x, mod_g, g_ffn, router_w.T, router_b,
        swg.astype(_BF16), swu.astype(_BF16), swd.astype(_BF16), tm=tm)
    nb = (t * TOP_K) // bm + n_exp
    blk_e, n_used, idx = _dispatch_tables(eidx, cnt[:, 0], bm=bm, nb=nb, t=t)
    y = _routed_ffn(blk_e, n_used, idx, h_slabs, wg, wu, wd, layer=layer, bm=bm,
                    n_out_rows=TOP_K * t + 2 * bm)
    w_rows = jnp.repeat(wts.T, d // 256, axis=0)
    return _combine(x, shared, w_rows, mod_g, g_final, y, tm=tm, split_rows=split_rows)


def _tile_rows(seq, dec_rows):
    for tm in (512, 256, 128, 64):
        if seq % tm == 0 and dec_rows % tm == 0:
            return tm
    raise ValueError("sequence lengths must be multiples of 64")


def kernel(x_prompt, x_sample, state_conv, state_pool, c_prompt, c_sample, w_mod, b_mod, g_mix, g_ffn, g_final, conv_w_in, conv_w, conv_w_out, pool_w, pool_scale, router_w, router_b, exp_w_gate, exp_w_up, exp_w_down, sh_w_gate, sh_w_up, sh_w_down):
    bp, seq, d = x_prompt.shape
    bs, dseq, _ = x_sample.shape
    depth = w_mod.shape[0]
    n_exp = router_w.shape[2]
    assert dseq == GROUP and seq % GROUP == 0 and depth == 2
    assert state_conv.shape[1] == CONV_W - 1 and state_pool.shape[1] == POOL_HDR - 1
    assert n_exp % N_EXPERT_GROUPS == 0 and d % (128 * len(POOL_WINDOWS)) == 0
    tp, ts = bp * seq, bs * dseq
    t = tp + ts
    tm = _tile_rows(seq, ts)
    bm = min(256, t * TOP_K // n_exp)
    assert (t * TOP_K) % bm == 0
    n_prompt_tiles, tiles_per_seq = tp // tm, seq // tm
    ng_p, ng = tp // GROUP, t // GROUP

    nb_pad = -(-(bp + bs) // 8) * 8
    c_all = jnp.concatenate([c_prompt, c_sample, jnp.zeros((nb_pad - bp - bs, d), _F32)], axis=0)
    mods = _mod_rows(c_all, w_mod, b_mod)
    gps = seq // GROUP
    mod_p = jnp.broadcast_to(mods[:, :bp, None, :], (depth, bp, gps, 6 * d)).reshape(depth, bp * gps, 6 * d)
    mod_g = jnp.concatenate([mod_p, mods[:, bp:bp + bs, :]], axis=1)

    conv_tbl = jnp.pad(state_conv, ((0, 0), (CONV_HDR - (CONV_W - 1), 0), (0, 0)))
    pool_tbl = jnp.pad(state_pool, ((0, 0), (1, 0), (0, 0)))

    statics = dict(tm=tm, n_prompt_tiles=n_prompt_tiles, tiles_per_seq=tiles_per_seq)
    x, conv_tail = _conv_mixer(x_prompt.reshape(tp, d), x_sample.reshape(ts, d), mod_g[0], conv_tbl, g_mix[0],
                               conv_w_in.astype(_BF16), conv_w, conv_w_out.astype(_BF16), **statics)
    x = _moe(x, mod_g[0], g_ffn[0], g_final, router_w[0], router_b[0], exp_w_gate, exp_w_up, exp_w_down,
             sh_w_gate[0], sh_w_up[0], sh_w_down[0], layer=0, tm=tm, bm=bm)
    x, pool_tail = _pool_mixer(x, mod_g[1], pool_tbl, g_mix[1], pool_w.astype(_BF16), pool_scale, **statics)
    y_p, y_s = _moe(x, mod_g[1], g_ffn[1], g_final, router_w[1], router_b[1], exp_w_gate, exp_w_up, exp_w_down,
                    sh_w_gate[1], sh_w_up[1], sh_w_down[1], layer=1, tm=tm, bm=bm, split_rows=(tp, ts))

    return (y_p.reshape(bp, seq, d),
            y_s.reshape(bs, dseq, d),
            conv_tail[gps - 1:ng_p:gps],
            conv_tail[ng_p:],
            pool_tail[gps - 1:ng_p:gps],
            pool_tail[ng_p:])
```

```python
import functools

import jax
import jax.numpy as jnp
from jax import lax
from jax.experimental import pallas as pl
from jax.experimental.pallas import tpu as pltpu

GROUP = 64
CONV_W = 3
POOL_WINDOWS = (2, 4, 8, 16)
POOL_HDR = 16
CONV_HDR = 8
TOP_K = 8
N_EXPERT_GROUPS = 8
TOPK_GROUPS = 4
ROUTED_SCALE = 2.5
PAST_LEN = 4096
EPS = 1e-6
VMEM_LIMIT_V7X = 56 * 1024 * 1024
NEG_INF = float("-inf")

_F32 = jnp.float32
_BF16 = jnp.bfloat16


def _silu(v):
    return v * jax.nn.sigmoid(v)


_U32 = jnp.uint32
_HI_MASK = 0xFFFF0000
SLAB_COLS = 256
RUN_ALIGN = 8
COMBINE_ROWS = 256


def _pack_pair(lo, hi):
    lo_b = lax.bitcast_convert_type(lo.astype(_BF16).astype(_F32), _U32)
    hi_b = lax.bitcast_convert_type(hi.astype(_BF16).astype(_F32), _U32)
    return hi_b | (lo_b >> 16)


def _unpack_pair(p):
    lo = lax.bitcast_convert_type(p << 16, _F32)
    hi = lax.bitcast_convert_type(p & _U32(_HI_MASK), _F32)
    return lo, hi


def _store_slab_cols(ref, row0, j, v, n_rows, ns):
    ref[pl.ds(row0 * ns + j, n_rows, stride=ns), :] = _pack_pair(v[:, :128], v[:, 128:])


def _load_slab_cols(ref, row0, j, n_rows, ns):
    return _unpack_pair(ref[pl.ds(row0 * ns + j, n_rows, stride=ns), :])


def _cparams(n_axes):
    return pltpu.CompilerParams(dimension_semantics=("arbitrary",) * n_axes,
                                vmem_limit_bytes=VMEM_LIMIT_V7X)


def _mod_kernel(c_ref, w_ref, b_ref, o_ref):
    c = c_ref[...]
    ca = _silu(c).astype(_BF16)
    o_ref[...] = jnp.dot(ca, w_ref[...].astype(_BF16), preferred_element_type=_F32) + b_ref[...]


def _mod_rows(c_all, w_mod, b_mod):
    depth, d, n6 = w_mod.shape
    nb = c_all.shape[0]
    tn = min(1024, n6)
    return pl.pallas_call(
        _mod_kernel,
        grid=(depth, n6 // tn),
        in_specs=[pl.BlockSpec((nb, d), lambda l, j: (0, 0)),
                  pl.BlockSpec((None, d, tn), lambda l, j: (l, 0, j)),
                  pl.BlockSpec((None, 1, tn), lambda l, j: (l, 0, j))],
        out_specs=pl.BlockSpec((None, nb, tn), lambda l, j: (l, 0, j)),
        out_shape=jax.ShapeDtypeStruct((depth, nb, n6), _F32),
        compiler_params=_cparams(2),
        name="adaln_rows",
    )(c_all, w_mod, b_mod.reshape(depth, 1, n6))


def _norm_mod_group(xs, mod_ref, g_ref, s, d):
    ms = jnp.mean(xs * xs, axis=-1, keepdims=True)
    xn = (xs * lax.rsqrt(ms + EPS)) * g_ref[...]
    return xn * (1.0 + mod_ref[s:s + 1, d:2 * d]) + mod_ref[s:s + 1, 0:d]


def _conv_mixer_kernel(xp_ref, xs_ref, mod_ref, st_ref, g_ref, wb_ref, wc_ref, wv_ref, cw_ref, wo_ref,
                       out_ref, tail_ref, h_scr, acc_scr, ubuf, gbuf, carry,
                       *, gp, d, n_prompt_tiles, tiles_per_seq):
    i = pl.program_id(0)
    j = pl.program_id(1)
    nj = pl.num_programs(1)
    is_sample = i >= n_prompt_tiles

    def x_rows(s):
        sl = slice(s * GROUP, (s + 1) * GROUP)
        return jnp.where(is_sample, xs_ref[sl, :], xp_ref[sl, :])

    @pl.when(j == 0)
    def _():
        for s in range(gp):
            h_scr[s * GROUP:(s + 1) * GROUP, :] = _norm_mod_group(x_rows(s), mod_ref, g_ref, s, d).astype(_BF16)
        acc_scr[...] = jnp.zeros_like(acc_scr)

    @pl.when(i == 0)
    def _():
        carry[j] = jnp.zeros((CONV_HDR, carry.shape[2]), _F32)

    hb = h_scr[...]
    bq = jnp.dot(hb, wb_ref[...], preferred_element_type=_F32)
    cq = jnp.dot(hb, wc_ref[...], preferred_element_type=_F32)
    vq = jnp.dot(hb, wv_ref[...], preferred_element_type=_F32)
    u = cq * vq

    stride = GROUP + CONV_HDR
    for s in range(gp):
        if s == 0:
            prev = jnp.where(lax.rem(i, tiles_per_seq) == 0, 0.0, carry[j])
        else:
            prev = u[s * GROUP - CONV_HDR:s * GROUP, :]
        ubuf[s * stride:s * stride + CONV_HDR, :] = jnp.where(is_sample, st_ref[s], prev)
        ubuf[s * stride + CONV_HDR:(s + 1) * stride, :] = u[s * GROUP:(s + 1) * GROUP, :]
        tail_ref[s] = u[(s + 1) * GROUP - (CONV_W - 1):(s + 1) * GROUP, :]
    carry[j] = u[gp * GROUP - CONV_HDR:gp * GROUP, :]

    w0 = cw_ref[0:1, :]
    w1 = cw_ref[1:2, :]
    w2 = cw_ref[2:3, :]
    for s in range(gp):
        base = s * stride + CONV_HDR
        u0 = ubuf[base:base + GROUP, :]
        u1 = ubuf[base - 1:base - 1 + GROUP, :]
        u2 = ubuf[base - 2:base - 2 + GROUP, :]
        conv = u2 * w0 + u1 * w1 + u0 * w2
        gbuf[s * GROUP:(s + 1) * GROUP, :] = (bq[s * GROUP:(s + 1) * GROUP, :] * conv).astype(_BF16)
    acc_scr[...] += jnp.dot(gbuf[...], wo_ref[...], preferred_element_type=_F32)

    @pl.when(j == nj - 1)
    def _():
        for s in range(gp):
            sl = slice(s * GROUP, (s + 1) * GROUP)
            out_ref[sl, :] = x_rows(s) + mod_ref[s:s + 1, 2 * d:3 * d] * acc_scr[sl, :]


def _conv_mixer(xp, xs, mod_g, st_tbl, g_mix, w_in_bf, conv_w, w_out_bf, *, tm, n_prompt_tiles, tiles_per_seq):
    d = xp.shape[1]
    t = xp.shape[0] + xs.shape[0]
    npt = n_prompt_tiles
    gp = tm // GROUP
    tn = min(512, d)
    nj = d // tn
    kern = functools.partial(_conv_mixer_kernel, gp=gp, d=d, n_prompt_tiles=n_prompt_tiles,
                             tiles_per_seq=tiles_per_seq)
    return pl.pallas_call(
        kern,
        grid=(t // tm, nj),
        in_specs=[pl.BlockSpec((tm, d), lambda i, j: (jnp.minimum(i, npt - 1), 0)),
                  pl.BlockSpec((tm, d), lambda i, j: (jnp.maximum(i - npt, 0), 0)),
                  pl.BlockSpec((gp, 3 * d), lambda i, j: (i, 0)),
                  pl.BlockSpec((gp, CONV_HDR, tn), lambda i, j: (jnp.maximum(i - npt, 0), 0, j)),
                  pl.BlockSpec((1, d), lambda i, j: (0, 0)),
                  pl.BlockSpec((d, tn), lambda i, j: (0, j)),
                  pl.BlockSpec((d, tn), lambda i, j: (0, nj + j)),
                  pl.BlockSpec((d, tn), lambda i, j: (0, 2 * nj + j)),
                  pl.BlockSpec((CONV_W, tn), lambda i, j: (0, j)),
                  pl.BlockSpec((tn, d), lambda i, j: (j, 0))],
        out_specs=[pl.BlockSpec((tm, d), lambda i, j: (i, 0)),
                   pl.BlockSpec((gp, CONV_W - 1, tn), lambda i, j: (i, 0, j))],
        out_shape=[jax.ShapeDtypeStruct((t, d), _F32),
                   jax.ShapeDtypeStruct((t // GROUP, CONV_W - 1, d), _F32)],
        scratch_shapes=[pltpu.VMEM((tm, d), _BF16),
                        pltpu.VMEM((tm, d), _F32),
                        pltpu.VMEM((gp * (GROUP + CONV_HDR), tn), _F32),
                        pltpu.VMEM((tm, tn), _BF16),
                        pltpu.VMEM((nj, CONV_HDR, tn), _F32)],
        compiler_params=_cparams(2),
        name="conv_mixer",
    )(xp, xs, mod_g, st_tbl, g_mix.reshape(1, d), w_in_bf, w_in_bf, w_in_bf, conv_w, w_out_bf)


def _pool_mixer_kernel(x_ref, mod_ref, st_ref, g_ref, pw_ref, ps_ref, out_ref, tail_ref,
                       dbuf, carry, *, gp, d, n_prompt_tiles, tiles_per_seq, tm):
    i = pl.program_id(0)
    pg = d // len(POOL_WINDOWS)
    is_sample = i >= n_prompt_tiles
    row = lax.broadcasted_iota(jnp.int32, (GROUP, 1), 0)

    @pl.when(i == 0)
    def _():
        carry[...] = jnp.zeros_like(carry)

    prev = jnp.where(lax.rem(i, tiles_per_seq) == 0, 0.0, carry[...])
    for s in range(gp):
        h = _norm_mod_group(x_ref[s * GROUP:(s + 1) * GROUP, :], mod_ref, g_ref, s, d)
        hdr = jnp.where(is_sample, st_ref[s], prev)
        he = jnp.concatenate([hdr, h], axis=0)
        pos0 = jnp.where(is_sample, PAST_LEN, lax.rem(i, tiles_per_seq) * tm + s * GROUP)
        pos1 = pos0 + row + 1
        for q, w in enumerate(POOL_WINDOWS):
            cs = slice(q * pg, (q + 1) * pg)
            sw = he[:, cs]
            span = 1
            while span < w:
                sw = sw + pltpu.roll(sw, span, 0)
                span *= 2
            inv_cnt = 1.0 / jnp.minimum(pos1, w).astype(_F32)
            pooled = sw[POOL_HDR:, :] * inv_cnt
            dbuf[s * GROUP:(s + 1) * GROUP, cs] = (pooled - h[:, cs]).astype(_BF16)
        prev = h[GROUP - POOL_HDR:, :]
        tail_ref[s] = h[GROUP - (POOL_HDR - 1):, :]
    carry[...] = prev
    for q in range(len(POOL_WINDOWS)):
        cs = slice(q * pg, (q + 1) * pg)
        y = jnp.dot(dbuf[:, cs], pw_ref[q], preferred_element_type=_F32) * ps_ref[:, cs]
        for s in range(gp):
            sl = slice(s * GROUP, (s + 1) * GROUP)
            out_ref[sl, cs] = x_ref[sl, cs] + mod_ref[s:s + 1, 2 * d + q * pg:2 * d + (q + 1) * pg] * y[sl, :]


def _pool_mixer(x, mod_g, st_tbl, g_mix, pool_w_bf, pool_scale, *, tm, n_prompt_tiles, tiles_per_seq):
    t, d = x.shape
    gp = tm // GROUP
    nq = len(POOL_WINDOWS)
    pg = d // nq
    kern = functools.partial(_pool_mixer_kernel, gp=gp, d=d, n_prompt_tiles=n_prompt_tiles,
                             tiles_per_seq=tiles_per_seq, tm=tm)
    return pl.pallas_call(
        kern,
        grid=(t // tm,),
        in_specs=[pl.BlockSpec((tm, d), lambda i: (i, 0)),
                  pl.BlockSpec((gp, 3 * d), lambda i: (i, 0)),
                  pl.BlockSpec((gp, POOL_HDR, d), lambda i: (jnp.maximum(i - n_prompt_tiles, 0), 0, 0)),
                  pl.BlockSpec((1, d), lambda i: (0, 0)),
                  pl.BlockSpec((nq, pg, pg), lambda i: (0, 0, 0)),
                  pl.BlockSpec((1, d), lambda i: (0, 0))],
        out_specs=[pl.BlockSpec((tm, d), lambda i: (i, 0)),
                   pl.BlockSpec((gp, POOL_HDR - 1, d), lambda i: (i, 0, 0))],
        out_shape=[jax.ShapeDtypeStruct((t, d), _F32),
                   jax.ShapeDtypeStruct((t // GROUP, POOL_HDR - 1, d), _F32)],
        scratch_shapes=[pltpu.VMEM((tm, d), _BF16),
                        pltpu.VMEM((POOL_HDR, d), _F32)],
        compiler_params=_cparams(1),
        name="pool_mixer",
    )(x, mod_g, st_tbl, g_mix.reshape(1, d), pool_w_bf, pool_scale.reshape(1, d))


def _router_kernel(x_ref, mod_ref, g_ref, wr_ref, br_ref, tri_ref, ltri_ref, swg_ref, swu_ref, swd_ref,
                   hs_ref, sh_ref, eidx_ref, wts_ref, pos_ref, cnt_ref, hb_scr, hf_scr,
                   *, gp, d, n_exp, tm, tmc):
    per_group = n_exp // N_EXPERT_GROUPS
    ns = d // SLAB_COLS

    for s in range(gp):
        sl = slice(s * GROUP, (s + 1) * GROUP)
        h = _norm_mod_group(x_ref[sl, :], mod_ref, g_ref, s, d)
        hf_scr[sl, :] = h
        hb_scr[sl, :] = h.astype(_BF16)
        for j in range(ns):
            _store_slab_cols(hs_ref, s * GROUP, j, h[:, SLAB_COLS * j:SLAB_COLS * (j + 1)], GROUP, ns)

    logits = lax.dot_general(wr_ref[...], hf_scr[...], (((1,), (1,)), ((), ())),
                             precision=lax.Precision.HIGHEST, preferred_element_type=_F32)
    sub = lax.broadcasted_iota(jnp.int32, (per_group, tm), 0)
    scores, biased, eids = [], [], []
    for g in range(N_EXPERT_GROUPS):
        rs = slice(g * per_group, (g + 1) * per_group)
        sc = jax.nn.sigmoid(logits[rs, :])
        scores.append(sc)
        biased.append(sc + br_ref[rs, :])
        eids.append(sub + g * per_group)

    gscore = []
    for g in range(N_EXPERT_GROUPS):
        m1 = jnp.max(biased[g], axis=0, keepdims=True)
        i1 = jnp.min(jnp.where(biased[g] == m1, sub, per_group), axis=0, keepdims=True)
        m2 = jnp.max(jnp.where(sub == i1, NEG_INF, biased[g]), axis=0, keepdims=True)
        gscore.append(m1 + m2)
    vals = []
    for g in range(N_EXPERT_GROUPS):
        beaten = jnp.zeros((1, tm), jnp.int32)
        for g2 in range(N_EXPERT_GROUPS):
            if g2 == g:
                continue
            beat = (gscore[g2] >= gscore[g]) if g2 < g else (gscore[g2] > gscore[g])
            beaten = beaten + beat.astype(jnp.int32)
        vals.append(jnp.where(beaten < TOPK_GROUPS, biased[g], NEG_INF))

    sel = [jnp.zeros((per_group, tm), _F32) for _ in range(N_EXPERT_GROUPS)]
    idxs, picked = [], []
    for _ in range(TOP_K):
        mx = vals[0]
        for g in range(1, N_EXPERT_GROUPS):
            mx = jnp.maximum(mx, vals[g])
        m = jnp.max(mx, axis=0, keepdims=True)
        ci = jnp.where(vals[0] == m, eids[0], n_exp)
        for g in range(1, N_EXPERT_GROUPS):
            ci = jnp.minimum(ci, jnp.where(vals[g] == m, eids[g], n_exp))
        idx = jnp.min(ci, axis=0, keepdims=True)
        sc_k = jnp.zeros((1, tm), _F32)
        for g in range(N_EXPERT_GROUPS):
            oh = eids[g] == idx
            sc_k = sc_k + jnp.sum(jnp.where(oh, scores[g], 0.0), axis=0, keepdims=True)
            vals[g] = jnp.where(oh, NEG_INF, vals[g])
            sel[g] = jnp.where(oh, 1.0, sel[g])
        idxs.append(idx)
        picked.append(sc_k)
    den = picked[0]
    for k in range(1, TOP_K):
        den = den + picked[k]

    for k in range(TOP_K):
        eidx_ref[k:k + 1, :] = idxs[k]
        wts_ref[k:k + 1, :] = picked[k] / den * ROUTED_SCALE

    sel_all = jnp.concatenate(sel, axis=0)
    places = []
    for hh in range(tm // tmc):
        s_h = sel_all[:, hh * tmc:(hh + 1) * tmc]
        n_h = jnp.sum(s_h, axis=1, keepdims=True).astype(jnp.int32)
        n_al = (((n_h + (RUN_ALIGN - 1)) // RUN_ALIGN) * RUN_ALIGN).astype(_F32)
        start = jnp.dot(ltri_ref[...], jnp.broadcast_to(n_al, (n_exp, 128)),
                        precision=lax.Precision.HIGHEST, preferred_element_type=_F32)[:, 0:1]
        places.append(jnp.dot(s_h.astype(_BF16), tri_ref[...], preferred_element_type=_F32) + start)
        cnt_ref[hh] = jnp.broadcast_to(n_h, (n_exp, 128))
    place = jnp.concatenate(places, axis=1)
    for k in range(TOP_K):
        pk = jnp.zeros((1, tm), _F32)
        for g in range(N_EXPERT_GROUPS):
            rs = slice(g * per_group, (g + 1) * per_group)
            pk = pk + jnp.sum(jnp.where(eids[g] == idxs[k], place[rs, :], 0.0), axis=0, keepdims=True)
        pos_ref[k:k + 1, :] = pk.astype(jnp.int32)

    hb = hb_scr[...]
    sg = jnp.dot(hb, swg_ref[...], preferred_element_type=_F32)
    su = jnp.dot(hb, swu_ref[...], preferred_element_type=_F32)
    sh_ref[...] = jnp.dot((_silu(sg) * su).astype(_BF16), swd_ref[...],
                          preferred_element_type=_F32).astype(_BF16)


def _router(x, mod_g, g_ffn, wr_t, br, swg_bf, swu_bf, swd_bf, *, tm, tmc):
    t, d = x.shape
    gp = tm // GROUP
    n_exp = wr_t.shape[0]
    f = swg_bf.shape[1]
    ns = d // SLAB_COLS
    nsub = tm // tmc
    kern = functools.partial(_router_kernel, gp=gp, d=d, n_exp=n_exp, tm=tm, tmc=tmc)
    const = lambda i: (0, 0)
    lower = lambda n: jnp.asarray(jnp.arange(n)[:, None] > jnp.arange(n)[None, :])
    tri = lower(tmc).T.astype(_BF16)
    ltri = lower(n_exp).astype(_F32)
    return pl.pallas_call(
        kern,
        grid=(t // tm,),
        in_specs=[pl.BlockSpec((tm, d), lambda i: (i, 0)),
                  pl.BlockSpec((gp, 3 * d), lambda i: (i, 1)),
                  pl.BlockSpec((1, d), const),
                  pl.BlockSpec((n_exp, d), const),
                  pl.BlockSpec((n_exp, 1), const),
                  pl.BlockSpec((tmc, tmc), const),
                  pl.BlockSpec((n_exp, n_exp), const),
                  pl.BlockSpec((d, f), const),
                  pl.BlockSpec((d, f), const),
                  pl.BlockSpec((f, d), const)],
        out_specs=[pl.BlockSpec((tm * ns, 128), lambda i: (i, 0)),
                   pl.BlockSpec((tm, d), lambda i: (i, 0)),
                   pl.BlockSpec((TOP_K, tm), lambda i: (0, i)),
                   pl.BlockSpec((TOP_K, tm), lambda i: (0, i)),
                   pl.BlockSpec((TOP_K, tm), lambda i: (0, i)),
                   pl.BlockSpec((nsub, n_exp, 128), lambda i: (i, 0, 0))],
        out_shape=[jax.ShapeDtypeStruct((t * ns, 128), _U32),
                   jax.ShapeDtypeStruct((t, d), _BF16),
                   jax.ShapeDtypeStruct((TOP_K, t), jnp.int32),
                   jax.ShapeDtypeStruct((TOP_K, t), _F32),
                   jax.ShapeDtypeStruct((TOP_K, t), jnp.int32),
                   jax.ShapeDtypeStruct((t // tmc, n_exp, 128), jnp.int32)],
        scratch_shapes=[pltpu.VMEM((tm, d), _BF16),
                        pltpu.VMEM((tm, d), _F32)],
        compiler_params=_cparams(1),
        name="router_shared",
    )(x, mod_g, g_ffn.reshape(1, d), wr_t, br.reshape(n_exp, 1), tri, ltri, swg_bf, swu_bf, swd_bf)


def _ffn_kernel(be_ref, nu_ref, idx_hbm, h_hbm, wg_ref, wu_ref, wd_ref, y_ref,
                idx_s, xflat, xmat, hbuf, wgb, wub, wdb, sem_i, sem_g, *, bm, ns):
    b = pl.program_id(0)
    nu = nu_ref[0]
    slot = lax.rem(b, 2)
    oslot = 1 - slot
    rows = bm * ns
    f = wgb.shape[1]

    def idx_copy(blk, sl):
        return pltpu.make_async_copy(idx_hbm.at[blk], idx_s.at[sl], sem_i.at[sl])

    def row_gather(sl, r):
        src = pl.multiple_of(idx_s[sl, r] * ns, ns)
        dst = pl.multiple_of((sl * bm + r) * ns, ns)
        return pltpu.make_async_copy(h_hbm.at[pl.ds(src, ns)], xflat.at[pl.ds(dst, ns)], sem_g.at[sl])

    def block_gather(sl):
        return pltpu.make_async_copy(h_hbm.at[pl.ds(0, rows)],
                                     xflat.at[pl.ds(pl.multiple_of(sl * rows, rows), rows)], sem_g.at[sl])

    @pl.when(b >= nu)
    def _():
        y_ref[...] = jnp.zeros_like(y_ref)

    @pl.when(b < nu)
    def _():
        @pl.when(b == 0)
        def _():
            first = idx_copy(0, 0)
            first.start()
            first.wait()
            for r in range(bm):
                row_gather(0, r).start()
            idx_copy(1, 1).start()

        new_expert = (b == 0) | (be_ref[b] != be_ref[jnp.maximum(b - 1, 0)])

        @pl.when(new_expert)
        def _():
            wgb[...] = wg_ref[...].astype(_BF16)
            wub[...] = wu_ref[...].astype(_BF16)
            wdb[...] = wd_ref[...].astype(_BF16)

        block_gather(slot).wait()
        idx_copy(b + 1, oslot).wait()

        per_piece = bm // ns
        for j in range(ns):
            lo, hi = _load_slab_cols(xflat, slot * bm, j, bm, ns)
            xmat[:, SLAB_COLS * j:SLAB_COLS * j + 128] = lo.astype(_BF16)
            xmat[:, SLAB_COLS * j + 128:SLAB_COLS * (j + 1)] = hi.astype(_BF16)
            for r in range(j * per_piece, (j + 1) * per_piece):
                row_gather(oslot, r).start()
        idx_copy(b + 2, slot).start()

        x = xmat[...]
        for c in range(f // SLAB_COLS):
            cs = slice(SLAB_COLS * c, SLAB_COLS * (c + 1))
            hg = jnp.dot(x, wgb[:, cs], preferred_element_type=_F32)
            hu = jnp.dot(x, wub[:, cs], preferred_element_type=_F32)
            hbuf[:, cs] = (_silu(hg) * hu).astype(_BF16)
        hb = hbuf[...]
        for j in range(ns):
            o = jnp.dot(hb, wdb[:, SLAB_COLS * j:SLAB_COLS * (j + 1)], preferred_element_type=_F32)
            _store_slab_cols(y_ref, 0, j, o, bm, ns)

        @pl.when(b == nu - 1)
        def _():
            block_gather(oslot).wait()
            idx_copy(b + 2, slot).wait()


def _routed_ffn(blk_e, n_used, idx, h_slabs, wg, wu, wd, *, layer, bm):
    nb = idx.shape[0] - 1
    _, _, d, f = wg.shape
    ns = d // SLAB_COLS
    assert f % SLAB_COLS == 0 and bm % ns == 0
    kern = functools.partial(_ffn_kernel, bm=bm, ns=ns)
    grid_spec = pltpu.PrefetchScalarGridSpec(
        num_scalar_prefetch=2,
        grid=(nb,),
        in_specs=[pl.BlockSpec(memory_space=pl.ANY),
                  pl.BlockSpec(memory_space=pl.ANY),
                  pl.BlockSpec((None, None, d, f), lambda b, be, nu: (layer, be[b], 0, 0)),
                  pl.BlockSpec((None, None, d, f), lambda b, be, nu: (layer, be[b], 0, 0)),
                  pl.BlockSpec((None, None, f, d), lambda b, be, nu: (layer, be[b], 0, 0))],
        out_specs=pl.BlockSpec((bm * ns, 128), lambda b, be, nu: (b, 0)),
        scratch_shapes=[pltpu.SMEM((2, bm), jnp.int32),
                        pltpu.VMEM((2 * bm * ns, 128), _U32),
                        pltpu.VMEM((bm, d), _BF16),
                        pltpu.VMEM((bm, f), _BF16),
                        pltpu.VMEM((d, f), _BF16),
                        pltpu.VMEM((d, f), _BF16),
                        pltpu.VMEM((f, d), _BF16),
                        pltpu.SemaphoreType.DMA((2,)),
                        pltpu.SemaphoreType.DMA((2,))])
    return pl.pallas_call(
        kern,
        grid_spec=grid_spec,
        out_shape=jax.ShapeDtypeStruct((nb * bm * ns, 128), _U32),
        compiler_params=_cparams(1),
        name="routed_ffn",
    )(blk_e, n_used, idx, h_slabs, wg, wu, wd)


def _combine_kernel(*refs, d, gpt, n_prompt_tiles):
    runs_hbm, pos_hbm, w_hbm, y_hbm, x_ref, sh_ref, mod_ref, gf_ref = refs[:8]
    n_out = 1 if n_prompt_tiles is None else 2
    out_refs = refs[8:8 + n_out]
    runs_s, pos_s, w_s, stage, lo_scr, hi_scr, sem_t, sem_y = refs[8 + n_out:]
    i = pl.program_id(0)
    n_tiles = pl.num_programs(0)
    ns = d // SLAB_COLS
    tmc = x_ref.shape[0]
    n_exp = runs_s.shape[1] // 2
    cap = stage.shape[0] // (2 * ns)

    def tables(tile, sl):
        return [pltpu.make_async_copy(src.at[tile], dst.at[sl], sem_t.at[sl])
                for src, dst in ((runs_hbm, runs_s), (pos_hbm, pos_s), (w_hbm, w_s))]

    def stage_runs(tsl, ysl, start):
        off = ysl * cap
        for e in range(n_exp):
            src = runs_s[tsl, e]
            pieces = (runs_s[tsl, n_exp + e] + (RUN_ALIGN - 1)) // RUN_ALIGN
            size = tmc // RUN_ALIGN
            while size >= 1:
                n_rows = size * RUN_ALIGN * ns
                take = (pieces & size) != 0

                @pl.when(take)
                def _(src=src, off=off, n_rows=n_rows):
                    cp = pltpu.make_async_copy(
                        y_hbm.at[pl.ds(pl.multiple_of(src * ns, ns), n_rows)],
                        stage.at[pl.ds(pl.multiple_of(off * ns, RUN_ALIGN * ns), n_rows)], sem_y.at[ysl])
                    if start:
                        cp.start()
                    else:
                        cp.wait()
                step = jnp.where(take, size * RUN_ALIGN, 0)
                src = src + step
                off = off + step
                size //= 2

    tsl = lax.rem(i, 3)
    ysl = lax.rem(i, 2)

    @pl.when(i == 0)
    def _():
        for cp in tables(0, 0):
            cp.start()
        for cp in tables(0, 0):
            cp.wait()
        stage_runs(0, 0, True)

        @pl.when(n_tiles > 1)
        def _():
            for cp in tables(1, 1):
                cp.start()

    @pl.when(i + 1 < n_tiles)
    def _():
        nsl = lax.rem(i + 1, 3)
        for cp in tables(i + 1, nsl):
            cp.wait()
        stage_runs(nsl, 1 - ysl, True)

    @pl.when(i + 2 < n_tiles)
    def _():
        for cp in tables(i + 2, lax.rem(i + 2, 3)):
            cp.start()

    stage_runs(tsl, ysl, False)

    def token(tk, carry):
        acc_lo = jnp.zeros((ns, 128), _F32)
        acc_hi = jnp.zeros((ns, 128), _F32)
        for k in range(TOP_K):
            place = ysl * cap + pos_s[tsl, tk * TOP_K + k]
            lo, hi = _unpack_pair(stage[pl.ds(pl.multiple_of(place * ns, ns), ns), :])
            wk = w_s[tsl, tk * TOP_K + k]
            acc_lo = acc_lo + wk * lo
            acc_hi = acc_hi + wk * hi
        row = pl.multiple_of(tk * ns, ns)
        lo_scr[pl.ds(row, ns), :] = acc_lo
        hi_scr[pl.ds(row, ns), :] = acc_hi
        return carry

    lax.fori_loop(0, tmc, token, 0, unroll=4)
    cols = []
    for j in range(ns):
        cols += [lo_scr[pl.ds(j, tmc, stride=ns), :], hi_scr[pl.ds(j, tmc, stride=ns), :]]
    acc = sh_ref[...].astype(_F32) + jnp.concatenate(cols, axis=1)
    groups_per_mod = mod_ref.shape[0]
    g0 = lax.rem(i, groups_per_mod // gpt) * gpt
    rows = []
    for s in range(gpt):
        sl = slice(s * GROUP, (s + 1) * GROUP)
        gate = mod_ref[pl.ds(g0 + s, 1), 2 * d:3 * d]
        xo = x_ref[sl, :] + gate * acc[sl, :]
        if n_prompt_tiles is not None:
            ms = jnp.mean(xo * xo, axis=-1, keepdims=True)
            xo = (xo * lax.rsqrt(ms + EPS)) * gf_ref[...]
        rows.append(xo)
    if n_prompt_tiles is None:
        for s in range(gpt):
            out_refs[0][s * GROUP:(s + 1) * GROUP, :] = rows[s]
    else:
        for o_ref, active in ((out_refs[0], i < n_prompt_tiles), (out_refs[1], i >= n_prompt_tiles)):
            @pl.when(active)
            def _(o_ref=o_ref):
                for s in range(gpt):
                    o_ref[s * GROUP:(s + 1) * GROUP, :] = rows[s]


def _combine(x, shared, runs, pos, wts, mod_g, g_final, y, *, tm, tmc, split_rows=None):
    t, d = x.shape
    n_exp = runs.shape[1] // 2
    gpt = tmc // GROUP
    gp = tm // GROUP
    nt = t // tmc
    row_spec = pl.BlockSpec((tmc, d), lambda i: (i, 0))
    if split_rows is None:
        npt = None
        out_specs = row_spec
        out_shape = jax.ShapeDtypeStruct((t, d), _F32)
    else:
        npt = split_rows[0] // tmc
        out_specs = [pl.BlockSpec((tmc, d), lambda i: (jnp.minimum(i, npt - 1), 0)),
                     pl.BlockSpec((tmc, d), lambda i: (jnp.maximum(i - npt, 0), 0))]
        out_shape = [jax.ShapeDtypeStruct((rows, d), _F32) for rows in split_rows]
    kern = functools.partial(_combine_kernel, d=d, gpt=gpt, n_prompt_tiles=npt)
    ns = d // SLAB_COLS
    cap = (tmc * TOP_K + n_exp * RUN_ALIGN)
    hbm = pl.BlockSpec(memory_space=pl.ANY)
    return pl.pallas_call(
        kern,
        grid=(nt,),
        in_specs=[hbm, hbm, hbm, hbm,
                  row_spec,
                  row_spec,
                  pl.BlockSpec((gp, 3 * d), lambda i: (i // (tm // tmc), 1)),
                  pl.BlockSpec((1, d), lambda i: (0, 0))],
        out_specs=out_specs,
        out_shape=out_shape,
        scratch_shapes=[pltpu.SMEM((3, 2 * n_exp), jnp.int32),
                        pltpu.SMEM((3, tmc * TOP_K), jnp.int32),
                        pltpu.SMEM((3, tmc * TOP_K), _F32),
                        pltpu.VMEM((2 * cap * ns, 128), _U32),
                        pltpu.VMEM((tmc * ns, 128), _F32),
                        pltpu.VMEM((tmc * ns, 128), _F32),
                        pltpu.SemaphoreType.DMA((3,)),
                        pltpu.SemaphoreType.DMA((2,))],
        compiler_params=_cparams(1),
        name="combine" if split_rows is None else "combine_final",
    )(runs, pos, wts, y, x, shared, mod_g, g_final.reshape(1, d))


def _dispatch_tables(eidx, tile_counts, *, bm, nb, t):
    counts = jnp.sum(tile_counts, axis=0)
    n_exp = counts.shape[0]
    low_bits = (t * TOP_K).bit_length()
    low_mask = (1 << low_bits) - 1
    assert t * TOP_K < low_mask and (n_exp + 1) << low_bits < 2 ** 31
    tok_k = jnp.arange(t, dtype=jnp.int32)[None, :] * TOP_K + jnp.arange(TOP_K, dtype=jnp.int32)[:, None]
    real = (eidx << low_bits) + tok_k
    n_pad = (-counts) % bm
    e_col = jnp.arange(n_exp, dtype=jnp.int32)[:, None]
    pad = jnp.where(jnp.arange(bm, dtype=jnp.int32)[None, :] < n_pad[:, None],
                    (e_col << low_bits) + low_mask, (n_exp << low_bits) + low_mask)
    keys = jnp.sort(jnp.concatenate([real.reshape(-1), pad.reshape(-1)]))
    low = keys & low_mask
    is_pad = low == low_mask
    tok = low // TOP_K
    r = jnp.arange(nb * bm, dtype=jnp.int32)
    row_src = jnp.where(is_pad, r % t, tok)
    idx = jnp.concatenate([row_src.reshape(nb, bm), jnp.zeros((1, bm), jnp.int32)], axis=0)
    blk_e = jnp.minimum(keys.reshape(nb, bm)[:, 0] >> low_bits, n_exp - 1)
    padded = counts + n_pad
    n_used = (jnp.sum(padded) // bm).astype(jnp.int32).reshape(1)
    seg_start = jnp.cumsum(padded) - padded
    run_start = seg_start[None, :] + jnp.cumsum(tile_counts, axis=0) - tile_counts
    runs = jnp.concatenate([run_start, tile_counts], axis=1).astype(jnp.int32)
    return blk_e, n_used, idx, runs


def _moe(x, mod_g, g_ffn, g_final, router_w, router_b, wg, wu, wd, swg, swu, swd,
         *, layer, tm, bm, split_rows=None):
    t, d = x.shape
    n_exp = router_w.shape[1]
    tmc = min(COMBINE_ROWS, tm)
    h_slabs, shared, eidx, wts, pos, cnt = _router(
        x, mod_g, g_ffn, router_w.T, router_b,
        swg.astype(_BF16), swu.astype(_BF16), swd.astype(_BF16), tm=tm, tmc=tmc)
    nb = (t * TOP_K) // bm + n_exp
    blk_e, n_used, idx, runs = _dispatch_tables(eidx, cnt[:, :, 0], bm=bm, nb=nb, t=t)
    y = _routed_ffn(blk_e, n_used, idx, h_slabs, wg, wu, wd, layer=layer, bm=bm)
    per_tile = lambda a: a.T.reshape(t // tmc, tmc * TOP_K)
    return _combine(x, shared, runs, per_tile(pos), per_tile(wts), mod_g, g_final, y,
                    tm=tm, tmc=tmc, split_rows=split_rows)


def _tile_rows(seq, dec_rows):
    for tm in (512, 256, 128, 64):
        if seq % tm == 0 and dec_rows % tm == 0:
            return tm
    raise ValueError("sequence lengths must be multiples of 64")


def kernel(x_prompt, x_sample, state_conv, state_pool, c_prompt, c_sample, w_mod, b_mod, g_mix, g_ffn, g_final, conv_w_in, conv_w, conv_w_out, pool_w, pool_scale, router_w, router_b, exp_w_gate, exp_w_up, exp_w_down, sh_w_gate, sh_w_up, sh_w_down):
    bp, seq, d = x_prompt.shape
    bs, dseq, _ = x_sample.shape
    depth = w_mod.shape[0]
    n_exp = router_w.shape[2]
    assert dseq == GROUP and seq % GROUP == 0 and depth == 2
    assert state_conv.shape[1] == CONV_W - 1 and state_pool.shape[1] == POOL_HDR - 1
    assert n_exp % N_EXPERT_GROUPS == 0 and d % (128 * len(POOL_WINDOWS)) == 0
    tp, ts = bp * seq, bs * dseq
    t = tp + ts
    tm = _tile_rows(seq, ts)
    bm = min(256, t * TOP_K // n_exp)
    assert (t * TOP_K) % bm == 0
    n_prompt_tiles, tiles_per_seq = tp // tm, seq // tm
    ng_p, ng = tp // GROUP, t // GROUP

    nb_pad = -(-(bp + bs) // 8) * 8
    c_all = jnp.concatenate([c_prompt, c_sample, jnp.zeros((nb_pad - bp - bs, d), _F32)], axis=0)
    mods = _mod_rows(c_all, w_mod, b_mod)
    gps = seq // GROUP
    mod_p = jnp.broadcast_to(mods[:, :bp, None, :], (depth, bp, gps, 6 * d)).reshape(depth, bp * gps, 6 * d)
    mod_g = jnp.concatenate([mod_p, mods[:, bp:bp + bs, :]], axis=1)

    conv_tbl = jnp.pad(state_conv, ((0, 0), (CONV_HDR - (CONV_W - 1), 0), (0, 0)))
    pool_tbl = jnp.pad(state_pool, ((0, 0), (1, 0), (0, 0)))

    statics = dict(tm=tm, n_prompt_tiles=n_prompt_tiles, tiles_per_seq=tiles_per_seq)
    x, conv_tail = _conv_mixer(x_prompt.reshape(tp, d), x_sample.reshape(ts, d), mod_g[0], conv_tbl, g_mix[0],
                               conv_w_in.astype(_BF16), conv_w, conv_w_out.astype(_BF16), **statics)
    x = _moe(x, mod_g[0], g_ffn[0], g_final, router_w[0], router_b[0], exp_w_gate, exp_w_up, exp_w_down,
             sh_w_gate[0], sh_w_up[0], sh_w_down[0], layer=0, tm=tm, bm=bm)
    x, pool_tail = _pool_mixer(x, mod_g[1], pool_tbl, g_mix[1], pool_w.astype(_BF16), pool_scale, **statics)
    y_p, y_s = _moe(x, mod_g[1], g_ffn[1], g_final, router_w[1], router_b[1], exp_w_gate, exp_w_up, exp_w_down,
                    sh_w_gate[1], sh_w_up[1], sh_w_down[1], layer=1, tm=tm, bm=bm, split_rows=(tp, ts))

    return (y_p.reshape(bp, seq, d),
            y_s.reshape(bs, dseq, d),
            conv_tail[gps - 1:ng_p:gps],
            conv_tail[ng_p:],
            pool_tail[gps - 1:ng_p:gps],
            pool_tail[ng_p:])
```

```python
import functools

import jax
import jax.numpy as jnp
from jax import lax
from jax.experimental import pallas as pl
from jax.experimental.pallas import tpu as pltpu

GROUP = 64
CONV_W = 3
POOL_WINDOWS = (2, 4, 8, 16)
POOL_HDR = 16
CONV_HDR = 8
TOP_K = 8
N_EXPERT_GROUPS = 8
TOPK_GROUPS = 4
ROUTED_SCALE = 2.5
PAST_LEN = 4096
EPS = 1e-6
VMEM_LIMIT_V7X = 56 * 1024 * 1024
NEG_INF = float("-inf")

_F32 = jnp.float32
_BF16 = jnp.bfloat16


def _silu(v):
    return v * jax.nn.sigmoid(v)


_U32 = jnp.uint32
_HI_MASK = 0xFFFF0000
SLAB_COLS = 256


def _pack_pair(lo, hi):
    lo_b = lax.bitcast_convert_type(lo.astype(_BF16).astype(_F32), _U32)
    hi_b = lax.bitcast_convert_type(hi.astype(_BF16).astype(_F32), _U32)
    return hi_b | (lo_b >> 16)


def _unpack_pair(p):
    lo = lax.bitcast_convert_type(p << 16, _F32)
    hi = lax.bitcast_convert_type(p & _U32(_HI_MASK), _F32)
    return lo, hi


def _store_slab_cols(ref, row0, j, v, n_rows, ns):
    ref[pl.ds(row0 * ns + j, n_rows, stride=ns), :] = _pack_pair(v[:, :128], v[:, 128:])


def _load_slab_cols(ref, row0, j, n_rows, ns):
    return _unpack_pair(ref[pl.ds(row0 * ns + j, n_rows, stride=ns), :])


def _cparams(n_axes):
    return pltpu.CompilerParams(dimension_semantics=("arbitrary",) * n_axes,
                                vmem_limit_bytes=VMEM_LIMIT_V7X)


def _mod_kernel(c_ref, w_ref, b_ref, o_ref):
    c = c_ref[...]
    ca = _silu(c).astype(_BF16)
    o_ref[...] = jnp.dot(ca, w_ref[...].astype(_BF16), preferred_element_type=_F32) + b_ref[...]


def _mod_rows(c_all, w_mod, b_mod):
    depth, d, n6 = w_mod.shape
    nb = c_all.shape[0]
    tn = min(1024, n6)
    return pl.pallas_call(
        _mod_kernel,
        grid=(depth, n6 // tn),
        in_specs=[pl.BlockSpec((nb, d), lambda l, j: (0, 0)),
                  pl.BlockSpec((None, d, tn), lambda l, j: (l, 0, j)),
                  pl.BlockSpec((None, 1, tn), lambda l, j: (l, 0, j))],
        out_specs=pl.BlockSpec((None, nb, tn), lambda l, j: (l, 0, j)),
        out_shape=jax.ShapeDtypeStruct((depth, nb, n6), _F32),
        compiler_params=_cparams(2),
        name="adaln_rows",
    )(c_all, w_mod, b_mod.reshape(depth, 1, n6))


def _norm_mod_group(xs, mod_ref, g_ref, s, d):
    ms = jnp.mean(xs * xs, axis=-1, keepdims=True)
    xn = (xs * lax.rsqrt(ms + EPS)) * g_ref[...]
    return xn * (1.0 + mod_ref[s:s + 1, d:2 * d]) + mod_ref[s:s + 1, 0:d]


def _conv_mixer_kernel(xp_ref, xs_ref, mod_ref, st_ref, g_ref, wb_ref, wc_ref, wv_ref, cw_ref, wo_ref,
                       out_ref, tail_ref, h_scr, acc_scr, ubuf, gbuf, carry,
                       *, gp, d, n_prompt_tiles, tiles_per_seq):
    i = pl.program_id(0)
    j = pl.program_id(1)
    nj = pl.num_programs(1)
    is_sample = i >= n_prompt_tiles

    def x_rows(s):
        sl = slice(s * GROUP, (s + 1) * GROUP)
        return jnp.where(is_sample, xs_ref[sl, :], xp_ref[sl, :])

    @pl.when(j == 0)
    def _():
        for s in range(gp):
            h_scr[s * GROUP:(s + 1) * GROUP, :] = _norm_mod_group(x_rows(s), mod_ref, g_ref, s, d).astype(_BF16)
        acc_scr[...] = jnp.zeros_like(acc_scr)

    @pl.when(i == 0)
    def _():
        carry[j] = jnp.zeros((CONV_HDR, carry.shape[2]), _F32)

    hb = h_scr[...]
    bq = jnp.dot(hb, wb_ref[...], preferred_element_type=_F32)
    cq = jnp.dot(hb, wc_ref[...], preferred_element_type=_F32)
    vq = jnp.dot(hb, wv_ref[...], preferred_element_type=_F32)
    u = cq * vq

    stride = GROUP + CONV_HDR
    for s in range(gp):
        if s == 0:
            prev = jnp.where(lax.rem(i, tiles_per_seq) == 0, 0.0, carry[j])
        else:
            prev = u[s * GROUP - CONV_HDR:s * GROUP, :]
        ubuf[s * stride:s * stride + CONV_HDR, :] = jnp.where(is_sample, st_ref[s], prev)
        ubuf[s * stride + CONV_HDR:(s + 1) * stride, :] = u[s * GROUP:(s + 1) * GROUP, :]
        tail_ref[s] = u[(s + 1) * GROUP - (CONV_W - 1):(s + 1) * GROUP, :]
    carry[j] = u[gp * GROUP - CONV_HDR:gp * GROUP, :]

    w0 = cw_ref[0:1, :]
    w1 = cw_ref[1:2, :]
    w2 = cw_ref[2:3, :]
    for s in range(gp):
        base = s * stride + CONV_HDR
        u0 = ubuf[base:base + GROUP, :]
        u1 = ubuf[base - 1:base - 1 + GROUP, :]
        u2 = ubuf[base - 2:base - 2 + GROUP, :]
        conv = u2 * w0 + u1 * w1 + u0 * w2
        gbuf[s * GROUP:(s + 1) * GROUP, :] = (bq[s * GROUP:(s + 1) * GROUP, :] * conv).astype(_BF16)
    acc_scr[...] += jnp.dot(gbuf[...], wo_ref[...], preferred_element_type=_F32)

    @pl.when(j == nj - 1)
    def _():
        for s in range(gp):
            sl = slice(s * GROUP, (s + 1) * GROUP)
            out_ref[sl, :] = x_rows(s) + mod_ref[s:s + 1, 2 * d:3 * d] * acc_scr[sl, :]


def _conv_mixer(xp, xs, mod_g, st_tbl, g_mix, w_in_bf, conv_w, w_out_bf, *, tm, n_prompt_tiles, tiles_per_seq):
    d = xp.shape[1]
    t = xp.shape[0] + xs.shape[0]
    npt = n_prompt_tiles
    gp = tm // GROUP
    tn = min(512, d)
    nj = d // tn
    kern = functools.partial(_conv_mixer_kernel, gp=gp, d=d, n_prompt_tiles=n_prompt_tiles,
                             tiles_per_seq=tiles_per_seq)
    return pl.pallas_call(
        kern,
        grid=(t // tm, nj),
        in_specs=[pl.BlockSpec((tm, d), lambda i, j: (jnp.minimum(i, npt - 1), 0)),
                  pl.BlockSpec((tm, d), lambda i, j: (jnp.maximum(i - npt, 0), 0)),
                  pl.BlockSpec((gp, 3 * d), lambda i, j: (i, 0)),
                  pl.BlockSpec((gp, CONV_HDR, tn), lambda i, j: (jnp.maximum(i - npt, 0), 0, j)),
                  pl.BlockSpec((1, d), lambda i, j: (0, 0)),
                  pl.BlockSpec((d, tn), lambda i, j: (0, j)),
                  pl.BlockSpec((d, tn), lambda i, j: (0, nj + j)),
                  pl.BlockSpec((d, tn), lambda i, j: (0, 2 * nj + j)),
                  pl.BlockSpec((CONV_W, tn), lambda i, j: (0, j)),
                  pl.BlockSpec((tn, d), lambda i, j: (j, 0))],
        out_specs=[pl.BlockSpec((tm, d), lambda i, j: (i, 0)),
                   pl.BlockSpec((gp, CONV_W - 1, tn), lambda i, j: (i, 0, j))],
        out_shape=[jax.ShapeDtypeStruct((t, d), _F32),
                   jax.ShapeDtypeStruct((t // GROUP, CONV_W - 1, d), _F32)],
        scratch_shapes=[pltpu.VMEM((tm, d), _BF16),
                        pltpu.VMEM((tm, d), _F32),
                        pltpu.VMEM((gp * (GROUP + CONV_HDR), tn), _F32),
                        pltpu.VMEM((tm, tn), _BF16),
                        pltpu.VMEM((nj, CONV_HDR, tn), _F32)],
        compiler_params=_cparams(2),
        name="conv_mixer",
    )(xp, xs, mod_g, st_tbl, g_mix.reshape(1, d), w_in_bf, w_in_bf, w_in_bf, conv_w, w_out_bf)


def _pool_mixer_kernel(x_ref, mod_ref, st_ref, g_ref, pw_ref, ps_ref, out_ref, tail_ref,
                       dbuf, carry, *, gp, d, n_prompt_tiles, tiles_per_seq, tm):
    i = pl.program_id(0)
    pg = d // len(POOL_WINDOWS)
    is_sample = i >= n_prompt_tiles
    row = lax.broadcasted_iota(jnp.int32, (GROUP, 1), 0)

    @pl.when(i == 0)
    def _():
        carry[...] = jnp.zeros_like(carry)

    prev = jnp.where(lax.rem(i, tiles_per_seq) == 0, 0.0, carry[...])
    for s in range(gp):
        h = _norm_mod_group(x_ref[s * GROUP:(s + 1) * GROUP, :], mod_ref, g_ref, s, d)
        hdr = jnp.where(is_sample, st_ref[s], prev)
        he = jnp.concatenate([hdr, h], axis=0)
        pos0 = jnp.where(is_sample, PAST_LEN, lax.rem(i, tiles_per_seq) * tm + s * GROUP)
        pos1 = pos0 + row + 1
        for q, w in enumerate(POOL_WINDOWS):
            cs = slice(q * pg, (q + 1) * pg)
            sw = he[:, cs]
            span = 1
            while span < w:
                sw = sw + pltpu.roll(sw, span, 0)
                span *= 2
            inv_cnt = 1.0 / jnp.minimum(pos1, w).astype(_F32)
            pooled = sw[POOL_HDR:, :] * inv_cnt
            dbuf[s * GROUP:(s + 1) * GROUP, cs] = (pooled - h[:, cs]).astype(_BF16)
        prev = h[GROUP - POOL_HDR:, :]
        tail_ref[s] = h[GROUP - (POOL_HDR - 1):, :]
    carry[...] = prev
    for q in range(len(POOL_WINDOWS)):
        cs = slice(q * pg, (q + 1) * pg)
        y = jnp.dot(dbuf[:, cs], pw_ref[q], preferred_element_type=_F32) * ps_ref[:, cs]
        for s in range(gp):
            sl = slice(s * GROUP, (s + 1) * GROUP)
            out_ref[sl, cs] = x_ref[sl, cs] + mod_ref[s:s + 1, 2 * d + q * pg:2 * d + (q + 1) * pg] * y[sl, :]


def _pool_mixer(x, mod_g, st_tbl, g_mix, pool_w_bf, pool_scale, *, tm, n_prompt_tiles, tiles_per_seq):
    t, d = x.shape
    gp = tm // GROUP
    nq = len(POOL_WINDOWS)
    pg = d // nq
    kern = functools.partial(_pool_mixer_kernel, gp=gp, d=d, n_prompt_tiles=n_prompt_tiles,
                             tiles_per_seq=tiles_per_seq, tm=tm)
    return pl.pallas_call(
        kern,
        grid=(t // tm,),
        in_specs=[pl.BlockSpec((tm, d), lambda i: (i, 0)),
                  pl.BlockSpec((gp, 3 * d), lambda i: (i, 0)),
                  pl.BlockSpec((gp, POOL_HDR, d), lambda i: (jnp.maximum(i - n_prompt_tiles, 0), 0, 0)),
                  pl.BlockSpec((1, d), lambda i: (0, 0)),
                  pl.BlockSpec((nq, pg, pg), lambda i: (0, 0, 0)),
                  pl.BlockSpec((1, d), lambda i: (0, 0))],
        out_specs=[pl.BlockSpec((tm, d), lambda i: (i, 0)),
                   pl.BlockSpec((gp, POOL_HDR - 1, d), lambda i: (i, 0, 0))],
        out_shape=[jax.ShapeDtypeStruct((t, d), _F32),
                   jax.ShapeDtypeStruct((t // GROUP, POOL_HDR - 1, d), _F32)],
        scratch_shapes=[pltpu.VMEM((tm, d), _BF16),
                        pltpu.VMEM((POOL_HDR, d), _F32)],
        compiler_params=_cparams(1),
        name="pool_mixer",
    )(x, mod_g, st_tbl, g_mix.reshape(1, d), pool_w_bf, pool_scale.reshape(1, d))


def _router_kernel(x_ref, mod_ref, g_ref, wr_ref, br_ref, swg_ref, swu_ref, swd_ref,
                   hs_ref, sh_ref, eidx_ref, wts_ref, cnt_ref, run_scr, hb_scr, hf_scr,
                   *, gp, d, n_exp, tm):
    i = pl.program_id(0)
    per_group = n_exp // N_EXPERT_GROUPS
    ns = d // SLAB_COLS

    @pl.when(i == 0)
    def _():
        run_scr[...] = jnp.zeros_like(run_scr)

    for s in range(gp):
        sl = slice(s * GROUP, (s + 1) * GROUP)
        h = _norm_mod_group(x_ref[sl, :], mod_ref, g_ref, s, d)
        hf_scr[sl, :] = h
        hb_scr[sl, :] = h.astype(_BF16)
        for j in range(ns):
            _store_slab_cols(hs_ref, s * GROUP, j, h[:, SLAB_COLS * j:SLAB_COLS * (j + 1)], GROUP, ns)

    logits = lax.dot_general(wr_ref[...], hf_scr[...], (((1,), (1,)), ((), ())),
                             precision=lax.Precision.HIGHEST, preferred_element_type=_F32)
    sub = lax.broadcasted_iota(jnp.int32, (per_group, tm), 0)
    scores, biased, eids = [], [], []
    for g in range(N_EXPERT_GROUPS):
        rs = slice(g * per_group, (g + 1) * per_group)
        sc = jax.nn.sigmoid(logits[rs, :])
        scores.append(sc)
        biased.append(sc + br_ref[rs, :])
        eids.append(sub + g * per_group)

    gscore = []
    for g in range(N_EXPERT_GROUPS):
        m1 = jnp.max(biased[g], axis=0, keepdims=True)
        i1 = jnp.min(jnp.where(biased[g] == m1, sub, per_group), axis=0, keepdims=True)
        m2 = jnp.max(jnp.where(sub == i1, NEG_INF, biased[g]), axis=0, keepdims=True)
        gscore.append(m1 + m2)
    vals = []
    for g in range(N_EXPERT_GROUPS):
        beaten = jnp.zeros((1, tm), jnp.int32)
        for g2 in range(N_EXPERT_GROUPS):
            if g2 == g:
                continue
            beat = (gscore[g2] >= gscore[g]) if g2 < g else (gscore[g2] > gscore[g])
            beaten = beaten + beat.astype(jnp.int32)
        vals.append(jnp.where(beaten < TOPK_GROUPS, biased[g], NEG_INF))

    sel = [jnp.zeros((per_group, tm), _F32) for _ in range(N_EXPERT_GROUPS)]
    idxs, picked = [], []
    for _ in range(TOP_K):
        mx = vals[0]
        for g in range(1, N_EXPERT_GROUPS):
            mx = jnp.maximum(mx, vals[g])
        m = jnp.max(mx, axis=0, keepdims=True)
        ci = jnp.where(vals[0] == m, eids[0], n_exp)
        for g in range(1, N_EXPERT_GROUPS):
            ci = jnp.minimum(ci, jnp.where(vals[g] == m, eids[g], n_exp))
        idx = jnp.min(ci, axis=0, keepdims=True)
        sc_k = jnp.zeros((1, tm), _F32)
        for g in range(N_EXPERT_GROUPS):
            oh = eids[g] == idx
            sc_k = sc_k + jnp.sum(jnp.where(oh, scores[g], 0.0), axis=0, keepdims=True)
            vals[g] = jnp.where(oh, NEG_INF, vals[g])
            sel[g] = jnp.where(oh, 1.0, sel[g])
        idxs.append(idx)
        picked.append(sc_k)
    den = picked[0]
    for k in range(1, TOP_K):
        den = den + picked[k]

    for k in range(TOP_K):
        eidx_ref[k:k + 1, :] = idxs[k]
        wts_ref[k:k + 1, :] = picked[k] / den * ROUTED_SCALE
    sel_all = jnp.concatenate(sel, axis=0)
    run_scr[...] = run_scr[...] + jnp.sum(sel_all, axis=1, keepdims=True)
    cnt_ref[...] = run_scr[...].astype(jnp.int32)

    hb = hb_scr[...]
    sg = jnp.dot(hb, swg_ref[...], preferred_element_type=_F32)
    su = jnp.dot(hb, swu_ref[...], preferred_element_type=_F32)
    sh_ref[...] = jnp.dot((_silu(sg) * su).astype(_BF16), swd_ref[...],
                          preferred_element_type=_F32).astype(_BF16)


def _router(x, mod_g, g_ffn, wr_t, br, swg_bf, swu_bf, swd_bf, *, tm):
    t, d = x.shape
    gp = tm // GROUP
    n_exp = wr_t.shape[0]
    f = swg_bf.shape[1]
    ns = d // SLAB_COLS
    kern = functools.partial(_router_kernel, gp=gp, d=d, n_exp=n_exp, tm=tm)
    const = lambda i: (0, 0)
    return pl.pallas_call(
        kern,
        grid=(t // tm,),
        in_specs=[pl.BlockSpec((tm, d), lambda i: (i, 0)),
                  pl.BlockSpec((gp, 3 * d), lambda i: (i, 1)),
                  pl.BlockSpec((1, d), const),
                  pl.BlockSpec((n_exp, d), const),
                  pl.BlockSpec((n_exp, 1), const),
                  pl.BlockSpec((d, f), const),
                  pl.BlockSpec((d, f), const),
                  pl.BlockSpec((f, d), const)],
        out_specs=[pl.BlockSpec((tm * ns, 128), lambda i: (i, 0)),
                   pl.BlockSpec((tm, d), lambda i: (i, 0)),
                   pl.BlockSpec((TOP_K, tm), lambda i: (0, i)),
                   pl.BlockSpec((TOP_K, tm), lambda i: (0, i)),
                   pl.BlockSpec((n_exp, 128), const)],
        out_shape=[jax.ShapeDtypeStruct((t * ns, 128), _U32),
                   jax.ShapeDtypeStruct((t, d), _BF16),
                   jax.ShapeDtypeStruct((TOP_K, t), jnp.int32),
                   jax.ShapeDtypeStruct((TOP_K, t), _F32),
                   jax.ShapeDtypeStruct((n_exp, 128), jnp.int32)],
        scratch_shapes=[pltpu.VMEM((n_exp, 128), _F32),
                        pltpu.VMEM((tm, d), _BF16),
                        pltpu.VMEM((tm, d), _F32)],
        compiler_params=_cparams(1),
        name="router_shared",
    )(x, mod_g, g_ffn.reshape(1, d), wr_t, br.reshape(n_exp, 1), swg_bf, swu_bf, swd_bf)


def _ffn_kernel(be_ref, ne_ref, nu_ref, idx_hbm, h_hbm, wg_hbm, wu_hbm, wd_hbm, y_hbm,
                idx_s, xflat, yflat, xmat, hbuf, wg32, wu32, wd32, wgb, wub, wdb,
                sem_i, sem_g, sem_s, sem_w, *, bm, ns, layer):
    b = pl.program_id(0)
    nu = nu_ref[0]
    nb = pl.num_programs(0)
    slot = lax.rem(b, 2)
    oslot = 1 - slot
    rows = bm * ns
    f = wgb.shape[1]

    def idx_copy(blk, sl):
        return pltpu.make_async_copy(idx_hbm.at[blk], idx_s.at[sl], sem_i.at[sl])

    def weight_copies(e):
        return [pltpu.make_async_copy(src.at[layer, e], dst, sem_w.at[0])
                for src, dst in ((wg_hbm, wg32), (wu_hbm, wu32), (wd_hbm, wd32))]

    def row_gather(sl, isl, r):
        src = pl.multiple_of(idx_s[isl, r] * ns, ns)
        dst = pl.multiple_of((sl * bm + r) * ns, ns)
        return pltpu.make_async_copy(h_hbm.at[pl.ds(src, ns)], xflat.at[pl.ds(dst, ns)], sem_g.at[sl])

    def row_scatter(sl, isl, r):
        src = pl.multiple_of((sl * bm + r) * ns, ns)
        dst = pl.multiple_of(idx_s[isl, bm + r] * ns, ns)
        return pltpu.make_async_copy(yflat.at[pl.ds(src, ns)], y_hbm.at[pl.ds(dst, ns)], sem_s.at[sl])

    def block_gather(sl):
        return pltpu.make_async_copy(h_hbm.at[pl.ds(0, rows)],
                                     xflat.at[pl.ds(pl.multiple_of(sl * rows, rows), rows)], sem_g.at[sl])

    def block_scatter(sl, dst0=0):
        return pltpu.make_async_copy(yflat.at[pl.ds(pl.multiple_of(sl * rows, rows), rows)],
                                     y_hbm.at[pl.ds(dst0, rows)], sem_s.at[sl])

    @pl.when(b < nu)
    def _():
        @pl.when(b == 0)
        def _():
            for cp in weight_copies(be_ref[0]):
                cp.start()
            yflat[...] = jnp.zeros_like(yflat)
            block_scatter(0, y_hbm.shape[0] - 2 * rows).start()
            before = idx_copy(nb + 1, 3)
            first = idx_copy(0, 0)
            before.start()
            first.start()
            before.wait()
            first.wait()
            for r in range(bm):
                row_gather(0, 0, r).start()
            idx_copy(1, 1).start()

        new_expert = (b == 0) | (be_ref[b] != be_ref[jnp.maximum(b - 1, 0)])

        @pl.when(new_expert)
        def _():
            for cp in weight_copies(be_ref[b]):
                cp.wait()
            wgb[...] = wg32[...].astype(_BF16)
            wub[...] = wu32[...].astype(_BF16)
            wdb[...] = wd32[...].astype(_BF16)
            for cp in weight_copies(ne_ref[b]):
                cp.start()

        i_next = lax.rem(b + 1, 4)
        i_prev = lax.rem(b + 3, 4)
        block_gather(slot).wait()
        idx_copy(b + 1, i_next).wait()

        per_piece = bm // ns
        for j in range(ns):
            lo, hi = _load_slab_cols(xflat, slot * bm, j, bm, ns)
            xmat[:, SLAB_COLS * j:SLAB_COLS * j + 128] = lo.astype(_BF16)
            xmat[:, SLAB_COLS * j + 128:SLAB_COLS * (j + 1)] = hi.astype(_BF16)
            for r in range(j * per_piece, (j + 1) * per_piece):
                row_gather(oslot, i_next, r).start()

        x = xmat[...]
        n_fc = f // SLAB_COLS
        per_piece = bm // n_fc
        for c in range(n_fc):
            cs = slice(SLAB_COLS * c, SLAB_COLS * (c + 1))
            hg = jnp.dot(x, wgb[:, cs], preferred_element_type=_F32)
            hu = jnp.dot(x, wub[:, cs], preferred_element_type=_F32)
            hbuf[:, cs] = (_silu(hg) * hu).astype(_BF16)
            for r in range(c * per_piece, (c + 1) * per_piece):
                row_scatter(oslot, i_prev, r).start()

        block_scatter(slot).wait()
        hb = hbuf[...]
        for j in range(ns):
            o = jnp.dot(hb, wdb[:, SLAB_COLS * j:SLAB_COLS * (j + 1)], preferred_element_type=_F32)
            _store_slab_cols(yflat, slot * bm, j, o, bm, ns)
        idx_copy(b + 2, lax.rem(b + 2, 4)).start()

        @pl.when(b == nu - 1)
        def _():
            for r in range(bm):
                row_scatter(slot, lax.rem(b, 4), r).start()
            block_gather(oslot).wait()
            idx_copy(b + 2, lax.rem(b + 2, 4)).wait()
            block_scatter(oslot).wait()
            block_scatter(slot).wait()
            for cp in weight_copies(0):
                cp.wait()


def _routed_ffn(blk_e, next_e, n_used, idx, h_slabs, wg, wu, wd, *, layer, bm, n_out_rows):
    nb = idx.shape[0] - 2
    _, _, d, f = wg.shape
    ns = d // SLAB_COLS
    assert f % SLAB_COLS == 0 and bm % ns == 0
    kern = functools.partial(_ffn_kernel, bm=bm, ns=ns, layer=layer)
    hbm = pl.BlockSpec(memory_space=pl.ANY)
    grid_spec = pltpu.PrefetchScalarGridSpec(
        num_scalar_prefetch=3,
        grid=(nb,),
        in_specs=[hbm, hbm, hbm, hbm, hbm],
        out_specs=hbm,
        scratch_shapes=[pltpu.SMEM((4, 2 * bm), jnp.int32),
                        pltpu.VMEM((2 * bm * ns, 128), _U32),
                        pltpu.VMEM((2 * bm * ns, 128), _U32),
                        pltpu.VMEM((bm, d), _BF16),
                        pltpu.VMEM((bm, f), _BF16),
                        pltpu.VMEM((d, f), _F32),
                        pltpu.VMEM((d, f), _F32),
                        pltpu.VMEM((f, d), _F32),
                        pltpu.VMEM((d, f), _BF16),
                        pltpu.VMEM((d, f), _BF16),
                        pltpu.VMEM((f, d), _BF16),
                        pltpu.SemaphoreType.DMA((4,)),
                        pltpu.SemaphoreType.DMA((2,)),
                        pltpu.SemaphoreType.DMA((2,)),
                        pltpu.SemaphoreType.DMA((1,))])
    return pl.pallas_call(
        kern,
        grid_spec=grid_spec,
        out_shape=jax.ShapeDtypeStruct((n_out_rows * ns, 128), _U32),
        compiler_params=_cparams(1),
        name="routed_ffn",
    )(blk_e, next_e, n_used, idx, h_slabs, wg, wu, wd)


def _combine_kernel(*refs, d, gpt, n_prompt_tiles):
    x_ref, sh_ref, w_ref, mod_ref, gf_ref = refs[:5]
    y_refs = refs[5:5 + TOP_K]
    n_out = 1 if n_prompt_tiles is None else 2
    out_refs = refs[5 + TOP_K:5 + TOP_K + n_out]
    lo_scr, hi_scr = refs[5 + TOP_K + n_out:]
    i = pl.program_id(0)
    ns = d // SLAB_COLS
    tmc = x_ref.shape[0]
    acc_lo = jnp.zeros(lo_scr.shape, _F32)
    acc_hi = jnp.zeros(hi_scr.shape, _F32)
    for k in range(TOP_K):
        lo, hi = _unpack_pair(y_refs[k][...])
        wk = w_ref[:, k:k + 1]
        acc_lo = acc_lo + wk * lo
        acc_hi = acc_hi + wk * hi
    lo_scr[...] = acc_lo
    hi_scr[...] = acc_hi
    cols = []
    for j in range(ns):
        cols += [lo_scr[pl.ds(j, tmc, stride=ns), :], hi_scr[pl.ds(j, tmc, stride=ns), :]]
    acc = sh_ref[...].astype(_F32) + jnp.concatenate(cols, axis=1)
    groups_per_mod = mod_ref.shape[0]
    g0 = lax.rem(i, groups_per_mod // gpt) * gpt
    rows = []
    for s in range(gpt):
        sl = slice(s * GROUP, (s + 1) * GROUP)
        gate = mod_ref[pl.ds(g0 + s, 1), 2 * d:3 * d]
        xo = x_ref[sl, :] + gate * acc[sl, :]
        if n_prompt_tiles is not None:
            ms = jnp.mean(xo * xo, axis=-1, keepdims=True)
            xo = (xo * lax.rsqrt(ms + EPS)) * gf_ref[...]
        rows.append(xo)
    if n_prompt_tiles is None:
        for s in range(gpt):
            out_refs[0][s * GROUP:(s + 1) * GROUP, :] = rows[s]
    else:
        for o_ref, active in ((out_refs[0], i < n_prompt_tiles), (out_refs[1], i >= n_prompt_tiles)):
            @pl.when(active)
            def _(o_ref=o_ref):
                for s in range(gpt):
                    o_ref[s * GROUP:(s + 1) * GROUP, :] = rows[s]


def _combine(x, shared, wts, mod_g, g_final, y, *, tm, split_rows=None):
    t, d = x.shape
    tmc = min(256, tm)
    gpt = tmc // GROUP
    gp = tm // GROUP
    nt = t // tmc
    row_spec = pl.BlockSpec((tmc, d), lambda i: (i, 0))
    if split_rows is None:
        npt = None
        out_specs = row_spec
        out_shape = jax.ShapeDtypeStruct((t, d), _F32)
    else:
        npt = split_rows[0] // tmc
        out_specs = [pl.BlockSpec((tmc, d), lambda i: (jnp.minimum(i, npt - 1), 0)),
                     pl.BlockSpec((tmc, d), lambda i: (jnp.maximum(i - npt, 0), 0))]
        out_shape = [jax.ShapeDtypeStruct((rows, d), _F32) for rows in split_rows]
    kern = functools.partial(_combine_kernel, d=d, gpt=gpt, n_prompt_tiles=npt)
    ns = d // SLAB_COLS
    y_specs = [pl.BlockSpec((tmc * ns, 128), functools.partial(lambda i, k: (k * nt + i, 0), k=k))
               for k in range(TOP_K)]
    return pl.pallas_call(
        kern,
        grid=(nt,),
        scratch_shapes=[pltpu.VMEM((tmc * ns, 128), _F32), pltpu.VMEM((tmc * ns, 128), _F32)],
        in_specs=[row_spec,
                  row_spec,
                  pl.BlockSpec((tmc * ns, TOP_K), lambda i: (i, 0)),
                  pl.BlockSpec((gp, 3 * d), lambda i: (i // (tm // tmc), 1)),
                  pl.BlockSpec((1, d), lambda i: (0, 0))] + y_specs,
        out_specs=out_specs,
        out_shape=out_shape,
        compiler_params=_cparams(1),
        name="combine" if split_rows is None else "combine_final",
    )(x, shared, wts, mod_g, g_final.reshape(1, d), *([y] * TOP_K))


def _dispatch_tables(eidx, counts, *, bm, nb, t):
    n_exp = counts.shape[0]
    low_bits = (t * TOP_K).bit_length()
    low_mask = (1 << low_bits) - 1
    assert t * TOP_K < low_mask and (n_exp + 1) << low_bits < 2 ** 31
    tok_k = jnp.arange(t, dtype=jnp.int32)[None, :] * TOP_K + jnp.arange(TOP_K, dtype=jnp.int32)[:, None]
    real = (eidx << low_bits) + tok_k
    n_pad = (-counts) % bm
    e_col = jnp.arange(n_exp, dtype=jnp.int32)[:, None]
    pad = jnp.where(jnp.arange(bm, dtype=jnp.int32)[None, :] < n_pad[:, None],
                    (e_col << low_bits) + low_mask, (n_exp << low_bits) + low_mask)
    keys = jnp.sort(jnp.concatenate([real.reshape(-1), pad.reshape(-1)]))
    low = keys & low_mask
    is_pad = low == low_mask
    tok = low // TOP_K
    r = jnp.arange(nb * bm, dtype=jnp.int32)
    row_src = jnp.where(is_pad, r % t, tok)
    row_dst = jnp.where(is_pad, TOP_K * t + ((r // bm) % 2) * bm + (r % bm),
                        (low % TOP_K) * t + tok)
    idx = jnp.concatenate([row_src.reshape(nb, bm), row_dst.reshape(nb, bm)], axis=1)
    before = jnp.concatenate([jnp.zeros((bm,), jnp.int32), TOP_K * t + bm + jnp.arange(bm, dtype=jnp.int32)])
    idx = jnp.concatenate([idx, jnp.zeros((1, 2 * bm), jnp.int32), before[None, :]], axis=0)
    blk_e = jnp.minimum(keys.reshape(nb, bm)[:, 0] >> low_bits, n_exp - 1)
    later = jnp.where(blk_e[None, :] > blk_e[:, None], blk_e[None, :], n_exp)
    next_e = jnp.min(later, axis=1)
    next_e = jnp.where(next_e == n_exp, blk_e, next_e).astype(jnp.int32)
    n_used = (jnp.sum(counts + n_pad) // bm).astype(jnp.int32).reshape(1)
    return blk_e, next_e, n_used, idx


def _moe(x, mod_g, g_ffn, g_final, router_w, router_b, wg, wu, wd, swg, swu, swd,
         *, layer, tm, bm, split_rows=None):
    t, d = x.shape
    n_exp = router_w.shape[1]
    h_slabs, shared, eidx, wts, cnt = _router(
        x, mod_g, g_ffn, router_w.T, router_b,
        swg.astype(_BF16), swu.astype(_BF16), swd.astype(_BF16), tm=tm)
    nb = (t * TOP_K) // bm + n_exp
    blk_e, next_e, n_used, idx = _dispatch_tables(eidx, cnt[:, 0], bm=bm, nb=nb, t=t)
    y = _routed_ffn(blk_e, next_e, n_used, idx, h_slabs, wg, wu, wd, layer=layer, bm=bm,
                    n_out_rows=TOP_K * t + 2 * bm)
    w_rows = jnp.repeat(wts.T, d // SLAB_COLS, axis=0)
    return _combine(x, shared, w_rows, mod_g, g_final, y, tm=tm, split_rows=split_rows)


def _tile_rows(seq, dec_rows):
    for tm in (512, 256, 128, 64):
        if seq % tm == 0 and dec_rows % tm == 0:
            return tm
    raise ValueError("sequence lengths must be multiples of 64")


def kernel(x_prompt, x_sample, state_conv, state_pool, c_prompt, c_sample, w_mod, b_mod, g_mix, g_ffn, g_final, conv_w_in, conv_w, conv_w_out, pool_w, pool_scale, router_w, router_b, exp_w_gate, exp_w_up, exp_w_down, sh_w_gate, sh_w_up, sh_w_down):
    bp, seq, d = x_prompt.shape
    bs, dseq, _ = x_sample.shape
    depth = w_mod.shape[0]
    n_exp = router_w.shape[2]
    assert dseq == GROUP and seq % GROUP == 0 and depth == 2
    assert state_conv.shape[1] == CONV_W - 1 and state_pool.shape[1] == POOL_HDR - 1
    assert n_exp % N_EXPERT_GROUPS == 0 and d % (128 * len(POOL_WINDOWS)) == 0
    tp, ts = bp * seq, bs * dseq
    t = tp + ts
    tm = _tile_rows(seq, ts)
    bm = min(256, t * TOP_K // n_exp)
    assert (t * TOP_K) % bm == 0
    n_prompt_tiles, tiles_per_seq = tp // tm, seq // tm
    ng_p, ng = tp // GROUP, t // GROUP

    nb_pad = -(-(bp + bs) // 8) * 8
    c_all = jnp.concatenate([c_prompt, c_sample, jnp.zeros((nb_pad - bp - bs, d), _F32)], axis=0)
    mods = _mod_rows(c_all, w_mod, b_mod)
    gps = seq // GROUP
    mod_p = jnp.broadcast_to(mods[:, :bp, None, :], (depth, bp, gps, 6 * d)).reshape(depth, bp * gps, 6 * d)
    mod_g = jnp.concatenate([mod_p, mods[:, bp:bp + bs, :]], axis=1)

    conv_tbl = jnp.pad(state_conv, ((0, 0), (CONV_HDR - (CONV_W - 1), 0), (0, 0)))
    pool_tbl = jnp.pad(state_pool, ((0, 0), (1, 0), (0, 0)))

    statics = dict(tm=tm, n_prompt_tiles=n_prompt_tiles, tiles_per_seq=tiles_per_seq)
    x, conv_tail = _conv_mixer(x_prompt.reshape(tp, d), x_sample.reshape(ts, d), mod_g[0], conv_tbl, g_mix[0],
                               conv_w_in.astype(_BF16), conv_w, conv_w_out.astype(_BF16), **statics)
    x = _moe(x, mod_g[0], g_ffn[0], g_final, router_w[0], router_b[0], exp_w_gate, exp_w_up, exp_w_down,
             sh_w_gate[0], sh_w_up[0], sh_w_down[0], layer=0, tm=tm, bm=bm)
    x, pool_tail = _pool_mixer(x, mod_g[1], pool_tbl, g_mix[1], pool_w.astype(_BF16), pool_scale, **statics)
    y_p, y_s = _moe(x, mod_g[1], g_ffn[1], g_final, router_w[1], router_b[1], exp_w_gate, exp_w_up, exp_w_down,
                    sh_w_gate[1], sh_w_up[1], sh_w_down[1], layer=1, tm=tm, bm=bm, split_rows=(tp, ts))

    return (y_p.reshape(bp, seq, d),
            y_s.reshape(bs, dseq, d),
            conv_tail[gps - 1:ng_p:gps],
            conv_tail[ng_p:],
            pool_tail[gps - 1:ng_p:gps],
            pool_tail[ng_p:])
```

```python
import functools

import jax
import jax.numpy as jnp
from jax import lax
from jax.experimental import pallas as pl
from jax.experimental.pallas import tpu as pltpu

GROUP = 64
CONV_W = 3
POOL_WINDOWS = (2, 4, 8, 16)
POOL_HDR = 16
CONV_HDR = 8
TOP_K = 8
N_EXPERT_GROUPS = 8
TOPK_GROUPS = 4
ROUTED_SCALE = 2.5
PAST_LEN = 4096
EPS = 1e-6
VMEM_LIMIT_V7X = 56 * 1024 * 1024
NEG_INF = float("-inf")

_F32 = jnp.float32
_BF16 = jnp.bfloat16


def _silu(v):
    return v * jax.nn.sigmoid(v)


_U32 = jnp.uint32
_HI_MASK = 0xFFFF0000
SLAB_COLS = 256
IDX_SLOTS = 8


def _pack_pair(lo, hi):
    lo_b = lax.bitcast_convert_type(lo.astype(_BF16).astype(_F32), _U32)
    hi_b = lax.bitcast_convert_type(hi.astype(_BF16).astype(_F32), _U32)
    return hi_b | (lo_b >> 16)


def _unpack_pair(p):
    lo = lax.bitcast_convert_type(p << 16, _F32)
    hi = lax.bitcast_convert_type(p & _U32(_HI_MASK), _F32)
    return lo, hi


def _store_slab_cols(ref, row0, j, v, n_rows, ns):
    ref[pl.ds(row0 * ns + j, n_rows, stride=ns), :] = _pack_pair(v[:, :128], v[:, 128:])


def _load_slab_cols(ref, row0, j, n_rows, ns):
    return _unpack_pair(ref[pl.ds(row0 * ns + j, n_rows, stride=ns), :])


def _cparams(n_axes):
    return pltpu.CompilerParams(dimension_semantics=("arbitrary",) * n_axes,
                                vmem_limit_bytes=VMEM_LIMIT_V7X)


def _mod_kernel(c_ref, w_ref, b_ref, o_ref):
    c = c_ref[...]
    ca = _silu(c).astype(_BF16)
    o_ref[...] = jnp.dot(ca, w_ref[...].astype(_BF16), preferred_element_type=_F32) + b_ref[...]


def _mod_rows(c_all, w_mod, b_mod):
    depth, d, n6 = w_mod.shape
    nb = c_all.shape[0]
    tn = min(1024, n6)
    return pl.pallas_call(
        _mod_kernel,
        grid=(depth, n6 // tn),
        in_specs=[pl.BlockSpec((nb, d), lambda l, j: (0, 0)),
                  pl.BlockSpec((None, d, tn), lambda l, j: (l, 0, j)),
                  pl.BlockSpec((None, 1, tn), lambda l, j: (l, 0, j))],
        out_specs=pl.BlockSpec((None, nb, tn), lambda l, j: (l, 0, j)),
        out_shape=jax.ShapeDtypeStruct((depth, nb, n6), _F32),
        compiler_params=_cparams(2),
        name="adaln_rows",
    )(c_all, w_mod, b_mod.reshape(depth, 1, n6))


def _norm_mod_group(xs, mod_ref, g_ref, s, d):
    ms = jnp.mean(xs * xs, axis=-1, keepdims=True)
    xn = (xs * lax.rsqrt(ms + EPS)) * g_ref[...]
    return xn * (1.0 + mod_ref[s:s + 1, d:2 * d]) + mod_ref[s:s + 1, 0:d]


def _conv_mixer_kernel(xp_ref, xs_ref, mod_ref, st_ref, g_ref, wb_ref, wc_ref, wv_ref, cw_ref, wo_ref,
                       out_ref, tail_ref, h_scr, acc_scr, ubuf, gbuf, carry,
                       *, gp, d, n_prompt_tiles, tiles_per_seq):
    i = pl.program_id(0)
    j = pl.program_id(1)
    nj = pl.num_programs(1)
    is_sample = i >= n_prompt_tiles

    def x_rows(s):
        sl = slice(s * GROUP, (s + 1) * GROUP)
        return jnp.where(is_sample, xs_ref[sl, :], xp_ref[sl, :])

    @pl.when(j == 0)
    def _():
        for s in range(gp):
            h_scr[s * GROUP:(s + 1) * GROUP, :] = _norm_mod_group(x_rows(s), mod_ref, g_ref, s, d).astype(_BF16)
        acc_scr[...] = jnp.zeros_like(acc_scr)

    @pl.when(i == 0)
    def _():
        carry[j] = jnp.zeros((CONV_HDR, carry.shape[2]), _F32)

    hb = h_scr[...]
    bq = jnp.dot(hb, wb_ref[...], preferred_element_type=_F32)
    cq = jnp.dot(hb, wc_ref[...], preferred_element_type=_F32)
    vq = jnp.dot(hb, wv_ref[...], preferred_element_type=_F32)
    u = cq * vq

    stride = GROUP + CONV_HDR
    for s in range(gp):
        if s == 0:
            prev = jnp.where(lax.rem(i, tiles_per_seq) == 0, 0.0, carry[j])
        else:
            prev = u[s * GROUP - CONV_HDR:s * GROUP, :]
        ubuf[s * stride:s * stride + CONV_HDR, :] = jnp.where(is_sample, st_ref[s], prev)
        ubuf[s * stride + CONV_HDR:(s + 1) * stride, :] = u[s * GROUP:(s + 1) * GROUP, :]
        tail_ref[s] = u[(s + 1) * GROUP - (CONV_W - 1):(s + 1) * GROUP, :]
    carry[j] = u[gp * GROUP - CONV_HDR:gp * GROUP, :]

    w0 = cw_ref[0:1, :]
    w1 = cw_ref[1:2, :]
    w2 = cw_ref[2:3, :]
    for s in range(gp):
        base = s * stride + CONV_HDR
        u0 = ubuf[base:base + GROUP, :]
        u1 = ubuf[base - 1:base - 1 + GROUP, :]
        u2 = ubuf[base - 2:base - 2 + GROUP, :]
        conv = u2 * w0 + u1 * w1 + u0 * w2
        gbuf[s * GROUP:(s + 1) * GROUP, :] = (bq[s * GROUP:(s + 1) * GROUP, :] * conv).astype(_BF16)
    acc_scr[...] += jnp.dot(gbuf[...], wo_ref[...], preferred_element_type=_F32)

    @pl.when(j == nj - 1)
    def _():
        for s in range(gp):
            sl = slice(s * GROUP, (s + 1) * GROUP)
            out_ref[sl, :] = x_rows(s) + mod_ref[s:s + 1, 2 * d:3 * d] * acc_scr[sl, :]


def _conv_mixer(xp, xs, mod_g, st_tbl, g_mix, w_in_bf, conv_w, w_out_bf, *, tm, n_prompt_tiles, tiles_per_seq):
    d = xp.shape[1]
    t = xp.shape[0] + xs.shape[0]
    npt = n_prompt_tiles
    gp = tm // GROUP
    tn = min(512, d)
    nj = d // tn
    kern = functools.partial(_conv_mixer_kernel, gp=gp, d=d, n_prompt_tiles=n_prompt_tiles,
                             tiles_per_seq=tiles_per_seq)
    return pl.pallas_call(
        kern,
        grid=(t // tm, nj),
        in_specs=[pl.BlockSpec((tm, d), lambda i, j: (jnp.minimum(i, npt - 1), 0)),
                  pl.BlockSpec((tm, d), lambda i, j: (jnp.maximum(i - npt, 0), 0)),
                  pl.BlockSpec((gp, 3 * d), lambda i, j: (i, 0)),
                  pl.BlockSpec((gp, CONV_HDR, tn), lambda i, j: (jnp.maximum(i - npt, 0), 0, j)),
                  pl.BlockSpec((1, d), lambda i, j: (0, 0)),
                  pl.BlockSpec((d, tn), lambda i, j: (0, j)),
                  pl.BlockSpec((d, tn), lambda i, j: (0, nj + j)),
                  pl.BlockSpec((d, tn), lambda i, j: (0, 2 * nj + j)),
                  pl.BlockSpec((CONV_W, tn), lambda i, j: (0, j)),
                  pl.BlockSpec((tn, d), lambda i, j: (j, 0))],
        out_specs=[pl.BlockSpec((tm, d), lambda i, j: (i, 0)),
                   pl.BlockSpec((gp, CONV_W - 1, tn), lambda i, j: (i, 0, j))],
        out_shape=[jax.ShapeDtypeStruct((t, d), _F32),
                   jax.ShapeDtypeStruct((t // GROUP, CONV_W - 1, d), _F32)],
        scratch_shapes=[pltpu.VMEM((tm, d), _BF16),
                        pltpu.VMEM((tm, d), _F32),
                        pltpu.VMEM((gp * (GROUP + CONV_HDR), tn), _F32),
                        pltpu.VMEM((tm, tn), _BF16),
                        pltpu.VMEM((nj, CONV_HDR, tn), _F32)],
        compiler_params=_cparams(2),
        name="conv_mixer",
    )(xp, xs, mod_g, st_tbl, g_mix.reshape(1, d), w_in_bf, w_in_bf, w_in_bf, conv_w, w_out_bf)


def _pool_mixer_kernel(x_ref, mod_ref, st_ref, g_ref, pw_ref, ps_ref, out_ref, tail_ref,
                       dbuf, carry, *, gp, d, n_prompt_tiles, tiles_per_seq, tm):
    i = pl.program_id(0)
    pg = d // len(POOL_WINDOWS)
    is_sample = i >= n_prompt_tiles
    row = lax.broadcasted_iota(jnp.int32, (GROUP, 1), 0)

    @pl.when(i == 0)
    def _():
        carry[...] = jnp.zeros_like(carry)

    prev = jnp.where(lax.rem(i, tiles_per_seq) == 0, 0.0, carry[...])
    for s in range(gp):
        h = _norm_mod_group(x_ref[s * GROUP:(s + 1) * GROUP, :], mod_ref, g_ref, s, d)
        hdr = jnp.where(is_sample, st_ref[s], prev)
        he = jnp.concatenate([hdr, h], axis=0)
        pos0 = jnp.where(is_sample, PAST_LEN, lax.rem(i, tiles_per_seq) * tm + s * GROUP)
        pos1 = pos0 + row + 1
        for q, w in enumerate(POOL_WINDOWS):
            cs = slice(q * pg, (q + 1) * pg)
            sw = he[:, cs]
            span = 1
            while span < w:
                sw = sw + pltpu.roll(sw, span, 0)
                span *= 2
            inv_cnt = 1.0 / jnp.minimum(pos1, w).astype(_F32)
            pooled = sw[POOL_HDR:, :] * inv_cnt
            dbuf[s * GROUP:(s + 1) * GROUP, cs] = (pooled - h[:, cs]).astype(_BF16)
        prev = h[GROUP - POOL_HDR:, :]
        tail_ref[s] = h[GROUP - (POOL_HDR - 1):, :]
    carry[...] = prev
    for q in range(len(POOL_WINDOWS)):
        cs = slice(q * pg, (q + 1) * pg)
        y = jnp.dot(dbuf[:, cs], pw_ref[q], preferred_element_type=_F32) * ps_ref[:, cs]
        for s in range(gp):
            sl = slice(s * GROUP, (s + 1) * GROUP)
            out_ref[sl, cs] = x_ref[sl, cs] + mod_ref[s:s + 1, 2 * d + q * pg:2 * d + (q + 1) * pg] * y[sl, :]


def _pool_mixer(x, mod_g, st_tbl, g_mix, pool_w_bf, pool_scale, *, tm, n_prompt_tiles, tiles_per_seq):
    t, d = x.shape
    gp = tm // GROUP
    nq = len(POOL_WINDOWS)
    pg = d // nq
    kern = functools.partial(_pool_mixer_kernel, gp=gp, d=d, n_prompt_tiles=n_prompt_tiles,
                             tiles_per_seq=tiles_per_seq, tm=tm)
    return pl.pallas_call(
        kern,
        grid=(t // tm,),
        in_specs=[pl.BlockSpec((tm, d), lambda i: (i, 0)),
                  pl.BlockSpec((gp, 3 * d), lambda i: (i, 0)),
                  pl.BlockSpec((gp, POOL_HDR, d), lambda i: (jnp.maximum(i - n_prompt_tiles, 0), 0, 0)),
                  pl.BlockSpec((1, d), lambda i: (0, 0)),
                  pl.BlockSpec((nq, pg, pg), lambda i: (0, 0, 0)),
                  pl.BlockSpec((1, d), lambda i: (0, 0))],
        out_specs=[pl.BlockSpec((tm, d), lambda i: (i, 0)),
                   pl.BlockSpec((gp, POOL_HDR - 1, d), lambda i: (i, 0, 0))],
        out_shape=[jax.ShapeDtypeStruct((t, d), _F32),
                   jax.ShapeDtypeStruct((t // GROUP, POOL_HDR - 1, d), _F32)],
        scratch_shapes=[pltpu.VMEM((tm, d), _BF16),
                        pltpu.VMEM((POOL_HDR, d), _F32)],
        compiler_params=_cparams(1),
        name="pool_mixer",
    )(x, mod_g, st_tbl, g_mix.reshape(1, d), pool_w_bf, pool_scale.reshape(1, d))


def _router_kernel(x_ref, mod_ref, g_ref, wr_ref, br_ref, swg_ref, swu_ref, swd_ref,
                   hs_ref, sh_ref, eidx_ref, wts_ref, cnt_ref, run_scr, hb_scr, hf_scr,
                   *, gp, d, n_exp, tm):
    i = pl.program_id(0)
    per_group = n_exp // N_EXPERT_GROUPS
    ns = d // SLAB_COLS

    @pl.when(i == 0)
    def _():
        run_scr[...] = jnp.zeros_like(run_scr)

    for s in range(gp):
        sl = slice(s * GROUP, (s + 1) * GROUP)
        h = _norm_mod_group(x_ref[sl, :], mod_ref, g_ref, s, d)
        hf_scr[sl, :] = h
        hb_scr[sl, :] = h.astype(_BF16)
        for j in range(ns):
            _store_slab_cols(hs_ref, s * GROUP, j, h[:, SLAB_COLS * j:SLAB_COLS * (j + 1)], GROUP, ns)

    logits = lax.dot_general(wr_ref[...], hf_scr[...], (((1,), (1,)), ((), ())),
                             precision=lax.Precision.HIGHEST, preferred_element_type=_F32)
    sub = lax.broadcasted_iota(jnp.int32, (per_group, tm), 0)
    scores, biased, eids = [], [], []
    for g in range(N_EXPERT_GROUPS):
        rs = slice(g * per_group, (g + 1) * per_group)
        sc = jax.nn.sigmoid(logits[rs, :])
        scores.append(sc)
        biased.append(sc + br_ref[rs, :])
        eids.append(sub + g * per_group)

    gscore = []
    for g in range(N_EXPERT_GROUPS):
        m1 = jnp.max(biased[g], axis=0, keepdims=True)
        i1 = jnp.min(jnp.where(biased[g] == m1, sub, per_group), axis=0, keepdims=True)
        m2 = jnp.max(jnp.where(sub == i1, NEG_INF, biased[g]), axis=0, keepdims=True)
        gscore.append(m1 + m2)
    vals = []
    for g in range(N_EXPERT_GROUPS):
        beaten = jnp.zeros((1, tm), jnp.int32)
        for g2 in range(N_EXPERT_GROUPS):
            if g2 == g:
                continue
            beat = (gscore[g2] >= gscore[g]) if g2 < g else (gscore[g2] > gscore[g])
            beaten = beaten + beat.astype(jnp.int32)
        vals.append(jnp.where(beaten < TOPK_GROUPS, biased[g], NEG_INF))

    sel = [jnp.zeros((per_group, tm), _F32) for _ in range(N_EXPERT_GROUPS)]
    idxs, picked = [], []
    for _ in range(TOP_K):
        mx = vals[0]
        for g in range(1, N_EXPERT_GROUPS):
            mx = jnp.maximum(mx, vals[g])
        m = jnp.max(mx, axis=0, keepdims=True)
        ci = jnp.where(vals[0] == m, eids[0], n_exp)
        for g in range(1, N_EXPERT_GROUPS):
            ci = jnp.minimum(ci, jnp.where(vals[g] == m, eids[g], n_exp))
        idx = jnp.min(ci, axis=0, keepdims=True)
        sc_k = jnp.zeros((1, tm), _F32)
        for g in range(N_EXPERT_GROUPS):
            oh = eids[g] == idx
            sc_k = sc_k + jnp.sum(jnp.where(oh, scores[g], 0.0), axis=0, keepdims=True)
            vals[g] = jnp.where(oh, NEG_INF, vals[g])
            sel[g] = jnp.where(oh, 1.0, sel[g])
        idxs.append(idx)
        picked.append(sc_k)
    den = picked[0]
    for k in range(1, TOP_K):
        den = den + picked[k]

    for k in range(TOP_K):
        eidx_ref[k:k + 1, :] = idxs[k]
        wts_ref[k:k + 1, :] = picked[k] / den * ROUTED_SCALE
    sel_all = jnp.concatenate(sel, axis=0)
    run_scr[...] = run_scr[...] + jnp.sum(sel_all, axis=1, keepdims=True)
    cnt_ref[...] = run_scr[...].astype(jnp.int32)

    hb = hb_scr[...]
    sg = jnp.dot(hb, swg_ref[...], preferred_element_type=_F32)
    su = jnp.dot(hb, swu_ref[...], preferred_element_type=_F32)
    sh_ref[...] = jnp.dot((_silu(sg) * su).astype(_BF16), swd_ref[...],
                          preferred_element_type=_F32).astype(_BF16)


def _router(x, mod_g, g_ffn, wr_t, br, swg_bf, swu_bf, swd_bf, *, tm):
    t, d = x.shape
    gp = tm // GROUP
    n_exp = wr_t.shape[0]
    f = swg_bf.shape[1]
    ns = d // SLAB_COLS
    kern = functools.partial(_router_kernel, gp=gp, d=d, n_exp=n_exp, tm=tm)
    const = lambda i: (0, 0)
    return pl.pallas_call(
        kern,
        grid=(t // tm,),
        in_specs=[pl.BlockSpec((tm, d), lambda i: (i, 0)),
                  pl.BlockSpec((gp, 3 * d), lambda i: (i, 1)),
                  pl.BlockSpec((1, d), const),
                  pl.BlockSpec((n_exp, d), const),
                  pl.BlockSpec((n_exp, 1), const),
                  pl.BlockSpec((d, f), const),
                  pl.BlockSpec((d, f), const),
                  pl.BlockSpec((f, d), const)],
        out_specs=[pl.BlockSpec((tm * ns, 128), lambda i: (i, 0)),
                   pl.BlockSpec((tm, d), lambda i: (i, 0)),
                   pl.BlockSpec((TOP_K, tm), lambda i: (0, i)),
                   pl.BlockSpec((TOP_K, tm), lambda i: (0, i)),
                   pl.BlockSpec((n_exp, 128), const)],
        out_shape=[jax.ShapeDtypeStruct((t * ns, 128), _U32),
                   jax.ShapeDtypeStruct((t, d), _BF16),
                   jax.ShapeDtypeStruct((TOP_K, t), jnp.int32),
                   jax.ShapeDtypeStruct((TOP_K, t), _F32),
                   jax.ShapeDtypeStruct((n_exp, 128), jnp.int32)],
        scratch_shapes=[pltpu.VMEM((n_exp, 128), _F32),
                        pltpu.VMEM((tm, d), _BF16),
                        pltpu.VMEM((tm, d), _F32)],
        compiler_params=_cparams(1),
        name="router_shared",
    )(x, mod_g, g_ffn.reshape(1, d), wr_t, br.reshape(n_exp, 1), swg_bf, swu_bf, swd_bf)


def _ffn_pair_kernel(be_ref, ne_ref, nu_ref, idx_hbm, h_hbm, wg_hbm, wu_hbm, wd_hbm, y_hbm,
                     idx_s, xflat, yflat, xmat, hbuf, wg32, wu32, wd32, wgb, wub, wdb,
                     sem_i, sem_g, sem_s, sem_w, *, bm, ns, layer):
    g = pl.program_id(0)
    nu = nu_ref[0]
    nb = 2 * pl.num_programs(0)
    rows = bm * ns
    f = wgb.shape[1]
    tbl = 2 * bm

    def idx_off(blk):
        return pl.multiple_of(lax.rem(blk, IDX_SLOTS) * tbl, tbl)

    def idx_copy(blk, sl=None):
        sl = lax.rem(blk, IDX_SLOTS) if sl is None else sl
        return pltpu.make_async_copy(idx_hbm.at[blk], idx_s.at[pl.ds(pl.multiple_of(sl * tbl, tbl), tbl)],
                                     sem_i.at[sl])

    def weight_copies(e):
        return [pltpu.make_async_copy(src.at[layer, e], dst, sem_w.at[0])
                for src, dst in ((wg_hbm, wg32), (wu_hbm, wu32), (wd_hbm, wd32))]

    def row_gather(half, base, r):
        src = pl.multiple_of(idx_s[base + r] * ns, ns)
        return pltpu.make_async_copy(h_hbm.at[pl.ds(src, ns)], xflat.at[pl.ds((half * bm + r) * ns, ns)],
                                     sem_g.at[half])

    def row_scatter(half, base, r):
        dst = pl.multiple_of(idx_s[base + bm + r] * ns, ns)
        return pltpu.make_async_copy(yflat.at[pl.ds((half * bm + r) * ns, ns)], y_hbm.at[pl.ds(dst, ns)],
                                     sem_s.at[half])

    def block_gather(half):
        return pltpu.make_async_copy(h_hbm.at[pl.ds(0, rows)], xflat.at[pl.ds(half * rows, rows)],
                                     sem_g.at[half])

    def block_scatter(half, dst0=0):
        return pltpu.make_async_copy(yflat.at[pl.ds(half * rows, rows)], y_hbm.at[pl.ds(dst0, rows)],
                                     sem_s.at[half])

    def block(blk, half):
        other = 1 - half
        if half == 0:
            @pl.when(blk == 0)
            def _():
                for cp in weight_copies(be_ref[0]):
                    cp.start()
                yflat[...] = jnp.zeros_like(yflat)
                block_scatter(0, y_hbm.shape[0] - 2 * rows).start()
                tables = [idx_copy(nb + 1, IDX_SLOTS - 1), idx_copy(0), idx_copy(1), idx_copy(2)]
                for cp in tables:
                    cp.start()
                for cp in tables[:2]:
                    cp.wait()
                for r in range(bm):
                    row_gather(0, 0, r).start()

        new_expert = (blk == 0) | (be_ref[blk] != be_ref[jnp.maximum(blk - 1, 0)])

        @pl.when(new_expert)
        def _():
            for cp in weight_copies(be_ref[blk]):
                cp.wait()
            wgb[...] = wg32[...].astype(_BF16)
            wub[...] = wu32[...].astype(_BF16)
            wdb[...] = wd32[...].astype(_BF16)
            for cp in weight_copies(ne_ref[blk]):
                cp.start()

        base_next = idx_off(blk + 1)
        base_prev = idx_off(blk + IDX_SLOTS - 1)
        idx_copy(blk + 3).start()
        block_gather(half).wait()
        idx_copy(blk + 1).wait()

        per_piece = bm // ns
        for j in range(ns):
            lo, hi = _load_slab_cols(xflat, half * bm, j, bm, ns)
            xmat[:, SLAB_COLS * j:SLAB_COLS * j + 128] = lo.astype(_BF16)
            xmat[:, SLAB_COLS * j + 128:SLAB_COLS * (j + 1)] = hi.astype(_BF16)
            for r in range(j * per_piece, (j + 1) * per_piece):
                row_gather(other, base_next, r).start()

        x = xmat[...]
        n_fc = f // SLAB_COLS
        per_piece = bm // n_fc
        for c in range(n_fc):
            cs = slice(SLAB_COLS * c, SLAB_COLS * (c + 1))
            hg = jnp.dot(x, wgb[:, cs], preferred_element_type=_F32)
            hu = jnp.dot(x, wub[:, cs], preferred_element_type=_F32)
            hbuf[:, cs] = (_silu(hg) * hu).astype(_BF16)
            for r in range(c * per_piece, (c + 1) * per_piece):
                row_scatter(other, base_prev, r).start()

        block_scatter(half).wait()
        hb = hbuf[...]
        for j in range(ns):
            o = jnp.dot(hb, wdb[:, SLAB_COLS * j:SLAB_COLS * (j + 1)], preferred_element_type=_F32)
            _store_slab_cols(yflat, half * bm, j, o, bm, ns)

        @pl.when(blk == nu - 1)
        def _():
            base_own = idx_off(blk)
            for r in range(bm):
                row_scatter(half, base_own, r).start()
            block_gather(other).wait()
            idx_copy(blk + 2).wait()
            idx_copy(blk + 3).wait()
            block_scatter(other).wait()
            block_scatter(half).wait()
            for cp in weight_copies(0):
                cp.wait()

    for half in range(2):
        blk = 2 * g + half
        pl.when(blk < nu)(functools.partial(block, blk, half))


def _routed_ffn(blk_e, next_e, n_used, idx, h_slabs, wg, wu, wd, *, layer, bm, n_out_rows):
    nb = idx.shape[0] - 2
    _, _, d, f = wg.shape
    ns = d // SLAB_COLS
    assert f % SLAB_COLS == 0 and bm % ns == 0 and nb % 2 == 0
    kern = functools.partial(_ffn_pair_kernel, bm=bm, ns=ns, layer=layer)
    hbm = pl.BlockSpec(memory_space=pl.ANY)
    grid_spec = pltpu.PrefetchScalarGridSpec(
        num_scalar_prefetch=3,
        grid=(nb // 2,),
        in_specs=[hbm, hbm, hbm, hbm, hbm],
        out_specs=hbm,
        scratch_shapes=[pltpu.SMEM((IDX_SLOTS * 2 * bm,), jnp.int32),
                        pltpu.VMEM((2 * bm * ns, 128), _U32),
                        pltpu.VMEM((2 * bm * ns, 128), _U32),
                        pltpu.VMEM((bm, d), _BF16),
                        pltpu.VMEM((bm, f), _BF16),
                        pltpu.VMEM((d, f), _F32),
                        pltpu.VMEM((d, f), _F32),
                        pltpu.VMEM((f, d), _F32),
                        pltpu.VMEM((d, f), _BF16),
                        pltpu.VMEM((d, f), _BF16),
                        pltpu.VMEM((f, d), _BF16),
                        pltpu.SemaphoreType.DMA((IDX_SLOTS,)),
                        pltpu.SemaphoreType.DMA((2,)),
                        pltpu.SemaphoreType.DMA((2,)),
                        pltpu.SemaphoreType.DMA((1,))])
    return pl.pallas_call(
        kern,
        grid_spec=grid_spec,
        out_shape=jax.ShapeDtypeStruct((n_out_rows * ns, 128), _U32),
        compiler_params=_cparams(1),
        name="routed_ffn",
    )(blk_e, next_e, n_used, idx, h_slabs, wg, wu, wd)


def _combine_kernel(*refs, d, gpt, n_prompt_tiles):
    x_ref, sh_ref, w_ref, mod_ref, gf_ref = refs[:5]
    y_refs = refs[5:5 + TOP_K]
    n_out = 1 if n_prompt_tiles is None else 2
    out_refs = refs[5 + TOP_K:5 + TOP_K + n_out]
    lo_scr, hi_scr = refs[5 + TOP_K + n_out:]
    i = pl.program_id(0)
    ns = d // SLAB_COLS
    tmc = x_ref.shape[0]
    acc_lo = jnp.zeros(lo_scr.shape, _F32)
    acc_hi = jnp.zeros(hi_scr.shape, _F32)
    for k in range(TOP_K):
        lo, hi = _unpack_pair(y_refs[k][...])
        wk = w_ref[:, k:k + 1]
        acc_lo = acc_lo + wk * lo
        acc_hi = acc_hi + wk * hi
    lo_scr[...] = acc_lo
    hi_scr[...] = acc_hi
    cols = []
    for j in range(ns):
        cols += [lo_scr[pl.ds(j, tmc, stride=ns), :], hi_scr[pl.ds(j, tmc, stride=ns), :]]
    acc = sh_ref[...].astype(_F32) + jnp.concatenate(cols, axis=1)
    groups_per_mod = mod_ref.shape[0]
    g0 = lax.rem(i, groups_per_mod // gpt) * gpt
    rows = []
    for s in range(gpt):
        sl = slice(s * GROUP, (s + 1) * GROUP)
        gate = mod_ref[pl.ds(g0 + s, 1), 2 * d:3 * d]
        xo = x_ref[sl, :] + gate * acc[sl, :]
        if n_prompt_tiles is not None:
            ms = jnp.mean(xo * xo, axis=-1, keepdims=True)
            xo = (xo * lax.rsqrt(ms + EPS)) * gf_ref[...]
        rows.append(xo)
    if n_prompt_tiles is None:
        for s in range(gpt):
            out_refs[0][s * GROUP:(s + 1) * GROUP, :] = rows[s]
    else:
        for o_ref, active in ((out_refs[0], i < n_prompt_tiles), (out_refs[1], i >= n_prompt_tiles)):
            @pl.when(active)
            def _(o_ref=o_ref):
                for s in range(gpt):
                    o_ref[s * GROUP:(s + 1) * GROUP, :] = rows[s]


def _combine(x, shared, wts, mod_g, g_final, y, *, tm, split_rows=None):
    t, d = x.shape
    tmc = min(256, tm)
    gpt = tmc // GROUP
    gp = tm // GROUP
    nt = t // tmc
    row_spec = pl.BlockSpec((tmc, d), lambda i: (i, 0))
    if split_rows is None:
        npt = None
        out_specs = row_spec
        out_shape = jax.ShapeDtypeStruct((t, d), _F32)
    else:
        npt = split_rows[0] // tmc
        out_specs = [pl.BlockSpec((tmc, d), lambda i: (jnp.minimum(i, npt - 1), 0)),
                     pl.BlockSpec((tmc, d), lambda i: (jnp.maximum(i - npt, 0), 0))]
        out_shape = [jax.ShapeDtypeStruct((rows, d), _F32) for rows in split_rows]
    kern = functools.partial(_combine_kernel, d=d, gpt=gpt, n_prompt_tiles=npt)
    ns = d // SLAB_COLS
    y_specs = [pl.BlockSpec((tmc * ns, 128), functools.partial(lambda i, k: (k * nt + i, 0), k=k))
               for k in range(TOP_K)]
    return pl.pallas_call(
        kern,
        grid=(nt,),
        scratch_shapes=[pltpu.VMEM((tmc * ns, 128), _F32), pltpu.VMEM((tmc * ns, 128), _F32)],
        in_specs=[row_spec,
                  row_spec,
                  pl.BlockSpec((tmc * ns, TOP_K), lambda i: (i, 0)),
                  pl.BlockSpec((gp, 3 * d), lambda i: (i // (tm // tmc), 1)),
                  pl.BlockSpec((1, d), lambda i: (0, 0))] + y_specs,
        out_specs=out_specs,
        out_shape=out_shape,
        compiler_params=_cparams(1),
        name="combine" if split_rows is None else "combine_final",
    )(x, shared, wts, mod_g, g_final.reshape(1, d), *([y] * TOP_K))


def _dispatch_tables(eidx, counts, *, bm, nb, t):
    n_exp = counts.shape[0]
    low_bits = (t * TOP_K).bit_length()
    low_mask = (1 << low_bits) - 1
    assert t * TOP_K < low_mask and (n_exp + 1) << low_bits < 2 ** 31
    tok_k = jnp.arange(t, dtype=jnp.int32)[None, :] * TOP_K + jnp.arange(TOP_K, dtype=jnp.int32)[:, None]
    real = (eidx << low_bits) + tok_k
    n_pad = (-counts) % bm
    e_col = jnp.arange(n_exp, dtype=jnp.int32)[:, None]
    pad = jnp.where(jnp.arange(bm, dtype=jnp.int32)[None, :] < n_pad[:, None],
                    (e_col << low_bits) + low_mask, (n_exp << low_bits) + low_mask)
    keys = jnp.sort(jnp.concatenate([real.reshape(-1), pad.reshape(-1)]))
    low = keys & low_mask
    is_pad = low == low_mask
    tok = low // TOP_K
    r = jnp.arange(nb * bm, dtype=jnp.int32)
    row_src = jnp.where(is_pad, r % t, tok)
    row_dst = jnp.where(is_pad, TOP_K * t + ((r // bm) % 2) * bm + (r % bm),
                        (low % TOP_K) * t + tok)
    idx = jnp.concatenate([row_src.reshape(nb, bm), row_dst.reshape(nb, bm)], axis=1)
    before = jnp.concatenate([jnp.zeros((bm,), jnp.int32), TOP_K * t + bm + jnp.arange(bm, dtype=jnp.int32)])
    idx = jnp.concatenate([idx, jnp.zeros((1, 2 * bm), jnp.int32), before[None, :]], axis=0)
    blk_e = jnp.minimum(keys.reshape(nb, bm)[:, 0] >> low_bits, n_exp - 1)
    later = jnp.where(blk_e[None, :] > blk_e[:, None], blk_e[None, :], n_exp)
    next_e = jnp.min(later, axis=1)
    next_e = jnp.where(next_e == n_exp, blk_e, next_e).astype(jnp.int32)
    n_used = (jnp.sum(counts + n_pad) // bm).astype(jnp.int32).reshape(1)
    return blk_e, next_e, n_used, idx


def _moe(x, mod_g, g_ffn, g_final, router_w, router_b, wg, wu, wd, swg, swu, swd,
         *, layer, tm, bm, split_rows=None):
    t, d = x.shape
    n_exp = router_w.shape[1]
    h_slabs, shared, eidx, wts, cnt = _router(
        x, mod_g, g_ffn, router_w.T, router_b,
        swg.astype(_BF16), swu.astype(_BF16), swd.astype(_BF16), tm=tm)
    nb = (t * TOP_K) // bm + n_exp
    blk_e, next_e, n_used, idx = _dispatch_tables(eidx, cnt[:, 0], bm=bm, nb=nb, t=t)
    y = _routed_ffn(blk_e, next_e, n_used, idx, h_slabs, wg, wu, wd, layer=layer, bm=bm,
                    n_out_rows=TOP_K * t + 2 * bm)
    w_rows = jnp.repeat(wts.T, d // SLAB_COLS, axis=0)
    return _combine(x, shared, w_rows, mod_g, g_final, y, tm=tm, split_rows=split_rows)


def _tile_rows(seq, dec_rows):
    for tm in (512, 256, 128, 64):
        if seq % tm == 0 and dec_rows % tm == 0:
            return tm
    raise ValueError("sequence lengths must be multiples of 64")


def kernel(x_prompt, x_sample, state_conv, state_pool, c_prompt, c_sample, w_mod, b_mod, g_mix, g_ffn, g_final, conv_w_in, conv_w, conv_w_out, pool_w, pool_scale, router_w, router_b, exp_w_gate, exp_w_up, exp_w_down, sh_w_gate, sh_w_up, sh_w_down):
    bp, seq, d = x_prompt.shape
    bs, dseq, _ = x_sample.shape
    depth = w_mod.shape[0]
    n_exp = router_w.shape[2]
    assert dseq == GROUP and seq % GROUP == 0 and depth == 2
    assert state_conv.shape[1] == CONV_W - 1 and state_pool.shape[1] == POOL_HDR - 1
    assert n_exp % N_EXPERT_GROUPS == 0 and d % (128 * len(POOL_WINDOWS)) == 0
    tp, ts = bp * seq, bs * dseq
    t = tp + ts
    tm = _tile_rows(seq, ts)
    bm = min(512, t * TOP_K // n_exp)
    assert (t * TOP_K) % bm == 0
    n_prompt_tiles, tiles_per_seq = tp // tm, seq // tm
    ng_p, ng = tp // GROUP, t // GROUP

    nb_pad = -(-(bp + bs) // 8) * 8
    c_all = jnp.concatenate([c_prompt, c_sample, jnp.zeros((nb_pad - bp - bs, d), _F32)], axis=0)
    mods = _mod_rows(c_all, w_mod, b_mod)
    gps = seq // GROUP
    mod_p = jnp.broadcast_to(mods[:, :bp, None, :], (depth, bp, gps, 6 * d)).reshape(depth, bp * gps, 6 * d)
    mod_g = jnp.concatenate([mod_p, mods[:, bp:bp + bs, :]], axis=1)

    conv_tbl = jnp.pad(state_conv, ((0, 0), (CONV_HDR - (CONV_W - 1), 0), (0, 0)))
    pool_tbl = jnp.pad(state_pool, ((0, 0), (1, 0), (0, 0)))

    statics = dict(tm=tm, n_prompt_tiles=n_prompt_tiles, tiles_per_seq=tiles_per_seq)
    x, conv_tail = _conv_mixer(x_prompt.reshape(tp, d), x_sample.reshape(ts, d), mod_g[0], conv_tbl, g_mix[0],
                               conv_w_in.astype(_BF16), conv_w, conv_w_out.astype(_BF16), **statics)
    x = _moe(x, mod_g[0], g_ffn[0], g_final, router_w[0], router_b[0], exp_w_gate, exp_w_up, exp_w_down,
             sh_w_gate[0], sh_w_up[0], sh_w_down[0], layer=0, tm=tm, bm=bm)
    x, pool_tail = _pool_mixer(x, mod_g[1], pool_tbl, g_mix[1], pool_w.astype(_BF16), pool_scale, **statics)
    y_p, y_s = _moe(x, mod_g[1], g_ffn[1], g_final, router_w[1], router_b[1], exp_w_gate, exp_w_up, exp_w_down,
                    sh_w_gate[1], sh_w_up[1], sh_w_down[1], layer=1, tm=tm, bm=bm, split_rows=(tp, ts))

    return (y_p.reshape(bp, seq, d),
            y_s.reshape(bs, dseq, d),
            conv_tail[gps - 1:ng_p:gps],
            conv_tail[ng_p:],
            pool_tail[gps - 1:ng_p:gps],
            pool_tail[ng_p:])
```

```python
import functools

import jax
import jax.numpy as jnp
from jax import lax
from jax.experimental import pallas as pl
from jax.experimental.pallas import tpu as pltpu

GROUP = 64
CONV_W = 3
POOL_WINDOWS = (2, 4, 8, 16)
POOL_HDR = 16
CONV_HDR = 8
TOP_K = 8
N_EXPERT_GROUPS = 8
TOPK_GROUPS = 4
ROUTED_SCALE = 2.5
PAST_LEN = 4096
EPS = 1e-6
VMEM_LIMIT_V7X = 56 * 1024 * 1024
NEG_INF = float("-inf")

_F32 = jnp.float32
_BF16 = jnp.bfloat16


def _silu(v):
    return v * jax.nn.sigmoid(v)


_U32 = jnp.uint32
_HI_MASK = 0xFFFF0000
SLAB_COLS = 256
IDX_SLOTS = 8


def _pack_pair(lo, hi):
    lo_b = lax.bitcast_convert_type(lo.astype(_BF16).astype(_F32), _U32)
    hi_b = lax.bitcast_convert_type(hi.astype(_BF16).astype(_F32), _U32)
    return hi_b | (lo_b >> 16)


def _unpack_pair(p):
    lo = lax.bitcast_convert_type(p << 16, _F32)
    hi = lax.bitcast_convert_type(p & _U32(_HI_MASK), _F32)
    return lo, hi


def _store_slab_cols(ref, row0, j, v, n_rows, ns):
    ref[pl.ds(row0 * ns + j, n_rows, stride=ns), :] = _pack_pair(v[:, :128], v[:, 128:])


def _load_slab_cols(ref, row0, j, n_rows, ns):
    return _unpack_pair(ref[pl.ds(row0 * ns + j, n_rows, stride=ns), :])


def _cparams(n_axes):
    return pltpu.CompilerParams(dimension_semantics=("arbitrary",) * n_axes,
                                vmem_limit_bytes=VMEM_LIMIT_V7X)


def _mod_kernel(c_ref, w_ref, b_ref, o_ref):
    c = c_ref[...]
    ca = _silu(c).astype(_BF16)
    o_ref[...] = jnp.dot(ca, w_ref[...].astype(_BF16), preferred_element_type=_F32) + b_ref[...]


def _mod_rows(c_all, w_mod, b_mod):
    depth, d, n6 = w_mod.shape
    nb = c_all.shape[0]
    tn = min(1024, n6)
    return pl.pallas_call(
        _mod_kernel,
        grid=(depth, n6 // tn),
        in_specs=[pl.BlockSpec((nb, d), lambda l, j: (0, 0)),
                  pl.BlockSpec((None, d, tn), lambda l, j: (l, 0, j)),
                  pl.BlockSpec((None, 1, tn), lambda l, j: (l, 0, j))],
        out_specs=pl.BlockSpec((None, nb, tn), lambda l, j: (l, 0, j)),
        out_shape=jax.ShapeDtypeStruct((depth, nb, n6), _F32),
        compiler_params=_cparams(2),
        name="adaln_rows",
    )(c_all, w_mod, b_mod.reshape(depth, 1, n6))


def _norm_mod_group(xs, mod_ref, g_ref, s, d):
    ms = jnp.mean(xs * xs, axis=-1, keepdims=True)
    xn = (xs * lax.rsqrt(ms + EPS)) * g_ref[...]
    return xn * (1.0 + mod_ref[s:s + 1, d:2 * d]) + mod_ref[s:s + 1, 0:d]


def _conv_mixer_kernel(xp_ref, xs_ref, mod_ref, st_ref, g_ref, wb_ref, wc_ref, wv_ref, cw_ref, wo_ref,
                       out_ref, tail_ref, h_scr, acc_scr, ubuf, gbuf, carry,
                       *, gp, d, n_prompt_tiles, tiles_per_seq):
    i = pl.program_id(0)
    j = pl.program_id(1)
    nj = pl.num_programs(1)
    is_sample = i >= n_prompt_tiles

    def x_rows(s):
        sl = slice(s * GROUP, (s + 1) * GROUP)
        return jnp.where(is_sample, xs_ref[sl, :], xp_ref[sl, :])

    @pl.when(j == 0)
    def _():
        for s in range(gp):
            h_scr[s * GROUP:(s + 1) * GROUP, :] = _norm_mod_group(x_rows(s), mod_ref, g_ref, s, d).astype(_BF16)
        acc_scr[...] = jnp.zeros_like(acc_scr)

    @pl.when(i == 0)
    def _():
        carry[j] = jnp.zeros((CONV_HDR, carry.shape[2]), _F32)

    hb = h_scr[...]
    bq = jnp.dot(hb, wb_ref[...], preferred_element_type=_F32)
    cq = jnp.dot(hb, wc_ref[...], preferred_element_type=_F32)
    vq = jnp.dot(hb, wv_ref[...], preferred_element_type=_F32)
    u = cq * vq

    stride = GROUP + CONV_HDR
    for s in range(gp):
        if s == 0:
            prev = jnp.where(lax.rem(i, tiles_per_seq) == 0, 0.0, carry[j])
        else:
            prev = u[s * GROUP - CONV_HDR:s * GROUP, :]
        ubuf[s * stride:s * stride + CONV_HDR, :] = jnp.where(is_sample, st_ref[s], prev)
        ubuf[s * stride + CONV_HDR:(s + 1) * stride, :] = u[s * GROUP:(s + 1) * GROUP, :]
        tail_ref[s] = u[(s + 1) * GROUP - (CONV_W - 1):(s + 1) * GROUP, :]
    carry[j] = u[gp * GROUP - CONV_HDR:gp * GROUP, :]

    w0 = cw_ref[0:1, :]
    w1 = cw_ref[1:2, :]
    w2 = cw_ref[2:3, :]
    for s in range(gp):
        base = s * stride + CONV_HDR
        u0 = ubuf[base:base + GROUP, :]
        u1 = ubuf[base - 1:base - 1 + GROUP, :]
        u2 = ubuf[base - 2:base - 2 + GROUP, :]
        conv = u2 * w0 + u1 * w1 + u0 * w2
        gbuf[s * GROUP:(s + 1) * GROUP, :] = (bq[s * GROUP:(s + 1) * GROUP, :] * conv).astype(_BF16)
    acc_scr[...] += jnp.dot(gbuf[...], wo_ref[...], preferred_element_type=_F32)

    @pl.when(j == nj - 1)
    def _():
        for s in range(gp):
            sl = slice(s * GROUP, (s + 1) * GROUP)
            out_ref[sl, :] = x_rows(s) + mod_ref[s:s + 1, 2 * d:3 * d] * acc_scr[sl, :]


def _conv_mixer(xp, xs, mod_g, st_tbl, g_mix, w_in_bf, conv_w, w_out_bf, *, tm, n_prompt_tiles, tiles_per_seq):
    d = xp.shape[1]
    t = xp.shape[0] + xs.shape[0]
    npt = n_prompt_tiles
    gp = tm // GROUP
    tn = min(512, d)
    nj = d // tn
    kern = functools.partial(_conv_mixer_kernel, gp=gp, d=d, n_prompt_tiles=n_prompt_tiles,
                             tiles_per_seq=tiles_per_seq)
    return pl.pallas_call(
        kern,
        grid=(t // tm, nj),
        in_specs=[pl.BlockSpec((tm, d), lambda i, j: (jnp.minimum(i, npt - 1), 0)),
                  pl.BlockSpec((tm, d), lambda i, j: (jnp.maximum(i - npt, 0), 0)),
                  pl.BlockSpec((gp, 3 * d), lambda i, j: (i, 0)),
                  pl.BlockSpec((gp, CONV_HDR, tn), lambda i, j: (jnp.maximum(i - npt, 0), 0, j)),
                  pl.BlockSpec((1, d), lambda i, j: (0, 0)),
                  pl.BlockSpec((d, tn), lambda i, j: (0, j)),
                  pl.BlockSpec((d, tn), lambda i, j: (0, nj + j)),
                  pl.BlockSpec((d, tn), lambda i, j: (0, 2 * nj + j)),
                  pl.BlockSpec((CONV_W, tn), lambda i, j: (0, j)),
                  pl.BlockSpec((tn, d), lambda i, j: (j, 0))],
        out_specs=[pl.BlockSpec((tm, d), lambda i, j: (i, 0)),
                   pl.BlockSpec((gp, CONV_W - 1, tn), lambda i, j: (i, 0, j))],
        out_shape=[jax.ShapeDtypeStruct((t, d), _F32),
                   jax.ShapeDtypeStruct((t // GROUP, CONV_W - 1, d), _F32)],
        scratch_shapes=[pltpu.VMEM((tm, d), _BF16),
                        pltpu.VMEM((tm, d), _F32),
                        pltpu.VMEM((gp * (GROUP + CONV_HDR), tn), _F32),
                        pltpu.VMEM((tm, tn), _BF16),
                        pltpu.VMEM((nj, CONV_HDR, tn), _F32)],
        compiler_params=_cparams(2),
        name="conv_mixer",
    )(xp, xs, mod_g, st_tbl, g_mix.reshape(1, d), w_in_bf, w_in_bf, w_in_bf, conv_w, w_out_bf)


def _pool_mixer_kernel(x_ref, mod_ref, st_ref, g_ref, pw_ref, ps_ref, out_ref, tail_ref,
                       dbuf, carry, *, gp, d, n_prompt_tiles, tiles_per_seq, tm):
    i = pl.program_id(0)
    pg = d // len(POOL_WINDOWS)
    is_sample = i >= n_prompt_tiles
    row = lax.broadcasted_iota(jnp.int32, (GROUP, 1), 0)

    @pl.when(i == 0)
    def _():
        carry[...] = jnp.zeros_like(carry)

    prev = jnp.where(lax.rem(i, tiles_per_seq) == 0, 0.0, carry[...])
    for s in range(gp):
        h = _norm_mod_group(x_ref[s * GROUP:(s + 1) * GROUP, :], mod_ref, g_ref, s, d)
        hdr = jnp.where(is_sample, st_ref[s], prev)
        he = jnp.concatenate([hdr, h], axis=0)
        pos0 = jnp.where(is_sample, PAST_LEN, lax.rem(i, tiles_per_seq) * tm + s * GROUP)
        pos1 = pos0 + row + 1
        for q, w in enumerate(POOL_WINDOWS):
            cs = slice(q * pg, (q + 1) * pg)
            sw = he[:, cs]
            span = 1
            while span < w:
                sw = sw + pltpu.roll(sw, span, 0)
                span *= 2
            inv_cnt = 1.0 / jnp.minimum(pos1, w).astype(_F32)
            pooled = sw[POOL_HDR:, :] * inv_cnt
            dbuf[s * GROUP:(s + 1) * GROUP, cs] = (pooled - h[:, cs]).astype(_BF16)
        prev = h[GROUP - POOL_HDR:, :]
        tail_ref[s] = h[GROUP - (POOL_HDR - 1):, :]
    carry[...] = prev
    for q in range(len(POOL_WINDOWS)):
        cs = slice(q * pg, (q + 1) * pg)
        y = jnp.dot(dbuf[:, cs], pw_ref[q], preferred_element_type=_F32) * ps_ref[:, cs]
        for s in range(gp):
            sl = slice(s * GROUP, (s + 1) * GROUP)
            out_ref[sl, cs] = x_ref[sl, cs] + mod_ref[s:s + 1, 2 * d + q * pg:2 * d + (q + 1) * pg] * y[sl, :]


def _pool_mixer(x, mod_g, st_tbl, g_mix, pool_w_bf, pool_scale, *, tm, n_prompt_tiles, tiles_per_seq):
    t, d = x.shape
    gp = tm // GROUP
    nq = len(POOL_WINDOWS)
    pg = d // nq
    kern = functools.partial(_pool_mixer_kernel, gp=gp, d=d, n_prompt_tiles=n_prompt_tiles,
                             tiles_per_seq=tiles_per_seq, tm=tm)
    return pl.pallas_call(
        kern,
        grid=(t // tm,),
        in_specs=[pl.BlockSpec((tm, d), lambda i: (i, 0)),
                  pl.BlockSpec((gp, 3 * d), lambda i: (i, 0)),
                  pl.BlockSpec((gp, POOL_HDR, d), lambda i: (jnp.maximum(i - n_prompt_tiles, 0), 0, 0)),
                  pl.BlockSpec((1, d), lambda i: (0, 0)),
                  pl.BlockSpec((nq, pg, pg), lambda i: (0, 0, 0)),
                  pl.BlockSpec((1, d), lambda i: (0, 0))],
        out_specs=[pl.BlockSpec((tm, d), lambda i: (i, 0)),
                   pl.BlockSpec((gp, POOL_HDR - 1, d), lambda i: (i, 0, 0))],
        out_shape=[jax.ShapeDtypeStruct((t, d), _F32),
                   jax.ShapeDtypeStruct((t // GROUP, POOL_HDR - 1, d), _F32)],
        scratch_shapes=[pltpu.VMEM((tm, d), _BF16),
                        pltpu.VMEM((POOL_HDR, d), _F32)],
        compiler_params=_cparams(1),
        name="pool_mixer",
    )(x, mod_g, st_tbl, g_mix.reshape(1, d), pool_w_bf, pool_scale.reshape(1, d))


def _router_kernel(x_ref, mod_ref, g_ref, wr_ref, br_ref, swg_ref, swu_ref, swd_ref,
                   hs_ref, sh_ref, eidx_ref, wts_ref, cnt_ref, run_scr, hb_scr,
                   *, gp, d, n_exp, tm):
    i = pl.program_id(0)
    per_group = n_exp // N_EXPERT_GROUPS
    ns = d // SLAB_COLS

    @pl.when(i == 0)
    def _():
        run_scr[...] = jnp.zeros_like(run_scr)

    for s in range(gp):
        sl = slice(s * GROUP, (s + 1) * GROUP)
        h = _norm_mod_group(x_ref[sl, :], mod_ref, g_ref, s, d)
        hb_scr[sl, :] = h.astype(_BF16)
        for j in range(ns):
            _store_slab_cols(hs_ref, s * GROUP, j, h[:, SLAB_COLS * j:SLAB_COLS * (j + 1)], GROUP, ns)

    logits = lax.dot_general(wr_ref[...], hb_scr[...], (((1,), (1,)), ((), ())),
                             preferred_element_type=_F32)
    sub = lax.broadcasted_iota(jnp.int32, (per_group, tm), 0)
    scores, biased, eids = [], [], []
    for g in range(N_EXPERT_GROUPS):
        rs = slice(g * per_group, (g + 1) * per_group)
        sc = jax.nn.sigmoid(logits[rs, :])
        scores.append(sc)
        biased.append(sc + br_ref[rs, :])
        eids.append(sub + g * per_group)

    gscore = []
    for g in range(N_EXPERT_GROUPS):
        m1 = jnp.max(biased[g], axis=0, keepdims=True)
        i1 = jnp.min(jnp.where(biased[g] == m1, sub, per_group), axis=0, keepdims=True)
        m2 = jnp.max(jnp.where(sub == i1, NEG_INF, biased[g]), axis=0, keepdims=True)
        gscore.append(m1 + m2)
    vals = []
    for g in range(N_EXPERT_GROUPS):
        beaten = jnp.zeros((1, tm), jnp.int32)
        for g2 in range(N_EXPERT_GROUPS):
            if g2 == g:
                continue
            beat = (gscore[g2] >= gscore[g]) if g2 < g else (gscore[g2] > gscore[g])
            beaten = beaten + beat.astype(jnp.int32)
        vals.append(jnp.where(beaten < TOPK_GROUPS, biased[g], NEG_INF))

    sel = [jnp.zeros((per_group, tm), _F32) for _ in range(N_EXPERT_GROUPS)]
    idxs, picked = [], []
    for _ in range(TOP_K):
        mx = vals[0]
        for g in range(1, N_EXPERT_GROUPS):
            mx = jnp.maximum(mx, vals[g])
        m = jnp.max(mx, axis=0, keepdims=True)
        ci = jnp.where(vals[0] == m, eids[0], n_exp)
        for g in range(1, N_EXPERT_GROUPS):
            ci = jnp.minimum(ci, jnp.where(vals[g] == m, eids[g], n_exp))
        idx = jnp.min(ci, axis=0, keepdims=True)
        sc_k = jnp.zeros((1, tm), _F32)
        for g in range(N_EXPERT_GROUPS):
            oh = eids[g] == idx
            sc_k = sc_k + jnp.sum(jnp.where(oh, scores[g], 0.0), axis=0, keepdims=True)
            vals[g] = jnp.where(oh, NEG_INF, vals[g])
            sel[g] = jnp.where(oh, 1.0, sel[g])
        idxs.append(idx)
        picked.append(sc_k)
    den = picked[0]
    for k in range(1, TOP_K):
        den = den + picked[k]

    for k in range(TOP_K):
        eidx_ref[k:k + 1, :] = idxs[k]
        wts_ref[k:k + 1, :] = picked[k] / den * ROUTED_SCALE
    sel_all = jnp.concatenate(sel, axis=0)
    run_scr[...] = run_scr[...] + jnp.sum(sel_all, axis=1, keepdims=True)
    cnt_ref[...] = run_scr[...].astype(jnp.int32)

    hb = hb_scr[...]
    sg = jnp.dot(hb, swg_ref[...], preferred_element_type=_F32)
    su = jnp.dot(hb, swu_ref[...], preferred_element_type=_F32)
    sh_ref[...] = jnp.dot((_silu(sg) * su).astype(_BF16), swd_ref[...],
                          preferred_element_type=_F32).astype(_BF16)


def _router(x, mod_g, g_ffn, wr_t, br, swg_bf, swu_bf, swd_bf, *, tm):
    t, d = x.shape
    gp = tm // GROUP
    n_exp = wr_t.shape[0]
    f = swg_bf.shape[1]
    ns = d // SLAB_COLS
    kern = functools.partial(_router_kernel, gp=gp, d=d, n_exp=n_exp, tm=tm)
    const = lambda i: (0, 0)
    return pl.pallas_call(
        kern,
        grid=(t // tm,),
        in_specs=[pl.BlockSpec((tm, d), lambda i: (i, 0)),
                  pl.BlockSpec((gp, 3 * d), lambda i: (i, 1)),
                  pl.BlockSpec((1, d), const),
                  pl.BlockSpec((n_exp, d), const),
                  pl.BlockSpec((n_exp, 1), const),
                  pl.BlockSpec((d, f), const),
                  pl.BlockSpec((d, f), const),
                  pl.BlockSpec((f, d), const)],
        out_specs=[pl.BlockSpec((tm * ns, 128), lambda i: (i, 0)),
                   pl.BlockSpec((tm, d), lambda i: (i, 0)),
                   pl.BlockSpec((TOP_K, tm), lambda i: (0, i)),
                   pl.BlockSpec((TOP_K, tm), lambda i: (0, i)),
                   pl.BlockSpec((n_exp, 128), const)],
        out_shape=[jax.ShapeDtypeStruct((t * ns, 128), _U32),
                   jax.ShapeDtypeStruct((t, d), _BF16),
                   jax.ShapeDtypeStruct((TOP_K, t), jnp.int32),
                   jax.ShapeDtypeStruct((TOP_K, t), _F32),
                   jax.ShapeDtypeStruct((n_exp, 128), jnp.int32)],
        scratch_shapes=[pltpu.VMEM((n_exp, 128), _F32),
                        pltpu.VMEM((tm, d), _BF16)],
        compiler_params=_cparams(1),
        name="router_shared",
    )(x, mod_g, g_ffn.reshape(1, d), wr_t, br.reshape(n_exp, 1), swg_bf, swu_bf, swd_bf)


def _ffn_pair_kernel(be_ref, ne_ref, nu_ref, idx_hbm, h_hbm, wg_hbm, wu_hbm, wd_hbm, y_hbm,
                     idx_s, xflat, yflat, xmat, hbuf, wg32, wu32, wd32, wgb, wub, wdb,
                     sem_i, sem_g, sem_s, sem_w, *, bm, ns, layer):
    g = pl.program_id(0)
    nu = nu_ref[0]
    nb = 2 * pl.num_programs(0)
    rows = bm * ns
    f = wgb.shape[1]
    tbl = 2 * bm

    def idx_off(blk):
        return pl.multiple_of(lax.rem(blk, IDX_SLOTS) * tbl, tbl)

    def idx_copy(blk, sl=None):
        sl = lax.rem(blk, IDX_SLOTS) if sl is None else sl
        return pltpu.make_async_copy(idx_hbm.at[blk], idx_s.at[pl.ds(pl.multiple_of(sl * tbl, tbl), tbl)],
                                     sem_i.at[sl])

    def weight_copies(e):
        return [pltpu.make_async_copy(src.at[layer, e], dst, sem_w.at[0])
                for src, dst in ((wg_hbm, wg32), (wu_hbm, wu32), (wd_hbm, wd32))]

    def row_gather(half, base, r):
        src = pl.multiple_of(idx_s[base + r] * ns, ns)
        return pltpu.make_async_copy(h_hbm.at[pl.ds(src, ns)], xflat.at[pl.ds((half * bm + r) * ns, ns)],
                                     sem_g.at[half])

    def row_scatter(half, base, r):
        dst = pl.multiple_of(idx_s[base + bm + r] * ns, ns)
        return pltpu.make_async_copy(yflat.at[pl.ds((half * bm + r) * ns, ns)], y_hbm.at[pl.ds(dst, ns)],
                                     sem_s.at[half])

    def block_gather(half):
        return pltpu.make_async_copy(h_hbm.at[pl.ds(0, rows)], xflat.at[pl.ds(half * rows, rows)],
                                     sem_g.at[half])

    def block_scatter(half, dst0=0):
        return pltpu.make_async_copy(yflat.at[pl.ds(half * rows, rows)], y_hbm.at[pl.ds(dst0, rows)],
                                     sem_s.at[half])

    def block(blk, half):
        other = 1 - half
        if half == 0:
            @pl.when(blk == 0)
            def _():
                for cp in weight_copies(be_ref[0]):
                    cp.start()
                yflat[...] = jnp.zeros_like(yflat)
                block_scatter(0, y_hbm.shape[0] - 2 * rows).start()
                tables = [idx_copy(nb + 1, IDX_SLOTS - 1), idx_copy(0), idx_copy(1), idx_copy(2)]
                for cp in tables:
                    cp.start()
                for cp in tables[:2]:
                    cp.wait()
                for r in range(bm):
                    row_gather(0, 0, r).start()

        new_expert = (blk == 0) | (be_ref[blk] != be_ref[jnp.maximum(blk - 1, 0)])

        @pl.when(new_expert)
        def _():
            for cp in weight_copies(be_ref[blk]):
                cp.wait()
            wgb[...] = wg32[...].astype(_BF16)
            wub[...] = wu32[...].astype(_BF16)
            wdb[...] = wd32[...].astype(_BF16)
            for cp in weight_copies(ne_ref[blk]):
                cp.start()

        base_next = idx_off(blk + 1)
        base_prev = idx_off(blk + IDX_SLOTS - 1)
        idx_copy(blk + 3).start()
        block_gather(half).wait()
        idx_copy(blk + 1).wait()

        per_piece = bm // ns
        for j in range(ns):
            lo, hi = _load_slab_cols(xflat, half * bm, j, bm, ns)
            xmat[:, SLAB_COLS * j:SLAB_COLS * j + 128] = lo.astype(_BF16)
            xmat[:, SLAB_COLS * j + 128:SLAB_COLS * (j + 1)] = hi.astype(_BF16)
            for r in range(j * per_piece, (j + 1) * per_piece):
                row_gather(other, base_next, r).start()

        x = xmat[...]
        n_fc = f // SLAB_COLS
        per_piece = bm // n_fc
        for c in range(n_fc):
            cs = slice(SLAB_COLS * c, SLAB_COLS * (c + 1))
            hg = jnp.dot(x, wgb[:, cs], preferred_element_type=_F32)
            hu = jnp.dot(x, wub[:, cs], preferred_element_type=_F32)
            hbuf[:, cs] = (_silu(hg) * hu).astype(_BF16)
            for r in range(c * per_piece, (c + 1) * per_piece):
                row_scatter(other, base_prev, r).start()

        block_scatter(half).wait()
        hb = hbuf[...]
        for j in range(ns):
            o = jnp.dot(hb, wdb[:, SLAB_COLS * j:SLAB_COLS * (j + 1)], preferred_element_type=_F32)
            _store_slab_cols(yflat, half * bm, j, o, bm, ns)

        @pl.when(blk == nu - 1)
        def _():
            base_own = idx_off(blk)
            for r in range(bm):
                row_scatter(half, base_own, r).start()
            block_gather(other).wait()
            idx_copy(blk + 2).wait()
            idx_copy(blk + 3).wait()
            block_scatter(other).wait()
            block_scatter(half).wait()
            for cp in weight_copies(0):
                cp.wait()

    for half in range(2):
        blk = 2 * g + half
        pl.when(blk < nu)(functools.partial(block, blk, half))


def _routed_ffn(blk_e, next_e, n_used, idx, h_slabs, wg, wu, wd, *, layer, bm, n_out_rows):
    nb = idx.shape[0] - 2
    _, _, d, f = wg.shape
    ns = d // SLAB_COLS
    assert f % SLAB_COLS == 0 and bm % ns == 0 and nb % 2 == 0
    kern = functools.partial(_ffn_pair_kernel, bm=bm, ns=ns, layer=layer)
    hbm = pl.BlockSpec(memory_space=pl.ANY)
    grid_spec = pltpu.PrefetchScalarGridSpec(
        num_scalar_prefetch=3,
        grid=(nb // 2,),
        in_specs=[hbm, hbm, hbm, hbm, hbm],
        out_specs=hbm,
        scratch_shapes=[pltpu.SMEM((IDX_SLOTS * 2 * bm,), jnp.int32),
                        pltpu.VMEM((2 * bm * ns, 128), _U32),
                        pltpu.VMEM((2 * bm * ns, 128), _U32),
                        pltpu.VMEM((bm, d), _BF16),
                        pltpu.VMEM((bm, f), _BF16),
                        pltpu.VMEM((d, f), _F32),
                        pltpu.VMEM((d, f), _F32),
                        pltpu.VMEM((f, d), _F32),
                        pltpu.VMEM((d, f), _BF16),
                        pltpu.VMEM((d, f), _BF16),
                        pltpu.VMEM((f, d), _BF16),
                        pltpu.SemaphoreType.DMA((IDX_SLOTS,)),
                        pltpu.SemaphoreType.DMA((2,)),
                        pltpu.SemaphoreType.DMA((2,)),
                        pltpu.SemaphoreType.DMA((1,))])
    return pl.pallas_call(
        kern,
        grid_spec=grid_spec,
        out_shape=jax.ShapeDtypeStruct((n_out_rows * ns, 128), _U32),
        compiler_params=_cparams(1),
        name="routed_ffn",
    )(blk_e, next_e, n_used, idx, h_slabs, wg, wu, wd)


def _combine_kernel(*refs, d, gpt, n_prompt_tiles):
    x_ref, sh_ref, w_ref, mod_ref, gf_ref = refs[:5]
    y_refs = refs[5:5 + TOP_K]
    n_out = 1 if n_prompt_tiles is None else 2
    out_refs = refs[5 + TOP_K:5 + TOP_K + n_out]
    lo_scr, hi_scr = refs[5 + TOP_K + n_out:]
    i = pl.program_id(0)
    ns = d // SLAB_COLS
    tmc = x_ref.shape[0]
    acc_lo = jnp.zeros(lo_scr.shape, _F32)
    acc_hi = jnp.zeros(hi_scr.shape, _F32)
    for k in range(TOP_K):
        lo, hi = _unpack_pair(y_refs[k][...])
        wk = w_ref[:, k:k + 1]
        acc_lo = acc_lo + wk * lo
        acc_hi = acc_hi + wk * hi
    lo_scr[...] = acc_lo
    hi_scr[...] = acc_hi
    cols = []
    for j in range(ns):
        cols += [lo_scr[pl.ds(j, tmc, stride=ns), :], hi_scr[pl.ds(j, tmc, stride=ns), :]]
    acc = sh_ref[...].astype(_F32) + jnp.concatenate(cols, axis=1)
    groups_per_mod = mod_ref.shape[0]
    g0 = lax.rem(i, groups_per_mod // gpt) * gpt
    rows = []
    for s in range(gpt):
        sl = slice(s * GROUP, (s + 1) * GROUP)
        gate = mod_ref[pl.ds(g0 + s, 1), 2 * d:3 * d]
        xo = x_ref[sl, :] + gate * acc[sl, :]
        if n_prompt_tiles is not None:
            ms = jnp.mean(xo * xo, axis=-1, keepdims=True)
            xo = (xo * lax.rsqrt(ms + EPS)) * gf_ref[...]
        rows.append(xo)
    if n_prompt_tiles is None:
        for s in range(gpt):
            out_refs[0][s * GROUP:(s + 1) * GROUP, :] = rows[s]
    else:
        for o_ref, active in ((out_refs[0], i < n_prompt_tiles), (out_refs[1], i >= n_prompt_tiles)):
            @pl.when(active)
            def _(o_ref=o_ref):
                for s in range(gpt):
                    o_ref[s * GROUP:(s + 1) * GROUP, :] = rows[s]


def _combine(x, shared, wts, mod_g, g_final, y, *, tm, split_rows=None):
    t, d = x.shape
    tmc = min(256, tm)
    gpt = tmc // GROUP
    gp = tm // GROUP
    nt = t // tmc
    row_spec = pl.BlockSpec((tmc, d), lambda i: (i, 0))
    if split_rows is None:
        npt = None
        out_specs = row_spec
        out_shape = jax.ShapeDtypeStruct((t, d), _F32)
    else:
        npt = split_rows[0] // tmc
        out_specs = [pl.BlockSpec((tmc, d), lambda i: (jnp.minimum(i, npt - 1), 0)),
                     pl.BlockSpec((tmc, d), lambda i: (jnp.maximum(i - npt, 0), 0))]
        out_shape = [jax.ShapeDtypeStruct((rows, d), _F32) for rows in split_rows]
    kern = functools.partial(_combine_kernel, d=d, gpt=gpt, n_prompt_tiles=npt)
    ns = d // SLAB_COLS
    y_specs = [pl.BlockSpec((tmc * ns, 128), functools.partial(lambda i, k: (k * nt + i, 0), k=k))
               for k in range(TOP_K)]
    return pl.pallas_call(
        kern,
        grid=(nt,),
        scratch_shapes=[pltpu.VMEM((tmc * ns, 128), _F32), pltpu.VMEM((tmc * ns, 128), _F32)],
        in_specs=[row_spec,
                  row_spec,
                  pl.BlockSpec((tmc * ns, TOP_K), lambda i: (i, 0)),
                  pl.BlockSpec((gp, 3 * d), lambda i: (i // (tm // tmc), 1)),
                  pl.BlockSpec((1, d), lambda i: (0, 0))] + y_specs,
        out_specs=out_specs,
        out_shape=out_shape,
        compiler_params=_cparams(1),
        name="combine" if split_rows is None else "combine_final",
    )(x, shared, wts, mod_g, g_final.reshape(1, d), *([y] * TOP_K))


def _dispatch_tables(eidx, counts, *, bm, nb, t):
    n_exp = counts.shape[0]
    low_bits = (t * TOP_K).bit_length()
    low_mask = (1 << low_bits) - 1
    assert t * TOP_K < low_mask and (n_exp + 1) << low_bits < 2 ** 31
    tok_k = jnp.arange(t, dtype=jnp.int32)[None, :] * TOP_K + jnp.arange(TOP_K, dtype=jnp.int32)[:, None]
    real = (eidx << low_bits) + tok_k
    n_pad = (-counts) % bm
    e_col = jnp.arange(n_exp, dtype=jnp.int32)[:, None]
    pad = jnp.where(jnp.arange(bm, dtype=jnp.int32)[None, :] < n_pad[:, None],
                    (e_col << low_bits) + low_mask, (n_exp << low_bits) + low_mask)
    keys = jnp.sort(jnp.concatenate([real.reshape(-1), pad.reshape(-1)]))
    low = keys & low_mask
    is_pad = low == low_mask
    tok = low // TOP_K
    r = jnp.arange(nb * bm, dtype=jnp.int32)
    row_src = jnp.where(is_pad, r % t, tok)
    row_dst = jnp.where(is_pad, TOP_K * t + ((r // bm) % 2) * bm + (r % bm),
                        (low % TOP_K) * t + tok)
    idx = jnp.concatenate([row_src.reshape(nb, bm), row_dst.reshape(nb, bm)], axis=1)
    before = jnp.concatenate([jnp.zeros((bm,), jnp.int32), TOP_K * t + bm + jnp.arange(bm, dtype=jnp.int32)])
    idx = jnp.concatenate([idx, jnp.zeros((1, 2 * bm), jnp.int32), before[None, :]], axis=0)
    blk_e = jnp.minimum(keys.reshape(nb, bm)[:, 0] >> low_bits, n_exp - 1)
    later = jnp.where(blk_e[None, :] > blk_e[:, None], blk_e[None, :], n_exp)
    next_e = jnp.min(later, axis=1)
    next_e = jnp.where(next_e == n_exp, blk_e, next_e).astype(jnp.int32)
    n_used = (jnp.sum(counts + n_pad) // bm).astype(jnp.int32).reshape(1)
    return blk_e, next_e, n_used, idx


def _moe(x, mod_g, g_ffn, g_final, router_w, router_b, wg, wu, wd, swg, swu, swd,
         *, layer, tm, bm, split_rows=None):
    t, d = x.shape
    n_exp = router_w.shape[1]
    h_slabs, shared, eidx, wts, cnt = _router(
        x, mod_g, g_ffn, router_w.T.astype(_BF16), router_b,
        swg.astype(_BF16), swu.astype(_BF16), swd.astype(_BF16), tm=tm)
    nb = (t * TOP_K) // bm + n_exp
    blk_e, next_e, n_used, idx = _dispatch_tables(eidx, cnt[:, 0], bm=bm, nb=nb, t=t)
    y = _routed_ffn(blk_e, next_e, n_used, idx, h_slabs, wg, wu, wd, layer=layer, bm=bm,
                    n_out_rows=TOP_K * t + 2 * bm)
    w_rows = jnp.repeat(wts.T, d // SLAB_COLS, axis=0)
    return _combine(x, shared, w_rows, mod_g, g_final, y, tm=tm, split_rows=split_rows)


def _tile_rows(seq, dec_rows):
    for tm in (512, 256, 128, 64):
        if seq % tm == 0 and dec_rows % tm == 0:
            return tm
    raise ValueError("sequence lengths must be multiples of 64")


def kernel(x_prompt, x_sample, state_conv, state_pool, c_prompt, c_sample, w_mod, b_mod, g_mix, g_ffn, g_final, conv_w_in, conv_w, conv_w_out, pool_w, pool_scale, router_w, router_b, exp_w_gate, exp_w_up, exp_w_down, sh_w_gate, sh_w_up, sh_w_down):
    bp, seq, d = x_prompt.shape
    bs, dseq, _ = x_sample.shape
    depth = w_mod.shape[0]
    n_exp = router_w.shape[2]
    assert dseq == GROUP and seq % GROUP == 0 and depth == 2
    assert state_conv.shape[1] == CONV_W - 1 and state_pool.shape[1] == POOL_HDR - 1
    assert n_exp % N_EXPERT_GROUPS == 0 and d % (128 * len(POOL_WINDOWS)) == 0
    tp, ts = bp * seq, bs * dseq
    t = tp + ts
    tm = _tile_rows(seq, ts)
    bm = min(512, t * TOP_K // n_exp)
    assert (t * TOP_K) % bm == 0
    n_prompt_tiles, tiles_per_seq = tp // tm, seq // tm
    ng_p, ng = tp // GROUP, t // GROUP

    nb_pad = -(-(bp + bs) // 8) * 8
    c_all = jnp.concatenate([c_prompt, c_sample, jnp.zeros((nb_pad - bp - bs, d), _F32)], axis=0)
    mods = _mod_rows(c_all, w_mod, b_mod)
    gps = seq // GROUP
    mod_p = jnp.broadcast_to(mods[:, :bp, None, :], (depth, bp, gps, 6 * d)).reshape(depth, bp * gps, 6 * d)
    mod_g = jnp.concatenate([mod_p, mods[:, bp:bp + bs, :]], axis=1)

    conv_tbl = jnp.pad(state_conv, ((0, 0), (CONV_HDR - (CONV_W - 1), 0), (0, 0)))
    pool_tbl = jnp.pad(state_pool, ((0, 0), (1, 0), (0, 0)))

    statics = dict(tm=tm, n_prompt_tiles=n_prompt_tiles, tiles_per_seq=tiles_per_seq)
    x, conv_tail = _conv_mixer(x_prompt.reshape(tp, d), x_sample.reshape(ts, d), mod_g[0], conv_tbl, g_mix[0],
                               conv_w_in.astype(_BF16), conv_w, conv_w_out.astype(_BF16), **statics)
    x = _moe(x, mod_g[0], g_ffn[0], g_final, router_w[0], router_b[0], exp_w_gate, exp_w_up, exp_w_down,
             sh_w_gate[0], sh_w_up[0], sh_w_down[0], layer=0, tm=tm, bm=bm)
    x, pool_tail = _pool_mixer(x, mod_g[1], pool_tbl, g_mix[1], pool_w.astype(_BF16), pool_scale, **statics)
    y_p, y_s = _moe(x, mod_g[1], g_ffn[1], g_final, router_w[1], router_b[1], exp_w_gate, exp_w_up, exp_w_down,
                    sh_w_gate[1], sh_w_up[1], sh_w_down[1], layer=1, tm=tm, bm=bm, split_rows=(tp, ts))

    return (y_p.reshape(bp, seq, d),
            y_s.reshape(bs, dseq, d),
            conv_tail[gps - 1:ng_p:gps],
            conv_tail[ng_p:],
            pool_tail[gps - 1:ng_p:gps],
            pool_tail[ng_p:])
```

```python
import functools

import jax
import jax.numpy as jnp
from jax import lax
from jax.experimental import pallas as pl
from jax.experimental.pallas import tpu as pltpu

GROUP = 64
CONV_W = 3
POOL_WINDOWS = (2, 4, 8, 16)
POOL_HDR = 16
CONV_HDR = 8
TOP_K = 8
N_EXPERT_GROUPS = 8
TOPK_GROUPS = 4
ROUTED_SCALE = 2.5
PAST_LEN = 4096
EPS = 1e-6
VMEM_LIMIT_V7X = 56 * 1024 * 1024
NEG_INF = float("-inf")

_F32 = jnp.float32
_BF16 = jnp.bfloat16


def _silu(v):
    return v * jax.nn.sigmoid(v)


_U32 = jnp.uint32
_HI_MASK = 0xFFFF0000
SLAB_COLS = 256
IDX_SLOTS = 8


def _pack_pair(lo, hi):
    lo_b = lax.bitcast_convert_type(lo.astype(_BF16).astype(_F32), _U32)
    hi_b = lax.bitcast_convert_type(hi.astype(_BF16).astype(_F32), _U32)
    return hi_b | (lo_b >> 16)


def _unpack_pair(p):
    lo = lax.bitcast_convert_type(p << 16, _F32)
    hi = lax.bitcast_convert_type(p & _U32(_HI_MASK), _F32)
    return lo, hi


def _store_slab_cols(ref, row0, j, v, n_rows, ns):
    ref[pl.ds(row0 * ns + j, n_rows, stride=ns), :] = _pack_pair(v[:, :128], v[:, 128:])


def _load_slab_cols(ref, row0, j, n_rows, ns):
    return _unpack_pair(ref[pl.ds(row0 * ns + j, n_rows, stride=ns), :])


def _cparams(n_axes):
    return pltpu.CompilerParams(dimension_semantics=("arbitrary",) * n_axes,
                                vmem_limit_bytes=VMEM_LIMIT_V7X)


def _mod_kernel(c_ref, w_ref, b_ref, o_ref):
    c = c_ref[...]
    ca = _silu(c).astype(_BF16)
    o_ref[...] = jnp.dot(ca, w_ref[...].astype(_BF16), preferred_element_type=_F32) + b_ref[...]


def _mod_rows(c_all, w_mod, b_mod):
    depth, d, n6 = w_mod.shape
    nb = c_all.shape[0]
    tn = min(1024, n6)
    return pl.pallas_call(
        _mod_kernel,
        grid=(depth, n6 // tn),
        in_specs=[pl.BlockSpec((nb, d), lambda l, j: (0, 0)),
                  pl.BlockSpec((None, d, tn), lambda l, j: (l, 0, j)),
                  pl.BlockSpec((None, 1, tn), lambda l, j: (l, 0, j))],
        out_specs=pl.BlockSpec((None, nb, tn), lambda l, j: (l, 0, j)),
        out_shape=jax.ShapeDtypeStruct((depth, nb, n6), _F32),
        compiler_params=_cparams(2),
        name="adaln_rows",
    )(c_all, w_mod, b_mod.reshape(depth, 1, n6))


def _norm_mod_group(xs, mod_ref, g_ref, s, d):
    ms = jnp.mean(xs * xs, axis=-1, keepdims=True)
    xn = (xs * lax.rsqrt(ms + EPS)) * g_ref[...]
    return xn * (1.0 + mod_ref[s:s + 1, d:2 * d]) + mod_ref[s:s + 1, 0:d]


def _conv_mixer_kernel(xp_ref, xs_ref, mod_ref, st_ref, g_ref, wb_ref, wc_ref, wv_ref, cw_ref, wo_ref,
                       out_ref, tail_ref, h_scr, acc_scr, ubuf, gbuf, carry,
                       *, gp, d, n_prompt_tiles, tiles_per_seq):
    i = pl.program_id(0)
    j = pl.program_id(1)
    nj = pl.num_programs(1)
    is_sample = i >= n_prompt_tiles

    def x_rows(s):
        sl = slice(s * GROUP, (s + 1) * GROUP)
        return jnp.where(is_sample, xs_ref[sl, :], xp_ref[sl, :])

    @pl.when(j == 0)
    def _():
        for s in range(gp):
            h_scr[s * GROUP:(s + 1) * GROUP, :] = _norm_mod_group(x_rows(s), mod_ref, g_ref, s, d).astype(_BF16)
        acc_scr[...] = jnp.zeros_like(acc_scr)

    @pl.when(i == 0)
    def _():
        carry[j] = jnp.zeros((CONV_HDR, carry.shape[2]), _F32)

    hb = h_scr[...]
    bq = jnp.dot(hb, wb_ref[...], preferred_element_type=_F32)
    cq = jnp.dot(hb, wc_ref[...], preferred_element_type=_F32)
    vq = jnp.dot(hb, wv_ref[...], preferred_element_type=_F32)
    u = cq * vq

    stride = GROUP + CONV_HDR
    for s in range(gp):
        if s == 0:
            prev = jnp.where(lax.rem(i, tiles_per_seq) == 0, 0.0, carry[j])
        else:
            prev = u[s * GROUP - CONV_HDR:s * GROUP, :]
        ubuf[s * stride:s * stride + CONV_HDR, :] = jnp.where(is_sample, st_ref[s], prev)
        ubuf[s * stride + CONV_HDR:(s + 1) * stride, :] = u[s * GROUP:(s + 1) * GROUP, :]
        tail_ref[s] = u[(s + 1) * GROUP - (CONV_W - 1):(s + 1) * GROUP, :]
    carry[j] = u[gp * GROUP - CONV_HDR:gp * GROUP, :]

    w0 = cw_ref[0:1, :]
    w1 = cw_ref[1:2, :]
    w2 = cw_ref[2:3, :]
    for s in range(gp):
        base = s * stride + CONV_HDR
        u0 = ubuf[base:base + GROUP, :]
        u1 = ubuf[base - 1:base - 1 + GROUP, :]
        u2 = ubuf[base - 2:base - 2 + GROUP, :]
        conv = u2 * w0 + u1 * w1 + u0 * w2
        gbuf[s * GROUP:(s + 1) * GROUP, :] = (bq[s * GROUP:(s + 1) * GROUP, :] * conv).astype(_BF16)
    acc_scr[...] += jnp.dot(gbuf[...], wo_ref[...], preferred_element_type=_F32)

    @pl.when(j == nj - 1)
    def _():
        for s in range(gp):
            sl = slice(s * GROUP, (s + 1) * GROUP)
            out_ref[sl, :] = x_rows(s) + mod_ref[s:s + 1, 2 * d:3 * d] * acc_scr[sl, :]


def _conv_mixer(xp, xs, mod_g, st_tbl, g_mix, w_in_bf, conv_w, w_out_bf, *, tm, n_prompt_tiles, tiles_per_seq):
    d = xp.shape[1]
    t = xp.shape[0] + xs.shape[0]
    npt = n_prompt_tiles
    gp = tm // GROUP
    tn = min(512, d)
    nj = d // tn
    kern = functools.partial(_conv_mixer_kernel, gp=gp, d=d, n_prompt_tiles=n_prompt_tiles,
                             tiles_per_seq=tiles_per_seq)
    return pl.pallas_call(
        kern,
        grid=(t // tm, nj),
        in_specs=[pl.BlockSpec((tm, d), lambda i, j: (jnp.minimum(i, npt - 1), 0)),
                  pl.BlockSpec((tm, d), lambda i, j: (jnp.maximum(i - npt, 0), 0)),
                  pl.BlockSpec((gp, 3 * d), lambda i, j: (i, 0)),
                  pl.BlockSpec((gp, CONV_HDR, tn), lambda i, j: (jnp.maximum(i - npt, 0), 0, j)),
                  pl.BlockSpec((1, d), lambda i, j: (0, 0)),
                  pl.BlockSpec((d, tn), lambda i, j: (0, j)),
                  pl.BlockSpec((d, tn), lambda i, j: (0, nj + j)),
                  pl.BlockSpec((d, tn), lambda i, j: (0, 2 * nj + j)),
                  pl.BlockSpec((CONV_W, tn), lambda i, j: (0, j)),
                  pl.BlockSpec((tn, d), lambda i, j: (j, 0))],
        out_specs=[pl.BlockSpec((tm, d), lambda i, j: (i, 0)),
                   pl.BlockSpec((gp, CONV_W - 1, tn), lambda i, j: (i, 0, j))],
        out_shape=[jax.ShapeDtypeStruct((t, d), _F32),
                   jax.ShapeDtypeStruct((t // GROUP, CONV_W - 1, d), _F32)],
        scratch_shapes=[pltpu.VMEM((tm, d), _BF16),
                        pltpu.VMEM((tm, d), _F32),
                        pltpu.VMEM((gp * (GROUP + CONV_HDR), tn), _F32),
                        pltpu.VMEM((tm, tn), _BF16),
                        pltpu.VMEM((nj, CONV_HDR, tn), _F32)],
        compiler_params=_cparams(2),
        name="conv_mixer",
    )(xp, xs, mod_g, st_tbl, g_mix.reshape(1, d), w_in_bf, w_in_bf, w_in_bf, conv_w, w_out_bf)


def _pool_mixer_kernel(x_ref, mod_ref, st_ref, g_ref, pw_ref, ps_ref, out_ref, tail_ref,
                       dbuf, carry, *, gp, d, n_prompt_tiles, tiles_per_seq, tm):
    i = pl.program_id(0)
    pg = d // len(POOL_WINDOWS)
    is_sample = i >= n_prompt_tiles
    row = lax.broadcasted_iota(jnp.int32, (GROUP, 1), 0)

    @pl.when(i == 0)
    def _():
        carry[...] = jnp.zeros_like(carry)

    prev = jnp.where(lax.rem(i, tiles_per_seq) == 0, 0.0, carry[...])
    for s in range(gp):
        h = _norm_mod_group(x_ref[s * GROUP:(s + 1) * GROUP, :], mod_ref, g_ref, s, d)
        hdr = jnp.where(is_sample, st_ref[s], prev)
        he = jnp.concatenate([hdr, h], axis=0)
        pos0 = jnp.where(is_sample, PAST_LEN, lax.rem(i, tiles_per_seq) * tm + s * GROUP)
        pos1 = pos0 + row + 1
        for q, w in enumerate(POOL_WINDOWS):
            cs = slice(q * pg, (q + 1) * pg)
            sw = he[:, cs]
            span = 1
            while span < w:
                sw = sw + pltpu.roll(sw, span, 0)
                span *= 2
            inv_cnt = 1.0 / jnp.minimum(pos1, w).astype(_F32)
            pooled = sw[POOL_HDR:, :] * inv_cnt
            dbuf[s * GROUP:(s + 1) * GROUP, cs] = (pooled - h[:, cs]).astype(_BF16)
        prev = h[GROUP - POOL_HDR:, :]
        tail_ref[s] = h[GROUP - (POOL_HDR - 1):, :]
    carry[...] = prev
    for q in range(len(POOL_WINDOWS)):
        cs = slice(q * pg, (q + 1) * pg)
        y = jnp.dot(dbuf[:, cs], pw_ref[q], preferred_element_type=_F32) * ps_ref[:, cs]
        for s in range(gp):
            sl = slice(s * GROUP, (s + 1) * GROUP)
            out_ref[sl, cs] = x_ref[sl, cs] + mod_ref[s:s + 1, 2 * d + q * pg:2 * d + (q + 1) * pg] * y[sl, :]


def _pool_mixer(x, mod_g, st_tbl, g_mix, pool_w_bf, pool_scale, *, tm, n_prompt_tiles, tiles_per_seq):
    t, d = x.shape
    gp = tm // GROUP
    nq = len(POOL_WINDOWS)
    pg = d // nq
    kern = functools.partial(_pool_mixer_kernel, gp=gp, d=d, n_prompt_tiles=n_prompt_tiles,
                             tiles_per_seq=tiles_per_seq, tm=tm)
    return pl.pallas_call(
        kern,
        grid=(t // tm,),
        in_specs=[pl.BlockSpec((tm, d), lambda i: (i, 0)),
                  pl.BlockSpec((gp, 3 * d), lambda i: (i, 0)),
                  pl.BlockSpec((gp, POOL_HDR, d), lambda i: (jnp.maximum(i - n_prompt_tiles, 0), 0, 0)),
                  pl.BlockSpec((1, d), lambda i: (0, 0)),
                  pl.BlockSpec((nq, pg, pg), lambda i: (0, 0, 0)),
                  pl.BlockSpec((1, d), lambda i: (0, 0))],
        out_specs=[pl.BlockSpec((tm, d), lambda i: (i, 0)),
                   pl.BlockSpec((gp, POOL_HDR - 1, d), lambda i: (i, 0, 0))],
        out_shape=[jax.ShapeDtypeStruct((t, d), _F32),
                   jax.ShapeDtypeStruct((t // GROUP, POOL_HDR - 1, d), _F32)],
        scratch_shapes=[pltpu.VMEM((tm, d), _BF16),
                        pltpu.VMEM((POOL_HDR, d), _F32)],
        compiler_params=_cparams(1),
        name="pool_mixer",
    )(x, mod_g, st_tbl, g_mix.reshape(1, d), pool_w_bf, pool_scale.reshape(1, d))


def _router_kernel(x_ref, mod_ref, g_ref, wr_ref, br_ref, swg_ref, swu_ref, swd_ref,
                   hs_ref, sh_ref, eidx_ref, wts_ref, cnt_ref, run_scr, hb_scr,
                   *, gp, d, n_exp, tm):
    i = pl.program_id(0)
    per_group = n_exp // N_EXPERT_GROUPS
    ns = d // SLAB_COLS

    @pl.when(i == 0)
    def _():
        run_scr[...] = jnp.zeros_like(run_scr)

    for s in range(gp):
        sl = slice(s * GROUP, (s + 1) * GROUP)
        h = _norm_mod_group(x_ref[sl, :], mod_ref, g_ref, s, d)
        hb_scr[sl, :] = h.astype(_BF16)
        for j in range(ns):
            _store_slab_cols(hs_ref, s * GROUP, j, h[:, SLAB_COLS * j:SLAB_COLS * (j + 1)], GROUP, ns)

    logits = lax.dot_general(wr_ref[...], hb_scr[...], (((1,), (1,)), ((), ())),
                             preferred_element_type=_F32)
    sub = lax.broadcasted_iota(jnp.int32, (per_group, tm), 0)
    scores, biased, eids = [], [], []
    for g in range(N_EXPERT_GROUPS):
        rs = slice(g * per_group, (g + 1) * per_group)
        sc = jax.nn.sigmoid(logits[rs, :])
        scores.append(sc)
        biased.append(sc + br_ref[rs, :])
        eids.append(sub + g * per_group)

    gscore = []
    for g in range(N_EXPERT_GROUPS):
        m1 = jnp.max(biased[g], axis=0, keepdims=True)
        i1 = jnp.min(jnp.where(biased[g] == m1, sub, per_group), axis=0, keepdims=True)
        m2 = jnp.max(jnp.where(sub == i1, NEG_INF, biased[g]), axis=0, keepdims=True)
        gscore.append(m1 + m2)
    vals = []
    for g in range(N_EXPERT_GROUPS):
        beaten = jnp.zeros((1, tm), jnp.int32)
        for g2 in range(N_EXPERT_GROUPS):
            if g2 == g:
                continue
            beat = (gscore[g2] >= gscore[g]) if g2 < g else (gscore[g2] > gscore[g])
            beaten = beaten + beat.astype(jnp.int32)
        vals.append(jnp.where(beaten < TOPK_GROUPS, biased[g], NEG_INF))

    sel = [jnp.zeros((per_group, tm), _F32) for _ in range(N_EXPERT_GROUPS)]
    idxs, picked = [], []
    for _ in range(TOP_K):
        mx = vals[0]
        for g in range(1, N_EXPERT_GROUPS):
            mx = jnp.maximum(mx, vals[g])
        m = jnp.max(mx, axis=0, keepdims=True)
        ci = jnp.where(vals[0] == m, eids[0], n_exp)
        for g in range(1, N_EXPERT_GROUPS):
            ci = jnp.minimum(ci, jnp.where(vals[g] == m, eids[g], n_exp))
        idx = jnp.min(ci, axis=0, keepdims=True)
        sc_k = jnp.zeros((1, tm), _F32)
        for g in range(N_EXPERT_GROUPS):
            oh = eids[g] == idx
            sc_k = sc_k + jnp.sum(jnp.where(oh, scores[g], 0.0), axis=0, keepdims=True)
            vals[g] = jnp.where(oh, NEG_INF, vals[g])
            sel[g] = jnp.where(oh, 1.0, sel[g])
        idxs.append(idx)
        picked.append(sc_k)
    den = picked[0]
    for k in range(1, TOP_K):
        den = den + picked[k]

    for k in range(TOP_K):
        eidx_ref[k:k + 1, :] = idxs[k]
        wts_ref[k:k + 1, :] = picked[k] / den * ROUTED_SCALE
    sel_all = jnp.concatenate(sel, axis=0)
    run_scr[...] = run_scr[...] + jnp.sum(sel_all, axis=1, keepdims=True)
    cnt_ref[...] = run_scr[...].astype(jnp.int32)

    hb = hb_scr[...]
    sg = jnp.dot(hb, swg_ref[...], preferred_element_type=_F32)
    su = jnp.dot(hb, swu_ref[...], preferred_element_type=_F32)
    sh_ref[...] = jnp.dot((_silu(sg) * su).astype(_BF16), swd_ref[...],
                          preferred_element_type=_F32).astype(_BF16)


def _router(x, mod_g, g_ffn, wr_t, br, swg_bf, swu_bf, swd_bf, *, tm):
    t, d = x.shape
    gp = tm // GROUP
    n_exp = wr_t.shape[0]
    f = swg_bf.shape[1]
    ns = d // SLAB_COLS
    kern = functools.partial(_router_kernel, gp=gp, d=d, n_exp=n_exp, tm=tm)
    const = lambda i: (0, 0)
    return pl.pallas_call(
        kern,
        grid=(t // tm,),
        in_specs=[pl.BlockSpec((tm, d), lambda i: (i, 0)),
                  pl.BlockSpec((gp, 3 * d), lambda i: (i, 1)),
                  pl.BlockSpec((1, d), const),
                  pl.BlockSpec((n_exp, d), const),
                  pl.BlockSpec((n_exp, 1), const),
                  pl.BlockSpec((d, f), const),
                  pl.BlockSpec((d, f), const),
                  pl.BlockSpec((f, d), const)],
        out_specs=[pl.BlockSpec((tm * ns, 128), lambda i: (i, 0)),
                   pl.BlockSpec((tm, d), lambda i: (i, 0)),
                   pl.BlockSpec((TOP_K, tm), lambda i: (0, i)),
                   pl.BlockSpec((TOP_K, tm), lambda i: (0, i)),
                   pl.BlockSpec((n_exp, 128), const)],
        out_shape=[jax.ShapeDtypeStruct((t * ns, 128), _U32),
                   jax.ShapeDtypeStruct((t, d), _BF16),
                   jax.ShapeDtypeStruct((TOP_K, t), jnp.int32),
                   jax.ShapeDtypeStruct((TOP_K, t), _F32),
                   jax.ShapeDtypeStruct((n_exp, 128), jnp.int32)],
        scratch_shapes=[pltpu.VMEM((n_exp, 128), _F32),
                        pltpu.VMEM((tm, d), _BF16)],
        compiler_params=_cparams(1),
        name="router_shared",
    )(x, mod_g, g_ffn.reshape(1, d), wr_t, br.reshape(n_exp, 1), swg_bf, swu_bf, swd_bf)


def _ffn_pair_kernel(be_ref, ne_ref, nu_ref, idx_hbm, h_hbm, wg_hbm, wu_hbm, wd_hbm, y_hbm,
                     idx_s, xflat, yflat, xmat, hbuf, wg32, wu32, wd32, wgb, wub, wdb,
                     sem_i, sem_g, sem_s, sem_w, *, bm, ns, layer):
    g = pl.program_id(0)
    nu = nu_ref[0]
    nb = 2 * pl.num_programs(0)
    rows = bm * ns
    f = wgb.shape[1]
    tbl = 2 * bm

    def idx_off(blk):
        return pl.multiple_of(lax.rem(blk, IDX_SLOTS) * tbl, tbl)

    def idx_copy(blk, sl=None):
        sl = lax.rem(blk, IDX_SLOTS) if sl is None else sl
        return pltpu.make_async_copy(idx_hbm.at[blk], idx_s.at[pl.ds(pl.multiple_of(sl * tbl, tbl), tbl)],
                                     sem_i.at[sl])

    def weight_copies(e):
        return [pltpu.make_async_copy(src.at[layer, e], dst, sem_w.at[0])
                for src, dst in ((wg_hbm, wg32), (wu_hbm, wu32), (wd_hbm, wd32))]

    def row_gather(half, base, r):
        src = pl.multiple_of(idx_s[base + r] * ns, ns)
        return pltpu.make_async_copy(h_hbm.at[pl.ds(src, ns)], xflat.at[pl.ds((half * bm + r) * ns, ns)],
                                     sem_g.at[half])

    def row_scatter(half, base, r):
        dst = pl.multiple_of(idx_s[base + bm + r] * ns, ns)
        return pltpu.make_async_copy(yflat.at[pl.ds((half * bm + r) * ns, ns)], y_hbm.at[pl.ds(dst, ns)],
                                     sem_s.at[half])

    def block_gather(half):
        return pltpu.make_async_copy(h_hbm.at[pl.ds(0, rows)], xflat.at[pl.ds(half * rows, rows)],
                                     sem_g.at[half])

    def block_scatter(half, dst0=0):
        return pltpu.make_async_copy(yflat.at[pl.ds(half * rows, rows)], y_hbm.at[pl.ds(dst0, rows)],
                                     sem_s.at[half])

    def block(blk, half):
        other = 1 - half
        if half == 0:
            @pl.when(blk == 0)
            def _():
                for cp in weight_copies(be_ref[0]):
                    cp.start()
                yflat[...] = jnp.zeros_like(yflat)
                block_scatter(0, y_hbm.shape[0] - 2 * rows).start()
                tables = [idx_copy(nb + 1, IDX_SLOTS - 1), idx_copy(0), idx_copy(1), idx_copy(2)]
                for cp in tables:
                    cp.start()
                for cp in tables[:2]:
                    cp.wait()
                for r in range(bm):
                    row_gather(0, 0, r).start()

        new_expert = (blk == 0) | (be_ref[blk] != be_ref[jnp.maximum(blk - 1, 0)])

        @pl.when(new_expert)
        def _():
            for cp in weight_copies(be_ref[blk]):
                cp.wait()
            wgb[...] = wg32[...].astype(_BF16)
            wub[...] = wu32[...].astype(_BF16)
            wdb[...] = wd32[...].astype(_BF16)
            for cp in weight_copies(ne_ref[blk]):
                cp.start()

        base_next = idx_off(blk + 1)
        base_prev = idx_off(blk + IDX_SLOTS - 1)
        idx_copy(blk + 3).start()
        block_gather(half).wait()
        idx_copy(blk + 1).wait()

        per_piece = bm // ns
        for j in range(ns):
            lo, hi = _load_slab_cols(xflat, half * bm, j, bm, ns)
            xmat[:, SLAB_COLS * j:SLAB_COLS * j + 128] = lo.astype(_BF16)
            xmat[:, SLAB_COLS * j + 128:SLAB_COLS * (j + 1)] = hi.astype(_BF16)
            for r in range(j * per_piece, (j + 1) * per_piece):
                row_gather(other, base_next, r).start()
                row_scatter(other, base_prev, r).start()

        x = xmat[...]
        n_fc = f // SLAB_COLS
        for c in range(n_fc):
            cs = slice(SLAB_COLS * c, SLAB_COLS * (c + 1))
            hg = jnp.dot(x, wgb[:, cs], preferred_element_type=_F32)
            hu = jnp.dot(x, wub[:, cs], preferred_element_type=_F32)
            hbuf[:, cs] = (_silu(hg) * hu).astype(_BF16)

        block_scatter(half).wait()
        hb = hbuf[...]
        for j in range(ns):
            o = jnp.dot(hb, wdb[:, SLAB_COLS * j:SLAB_COLS * (j + 1)], preferred_element_type=_F32)
            _store_slab_cols(yflat, half * bm, j, o, bm, ns)

        @pl.when(blk == nu - 1)
        def _():
            base_own = idx_off(blk)
            for r in range(bm):
                row_scatter(half, base_own, r).start()
            block_gather(other).wait()
            idx_copy(blk + 2).wait()
            idx_copy(blk + 3).wait()
            block_scatter(other).wait()
            block_scatter(half).wait()
            for cp in weight_copies(0):
                cp.wait()

    for half in range(2):
        blk = 2 * g + half
        pl.when(blk < nu)(functools.partial(block, blk, half))


def _routed_ffn(blk_e, next_e, n_used, idx, h_slabs, wg, wu, wd, *, layer, bm, n_out_rows):
    nb = idx.shape[0] - 2
    _, _, d, f = wg.shape
    ns = d // SLAB_COLS
    assert f % SLAB_COLS == 0 and bm % ns == 0 and nb % 2 == 0
    kern = functools.partial(_ffn_pair_kernel, bm=bm, ns=ns, layer=layer)
    hbm = pl.BlockSpec(memory_space=pl.ANY)
    grid_spec = pltpu.PrefetchScalarGridSpec(
        num_scalar_prefetch=3,
        grid=(nb // 2,),
        in_specs=[hbm, hbm, hbm, hbm, hbm],
        out_specs=hbm,
        scratch_shapes=[pltpu.SMEM((IDX_SLOTS * 2 * bm,), jnp.int32),
                        pltpu.VMEM((2 * bm * ns, 128), _U32),
                        pltpu.VMEM((2 * bm * ns, 128), _U32),
                        pltpu.VMEM((bm, d), _BF16),
                        pltpu.VMEM((bm, f), _BF16),
                        pltpu.VMEM((d, f), _F32),
                        pltpu.VMEM((d, f), _F32),
                        pltpu.VMEM((f, d), _F32),
                        pltpu.VMEM((d, f), _BF16),
                        pltpu.VMEM((d, f), _BF16),
                        pltpu.VMEM((f, d), _BF16),
                        pltpu.SemaphoreType.DMA((IDX_SLOTS,)),
                        pltpu.SemaphoreType.DMA((2,)),
                        pltpu.SemaphoreType.DMA((2,)),
                        pltpu.SemaphoreType.DMA((1,))])
    return pl.pallas_call(
        kern,
        grid_spec=grid_spec,
        out_shape=jax.ShapeDtypeStruct((n_out_rows * ns, 128), _U32),
        compiler_params=_cparams(1),
        name="routed_ffn",
    )(blk_e, next_e, n_used, idx, h_slabs, wg, wu, wd)


def _combine_kernel(*refs, d, gpt, n_prompt_tiles):
    x_ref, sh_ref, w_ref, mod_ref, gf_ref = refs[:5]
    y_refs = refs[5:5 + TOP_K]
    n_out = 1 if n_prompt_tiles is None else 2
    out_refs = refs[5 + TOP_K:5 + TOP_K + n_out]
    lo_scr, hi_scr = refs[5 + TOP_K + n_out:]
    i = pl.program_id(0)
    ns = d // SLAB_COLS
    tmc = x_ref.shape[0]
    acc_lo = jnp.zeros(lo_scr.shape, _F32)
    acc_hi = jnp.zeros(hi_scr.shape, _F32)
    for k in range(TOP_K):
        lo, hi = _unpack_pair(y_refs[k][...])
        wk = w_ref[:, k:k + 1]
        acc_lo = acc_lo + wk * lo
        acc_hi = acc_hi + wk * hi
    lo_scr[...] = acc_lo
    hi_scr[...] = acc_hi
    cols = []
    for j in range(ns):
        cols += [lo_scr[pl.ds(j, tmc, stride=ns), :], hi_scr[pl.ds(j, tmc, stride=ns), :]]
    acc = sh_ref[...].astype(_F32) + jnp.concatenate(cols, axis=1)
    groups_per_mod = mod_ref.shape[0]
    g0 = lax.rem(i, groups_per_mod // gpt) * gpt
    rows = []
    for s in range(gpt):
        sl = slice(s * GROUP, (s + 1) * GROUP)
        gate = mod_ref[pl.ds(g0 + s, 1), 2 * d:3 * d]
        xo = x_ref[sl, :] + gate * acc[sl, :]
        if n_prompt_tiles is not None:
            ms = jnp.mean(xo * xo, axis=-1, keepdims=True)
            xo = (xo * lax.rsqrt(ms + EPS)) * gf_ref[...]
        rows.append(xo)
    if n_prompt_tiles is None:
        for s in range(gpt):
            out_refs[0][s * GROUP:(s + 1) * GROUP, :] = rows[s]
    else:
        for o_ref, active in ((out_refs[0], i < n_prompt_tiles), (out_refs[1], i >= n_prompt_tiles)):
            @pl.when(active)
            def _(o_ref=o_ref):
                for s in range(gpt):
                    o_ref[s * GROUP:(s + 1) * GROUP, :] = rows[s]


def _combine(x, shared, wts, mod_g, g_final, y, *, tm, split_rows=None):
    t, d = x.shape
    tmc = min(256, tm)
    gpt = tmc // GROUP
    gp = tm // GROUP
    nt = t // tmc
    row_spec = pl.BlockSpec((tmc, d), lambda i: (i, 0))
    if split_rows is None:
        npt = None
        out_specs = row_spec
        out_shape = jax.ShapeDtypeStruct((t, d), _F32)
    else:
        npt = split_rows[0] // tmc
        out_specs = [pl.BlockSpec((tmc, d), lambda i: (jnp.minimum(i, npt - 1), 0)),
                     pl.BlockSpec((tmc, d), lambda i: (jnp.maximum(i - npt, 0), 0))]
        out_shape = [jax.ShapeDtypeStruct((rows, d), _F32) for rows in split_rows]
    kern = functools.partial(_combine_kernel, d=d, gpt=gpt, n_prompt_tiles=npt)
    ns = d // SLAB_COLS
    y_specs = [pl.BlockSpec((tmc * ns, 128), functools.partial(lambda i, k: (k * nt + i, 0), k=k))
               for k in range(TOP_K)]
    return pl.pallas_call(
        kern,
        grid=(nt,),
        scratch_shapes=[pltpu.VMEM((tmc * ns, 128), _F32), pltpu.VMEM((tmc * ns, 128), _F32)],
        in_specs=[row_spec,
                  row_spec,
                  pl.BlockSpec((tmc * ns, TOP_K), lambda i: (i, 0)),
                  pl.BlockSpec((gp, 3 * d), lambda i: (i // (tm // tmc), 1)),
                  pl.BlockSpec((1, d), lambda i: (0, 0))] + y_specs,
        out_specs=out_specs,
        out_shape=out_shape,
        compiler_params=_cparams(1),
        name="combine" if split_rows is None else "combine_final",
    )(x, shared, wts, mod_g, g_final.reshape(1, d), *([y] * TOP_K))


def _dispatch_tables(eidx, counts, *, bm, nb, t):
    n_exp = counts.shape[0]
    low_bits = (t * TOP_K).bit_length()
    low_mask = (1 << low_bits) - 1
    assert t * TOP_K < low_mask and (n_exp + 1) << low_bits < 2 ** 31
    tok_k = jnp.arange(t, dtype=jnp.int32)[None, :] * TOP_K + jnp.arange(TOP_K, dtype=jnp.int32)[:, None]
    real = (eidx << low_bits) + tok_k
    n_pad = (-counts) % bm
    e_col = jnp.arange(n_exp, dtype=jnp.int32)[:, None]
    pad = jnp.where(jnp.arange(bm, dtype=jnp.int32)[None, :] < n_pad[:, None],
                    (e_col << low_bits) + low_mask, (n_exp << low_bits) + low_mask)
    keys = jnp.sort(jnp.concatenate([real.reshape(-1), pad.reshape(-1)]))
    low = keys & low_mask
    is_pad = low == low_mask
    tok = low // TOP_K
    r = jnp.arange(nb * bm, dtype=jnp.int32)
    row_src = jnp.where(is_pad, r % t, tok)
    row_dst = jnp.where(is_pad, TOP_K * t + ((r // bm) % 2) * bm + (r % bm),
                        (low % TOP_K) * t + tok)
    idx = jnp.concatenate([row_src.reshape(nb, bm), row_dst.reshape(nb, bm)], axis=1)
    before = jnp.concatenate([jnp.zeros((bm,), jnp.int32), TOP_K * t + bm + jnp.arange(bm, dtype=jnp.int32)])
    idx = jnp.concatenate([idx, jnp.zeros((1, 2 * bm), jnp.int32), before[None, :]], axis=0)
    blk_e = jnp.minimum(keys.reshape(nb, bm)[:, 0] >> low_bits, n_exp - 1)
    later = jnp.where(blk_e[None, :] > blk_e[:, None], blk_e[None, :], n_exp)
    next_e = jnp.min(later, axis=1)
    next_e = jnp.where(next_e == n_exp, blk_e, next_e).astype(jnp.int32)
    n_used = (jnp.sum(counts + n_pad) // bm).astype(jnp.int32).reshape(1)
    return blk_e, next_e, n_used, idx


def _moe(x, mod_g, g_ffn, g_final, router_w, router_b, wg, wu, wd, swg, swu, swd,
         *, layer, tm, bm, split_rows=None):
    t, d = x.shape
    n_exp = router_w.shape[1]
    h_slabs, shared, eidx, wts, cnt = _router(
        x, mod_g, g_ffn, router_w.T.astype(_BF16), router_b,
        swg.astype(_BF16), swu.astype(_BF16), swd.astype(_BF16), tm=tm)
    nb = (t * TOP_K) // bm + n_exp
    blk_e, next_e, n_used, idx = _dispatch_tables(eidx, cnt[:, 0], bm=bm, nb=nb, t=t)
    y = _routed_ffn(blk_e, next_e, n_used, idx, h_slabs, wg, wu, wd, layer=layer, bm=bm,
                    n_out_rows=TOP_K * t + 2 * bm)
    w_rows = jnp.repeat(wts.T, d // SLAB_COLS, axis=0)
    return _combine(x, shared, w_rows, mod_g, g_final, y, tm=tm, split_rows=split_rows)


def _tile_rows(seq, dec_rows):
    for tm in (512, 256, 128, 64):
        if seq % tm == 0 and dec_rows % tm == 0:
            return tm
    raise ValueError("sequence lengths must be multiples of 64")


def kernel(x_prompt, x_sample, state_conv, state_pool, c_prompt, c_sample, w_mod, b_mod, g_mix, g_ffn, g_final, conv_w_in, conv_w, conv_w_out, pool_w, pool_scale, router_w, router_b, exp_w_gate, exp_w_up, exp_w_down, sh_w_gate, sh_w_up, sh_w_down):
    bp, seq, d = x_prompt.shape
    bs, dseq, _ = x_sample.shape
    depth = w_mod.shape[0]
    n_exp = router_w.shape[2]
    assert dseq == GROUP and seq % GROUP == 0 and depth == 2
    assert state_conv.shape[1] == CONV_W - 1 and state_pool.shape[1] == POOL_HDR - 1
    assert n_exp % N_EXPERT_GROUPS == 0 and d % (128 * len(POOL_WINDOWS)) == 0
    tp, ts = bp * seq, bs * dseq
    t = tp + ts
    tm = _tile_rows(seq, ts)
    bm = min(512, t * TOP_K // n_exp)
    assert (t * TOP_K) % bm == 0
    n_prompt_tiles, tiles_per_seq = tp // tm, seq // tm
    ng_p, ng = tp // GROUP, t // GROUP

    nb_pad = -(-(bp + bs) // 8) * 8
    c_all = jnp.concatenate([c_prompt, c_sample, jnp.zeros((nb_pad - bp - bs, d), _F32)], axis=0)
    mods = _mod_rows(c_all, w_mod, b_mod)
    gps = seq // GROUP
    mod_p = jnp.broadcast_to(mods[:, :bp, None, :], (depth, bp, gps, 6 * d)).reshape(depth, bp * gps, 6 * d)
    mod_g = jnp.concatenate([mod_p, mods[:, bp:bp + bs, :]], axis=1)

    conv_tbl = jnp.pad(state_conv, ((0, 0), (CONV_HDR - (CONV_W - 1), 0), (0, 0)))
    pool_tbl = jnp.pad(state_pool, ((0, 0), (1, 0), (0, 0)))

    statics = dict(tm=tm, n_prompt_tiles=n_prompt_tiles, tiles_per_seq=tiles_per_seq)
    x, conv_tail = _conv_mixer(x_prompt.reshape(tp, d), x_sample.reshape(ts, d), mod_g[0], conv_tbl, g_mix[0],
                               conv_w_in.astype(_BF16), conv_w, conv_w_out.astype(_BF16), **statics)
    x = _moe(x, mod_g[0], g_ffn[0], g_final, router_w[0], router_b[0], exp_w_gate, exp_w_up, exp_w_down,
             sh_w_gate[0], sh_w_up[0], sh_w_down[0], layer=0, tm=tm, bm=bm)
    x, pool_tail = _pool_mixer(x, mod_g[1], pool_tbl, g_mix[1], pool_w.astype(_BF16), pool_scale, **statics)
    y_p, y_s = _moe(x, mod_g[1], g_ffn[1], g_final, router_w[1], router_b[1], exp_w_gate, exp_w_up, exp_w_down,
                    sh_w_gate[1], sh_w_up[1], sh_w_down[1], layer=1, tm=tm, bm=bm, split_rows=(tp, ts))

    return (y_p.reshape(bp, seq, d),
            y_s.reshape(bs, dseq, d),
            conv_tail[gps - 1:ng_p:gps],
            conv_tail[ng_p:],
            pool_tail[gps - 1:ng_p:gps],
            pool_tail[ng_p:])
```

```python
import functools

import jax
import jax.numpy as jnp
from jax import lax
from jax.experimental import pallas as pl
from jax.experimental.pallas import tpu as pltpu

GROUP = 64
CONV_W = 3
POOL_WINDOWS = (2, 4, 8, 16)
POOL_HDR = 16
CONV_HDR = 8
TOP_K = 8
N_EXPERT_GROUPS = 8
TOPK_GROUPS = 4
ROUTED_SCALE = 2.5
PAST_LEN = 4096
EPS = 1e-6
VMEM_LIMIT_V7X = 56 * 1024 * 1024
NEG_INF = float("-inf")

_F32 = jnp.float32
_BF16 = jnp.bfloat16


def _silu(v):
    return v * jax.nn.sigmoid(v)


_U32 = jnp.uint32
_HI_MASK = 0xFFFF0000
SLAB_COLS = 256
IDX_SLOTS = 8
FFN_BLOCK_ROWS = 512
COMBINE_ROWS = 256


def _pack_pair(lo, hi):
    lo_b = lax.bitcast_convert_type(lo.astype(_BF16).astype(_F32), _U32)
    hi_b = lax.bitcast_convert_type(hi.astype(_BF16).astype(_F32), _U32)
    return hi_b | (lo_b >> 16)


def _unpack_pair(p):
    lo = lax.bitcast_convert_type(p << 16, _F32)
    hi = lax.bitcast_convert_type(p & _U32(_HI_MASK), _F32)
    return lo, hi


def _store_slab_cols(ref, row0, j, v, n_rows, ns):
    ref[pl.ds(row0 * ns + j, n_rows, stride=ns), :] = _pack_pair(v[:, :128], v[:, 128:])


def _load_slab_cols(ref, row0, j, n_rows, ns):
    return _unpack_pair(ref[pl.ds(row0 * ns + j, n_rows, stride=ns), :])


def _cparams(n_axes):
    return pltpu.CompilerParams(dimension_semantics=("arbitrary",) * n_axes,
                                vmem_limit_bytes=VMEM_LIMIT_V7X)


def _mod_kernel(c_ref, w_ref, b_ref, o_ref):
    c = c_ref[...]
    ca = _silu(c).astype(_BF16)
    o_ref[...] = jnp.dot(ca, w_ref[...].astype(_BF16), preferred_element_type=_F32) + b_ref[...]


def _mod_rows(c_all, w_mod, b_mod):
    depth, d, n6 = w_mod.shape
    nb = c_all.shape[0]
    tn = min(1024, n6)
    return pl.pallas_call(
        _mod_kernel,
        grid=(depth, n6 // tn),
        in_specs=[pl.BlockSpec((nb, d), lambda l, j: (0, 0)),
                  pl.BlockSpec((None, d, tn), lambda l, j: (l, 0, j)),
                  pl.BlockSpec((None, 1, tn), lambda l, j: (l, 0, j))],
        out_specs=pl.BlockSpec((None, nb, tn), lambda l, j: (l, 0, j)),
        out_shape=jax.ShapeDtypeStruct((depth, nb, n6), _F32),
        compiler_params=_cparams(2),
        name="adaln_rows",
    )(c_all, w_mod, b_mod.reshape(depth, 1, n6))


def _norm_mod_group(xs, mod_ref, g_ref, s, d):
    ms = jnp.mean(xs * xs, axis=-1, keepdims=True)
    xn = (xs * lax.rsqrt(ms + EPS)) * g_ref[...]
    return xn * (1.0 + mod_ref[s:s + 1, d:2 * d]) + mod_ref[s:s + 1, 0:d]


def _conv_mixer_kernel(xp_ref, xs_ref, mod_ref, st_ref, g_ref, wb_ref, wc_ref, wv_ref, cw_ref, wo_ref,
                       out_ref, tail_ref, h_scr, acc_scr, ubuf, gbuf, carry,
                       *, gp, d, n_prompt_tiles, tiles_per_seq):
    i = pl.program_id(0)
    j = pl.program_id(1)
    nj = pl.num_programs(1)
    is_sample = i >= n_prompt_tiles

    def x_rows(s):
        sl = slice(s * GROUP, (s + 1) * GROUP)
        return jnp.where(is_sample, xs_ref[sl, :], xp_ref[sl, :])

    @pl.when(j == 0)
    def _():
        for s in range(gp):
            h_scr[s * GROUP:(s + 1) * GROUP, :] = _norm_mod_group(x_rows(s), mod_ref, g_ref, s, d).astype(_BF16)
        acc_scr[...] = jnp.zeros_like(acc_scr)

    @pl.when(i == 0)
    def _():
        carry[j] = jnp.zeros((CONV_HDR, carry.shape[2]), _F32)

    hb = h_scr[...]
    bq = jnp.dot(hb, wb_ref[...], preferred_element_type=_F32)
    cq = jnp.dot(hb, wc_ref[...], preferred_element_type=_F32)
    vq = jnp.dot(hb, wv_ref[...], preferred_element_type=_F32)
    u = cq * vq

    stride = GROUP + CONV_HDR
    for s in range(gp):
        if s == 0:
            prev = jnp.where(lax.rem(i, tiles_per_seq) == 0, 0.0, carry[j])
        else:
            prev = u[s * GROUP - CONV_HDR:s * GROUP, :]
        ubuf[s * stride:s * stride + CONV_HDR, :] = jnp.where(is_sample, st_ref[s], prev)
        ubuf[s * stride + CONV_HDR:(s + 1) * stride, :] = u[s * GROUP:(s + 1) * GROUP, :]
        tail_ref[s] = u[(s + 1) * GROUP - (CONV_W - 1):(s + 1) * GROUP, :]
    carry[j] = u[gp * GROUP - CONV_HDR:gp * GROUP, :]

    w0 = cw_ref[0:1, :]
    w1 = cw_ref[1:2, :]
    w2 = cw_ref[2:3, :]
    for s in range(gp):
        base = s * stride + CONV_HDR
        u0 = ubuf[base:base + GROUP, :]
        u1 = ubuf[base - 1:base - 1 + GROUP, :]
        u2 = ubuf[base - 2:base - 2 + GROUP, :]
        conv = u2 * w0 + u1 * w1 + u0 * w2
        gbuf[s * GROUP:(s + 1) * GROUP, :] = (bq[s * GROUP:(s + 1) * GROUP, :] * conv).astype(_BF16)
    acc_scr[...] += jnp.dot(gbuf[...], wo_ref[...], preferred_element_type=_F32)

    @pl.when(j == nj - 1)
    def _():
        for s in range(gp):
            sl = slice(s * GROUP, (s + 1) * GROUP)
            out_ref[sl, :] = x_rows(s) + mod_ref[s:s + 1, 2 * d:3 * d] * acc_scr[sl, :]


def _conv_mixer(xp, xs, mod_g, st_tbl, g_mix, w_in_bf, conv_w, w_out_bf, *, tm, n_prompt_tiles, tiles_per_seq):
    d = xp.shape[1]
    t = xp.shape[0] + xs.shape[0]
    npt = n_prompt_tiles
    gp = tm // GROUP
    tn = min(512, d)
    nj = d // tn
    kern = functools.partial(_conv_mixer_kernel, gp=gp, d=d, n_prompt_tiles=n_prompt_tiles,
                             tiles_per_seq=tiles_per_seq)
    return pl.pallas_call(
        kern,
        grid=(t // tm, nj),
        in_specs=[pl.BlockSpec((tm, d), lambda i, j: (jnp.minimum(i, npt - 1), 0)),
                  pl.BlockSpec((tm, d), lambda i, j: (jnp.maximum(i - npt, 0), 0)),
                  pl.BlockSpec((gp, 3 * d), lambda i, j: (i, 0)),
                  pl.BlockSpec((gp, CONV_HDR, tn), lambda i, j: (jnp.maximum(i - npt, 0), 0, j)),
                  pl.BlockSpec((1, d), lambda i, j: (0, 0)),
                  pl.BlockSpec((d, tn), lambda i, j: (0, j)),
                  pl.BlockSpec((d, tn), lambda i, j: (0, nj + j)),
                  pl.BlockSpec((d, tn), lambda i, j: (0, 2 * nj + j)),
                  pl.BlockSpec((CONV_W, tn), lambda i, j: (0, j)),
                  pl.BlockSpec((tn, d), lambda i, j: (j, 0))],
        out_specs=[pl.BlockSpec((tm, d), lambda i, j: (i, 0)),
                   pl.BlockSpec((gp, CONV_W - 1, tn), lambda i, j: (i, 0, j))],
        out_shape=[jax.ShapeDtypeStruct((t, d), _F32),
                   jax.ShapeDtypeStruct((t // GROUP, CONV_W - 1, d), _F32)],
        scratch_shapes=[pltpu.VMEM((tm, d), _BF16),
                        pltpu.VMEM((tm, d), _F32),
                        pltpu.VMEM((gp * (GROUP + CONV_HDR), tn), _F32),
                        pltpu.VMEM((tm, tn), _BF16),
                        pltpu.VMEM((nj, CONV_HDR, tn), _F32)],
        compiler_params=_cparams(2),
        name="conv_mixer",
    )(xp, xs, mod_g, st_tbl, g_mix.reshape(1, d), w_in_bf, w_in_bf, w_in_bf, conv_w, w_out_bf)


def _pool_mixer_kernel(x_ref, mod_ref, st_ref, g_ref, pw_ref, ps_ref, out_ref, tail_ref,
                       dbuf, carry, *, gp, d, n_prompt_tiles, tiles_per_seq, tm):
    i = pl.program_id(0)
    pg = d // len(POOL_WINDOWS)
    is_sample = i >= n_prompt_tiles
    row = lax.broadcasted_iota(jnp.int32, (GROUP, 1), 0)

    @pl.when(i == 0)
    def _():
        carry[...] = jnp.zeros_like(carry)

    prev = jnp.where(lax.rem(i, tiles_per_seq) == 0, 0.0, carry[...])
    for s in range(gp):
        h = _norm_mod_group(x_ref[s * GROUP:(s + 1) * GROUP, :], mod_ref, g_ref, s, d)
        hdr = jnp.where(is_sample, st_ref[s], prev)
        he = jnp.concatenate([hdr, h], axis=0)
        pos0 = jnp.where(is_sample, PAST_LEN, lax.rem(i, tiles_per_seq) * tm + s * GROUP)
        pos1 = pos0 + row + 1
        for q, w in enumerate(POOL_WINDOWS):
            cs = slice(q * pg, (q + 1) * pg)
            sw = he[:, cs]
            span = 1
            while span < w:
                sw = sw + pltpu.roll(sw, span, 0)
                span *= 2
            inv_cnt = 1.0 / jnp.minimum(pos1, w).astype(_F32)
            pooled = sw[POOL_HDR:, :] * inv_cnt
            dbuf[s * GROUP:(s + 1) * GROUP, cs] = (pooled - h[:, cs]).astype(_BF16)
        prev = h[GROUP - POOL_HDR:, :]
        tail_ref[s] = h[GROUP - (POOL_HDR - 1):, :]
    carry[...] = prev
    for q in range(len(POOL_WINDOWS)):
        cs = slice(q * pg, (q + 1) * pg)
        y = jnp.dot(dbuf[:, cs], pw_ref[q], preferred_element_type=_F32) * ps_ref[:, cs]
        for s in range(gp):
            sl = slice(s * GROUP, (s + 1) * GROUP)
            out_ref[sl, cs] = x_ref[sl, cs] + mod_ref[s:s + 1, 2 * d + q * pg:2 * d + (q + 1) * pg] * y[sl, :]


def _pool_mixer(x, mod_g, st_tbl, g_mix, pool_w_bf, pool_scale, *, tm, n_prompt_tiles, tiles_per_seq):
    t, d = x.shape
    gp = tm // GROUP
    nq = len(POOL_WINDOWS)
    pg = d // nq
    kern = functools.partial(_pool_mixer_kernel, gp=gp, d=d, n_prompt_tiles=n_prompt_tiles,
                             tiles_per_seq=tiles_per_seq, tm=tm)
    return pl.pallas_call(
        kern,
        grid=(t // tm,),
        in_specs=[pl.BlockSpec((tm, d), lambda i: (i, 0)),
                  pl.BlockSpec((gp, 3 * d), lambda i: (i, 0)),
                  pl.BlockSpec((gp, POOL_HDR, d), lambda i: (jnp.maximum(i - n_prompt_tiles, 0), 0, 0)),
                  pl.BlockSpec((1, d), lambda i: (0, 0)),
                  pl.BlockSpec((nq, pg, pg), lambda i: (0, 0, 0)),
                  pl.BlockSpec((1, d), lambda i: (0, 0))],
        out_specs=[pl.BlockSpec((tm, d), lambda i: (i, 0)),
                   pl.BlockSpec((gp, POOL_HDR - 1, d), lambda i: (i, 0, 0))],
        out_shape=[jax.ShapeDtypeStruct((t, d), _F32),
                   jax.ShapeDtypeStruct((t // GROUP, POOL_HDR - 1, d), _F32)],
        scratch_shapes=[pltpu.VMEM((tm, d), _BF16),
                        pltpu.VMEM((POOL_HDR, d), _F32)],
        compiler_params=_cparams(1),
        name="pool_mixer",
    )(x, mod_g, st_tbl, g_mix.reshape(1, d), pool_w_bf, pool_scale.reshape(1, d))


def _router_kernel(x_ref, mod_ref, g_ref, wr_ref, br_ref, swg_ref, swu_ref, swd_ref,
                   hs_ref, sh_ref, eidx_ref, wts_ref, cnt_ref, run_scr, hb_scr,
                   *, gp, d, n_exp, tm):
    i = pl.program_id(0)
    per_group = n_exp // N_EXPERT_GROUPS
    ns = d // SLAB_COLS

    @pl.when(i == 0)
    def _():
        run_scr[...] = jnp.zeros_like(run_scr)

    for s in range(gp):
        sl = slice(s * GROUP, (s + 1) * GROUP)
        h = _norm_mod_group(x_ref[sl, :], mod_ref, g_ref, s, d)
        hb_scr[sl, :] = h.astype(_BF16)
        for j in range(ns):
            _store_slab_cols(hs_ref, s * GROUP, j, h[:, SLAB_COLS * j:SLAB_COLS * (j + 1)], GROUP, ns)

    logits = lax.dot_general(wr_ref[...], hb_scr[...], (((1,), (1,)), ((), ())),
                             preferred_element_type=_F32)
    sub = lax.broadcasted_iota(jnp.int32, (per_group, tm), 0)
    scores, biased, eids = [], [], []
    for g in range(N_EXPERT_GROUPS):
        rs = slice(g * per_group, (g + 1) * per_group)
        sc = jax.nn.sigmoid(logits[rs, :])
        scores.append(sc)
        biased.append(sc + br_ref[rs, :])
        eids.append(sub + g * per_group)

    gscore = []
    for g in range(N_EXPERT_GROUPS):
        m1 = jnp.max(biased[g], axis=0, keepdims=True)
        i1 = jnp.min(jnp.where(biased[g] == m1, sub, per_group), axis=0, keepdims=True)
        m2 = jnp.max(jnp.where(sub == i1, NEG_INF, biased[g]), axis=0, keepdims=True)
        gscore.append(m1 + m2)
    vals = []
    for g in range(N_EXPERT_GROUPS):
        beaten = jnp.zeros((1, tm), jnp.int32)
        for g2 in range(N_EXPERT_GROUPS):
            if g2 == g:
                continue
            beat = (gscore[g2] >= gscore[g]) if g2 < g else (gscore[g2] > gscore[g])
            beaten = beaten + beat.astype(jnp.int32)
        vals.append(jnp.where(beaten < TOPK_GROUPS, biased[g], NEG_INF))

    sel = [jnp.zeros((per_group, tm), _F32) for _ in range(N_EXPERT_GROUPS)]
    idxs, picked = [], []
    for _ in range(TOP_K):
        mx = vals[0]
        for g in range(1, N_EXPERT_GROUPS):
            mx = jnp.maximum(mx, vals[g])
        m = jnp.max(mx, axis=0, keepdims=True)
        ci = jnp.where(vals[0] == m, eids[0], n_exp)
        for g in range(1, N_EXPERT_GROUPS):
            ci = jnp.minimum(ci, jnp.where(vals[g] == m, eids[g], n_exp))
        idx = jnp.min(ci, axis=0, keepdims=True)
        sc_k = jnp.zeros((1, tm), _F32)
        for g in range(N_EXPERT_GROUPS):
            oh = eids[g] == idx
            sc_k = sc_k + jnp.sum(jnp.where(oh, scores[g], 0.0), axis=0, keepdims=True)
            vals[g] = jnp.where(oh, NEG_INF, vals[g])
            sel[g] = jnp.where(oh, 1.0, sel[g])
        idxs.append(idx)
        picked.append(sc_k)
    den = picked[0]
    for k in range(1, TOP_K):
        den = den + picked[k]

    for k in range(TOP_K):
        eidx_ref[k:k + 1, :] = idxs[k]
        wts_ref[k:k + 1, :] = picked[k] / den * ROUTED_SCALE
    sel_all = jnp.concatenate(sel, axis=0)
    run_scr[...] = run_scr[...] + jnp.sum(sel_all, axis=1, keepdims=True)
    cnt_ref[...] = run_scr[...].astype(jnp.int32)

    hb = hb_scr[...]
    sg = jnp.dot(hb, swg_ref[...], preferred_element_type=_F32)
    su = jnp.dot(hb, swu_ref[...], preferred_element_type=_F32)
    sh_ref[...] = jnp.dot((_silu(sg) * su).astype(_BF16), swd_ref[...],
                          preferred_element_type=_F32).astype(_BF16)


def _router(x, mod_g, g_ffn, wr_t, br, swg_bf, swu_bf, swd_bf, *, tm):
    t, d = x.shape
    gp = tm // GROUP
    n_exp = wr_t.shape[0]
    f = swg_bf.shape[1]
    ns = d // SLAB_COLS
    kern = functools.partial(_router_kernel, gp=gp, d=d, n_exp=n_exp, tm=tm)
    const = lambda i: (0, 0)
    return pl.pallas_call(
        kern,
        grid=(t // tm,),
        in_specs=[pl.BlockSpec((tm, d), lambda i: (i, 0)),
                  pl.BlockSpec((gp, 3 * d), lambda i: (i, 1)),
                  pl.BlockSpec((1, d), const),
                  pl.BlockSpec((n_exp, d), const),
                  pl.BlockSpec((n_exp, 1), const),
                  pl.BlockSpec((d, f), const),
                  pl.BlockSpec((d, f), const),
                  pl.BlockSpec((f, d), const)],
        out_specs=[pl.BlockSpec((tm * ns, 128), lambda i: (i, 0)),
                   pl.BlockSpec((tm, d), lambda i: (i, 0)),
                   pl.BlockSpec((TOP_K, tm), lambda i: (0, i)),
                   pl.BlockSpec((TOP_K, tm), lambda i: (0, i)),
                   pl.BlockSpec((n_exp, 128), const)],
        out_shape=[jax.ShapeDtypeStruct((t * ns, 128), _U32),
                   jax.ShapeDtypeStruct((t, d), _BF16),
                   jax.ShapeDtypeStruct((TOP_K, t), jnp.int32),
                   jax.ShapeDtypeStruct((TOP_K, t), _F32),
                   jax.ShapeDtypeStruct((n_exp, 128), jnp.int32)],
        scratch_shapes=[pltpu.VMEM((n_exp, 128), _F32),
                        pltpu.VMEM((tm, d), _BF16)],
        compiler_params=_cparams(1),
        name="router_shared",
    )(x, mod_g, g_ffn.reshape(1, d), wr_t, br.reshape(n_exp, 1), swg_bf, swu_bf, swd_bf)


def _ffn_pair_kernel(be_ref, ne_ref, nu_ref, idx_hbm, h_hbm, wg_hbm, wu_hbm, wd_hbm, y_hbm,
                     idx_s, xflat, yflat, xmat, hbuf, wg32, wu32, wd32, wgb, wub, wdb,
                     sem_i, sem_g, sem_s, sem_w, *, bm, ns, layer):
    g = pl.program_id(0)
    nu = nu_ref[0]
    nb = 2 * pl.num_programs(0)
    rows = bm * ns
    f = wgb.shape[1]
    tbl = 2 * bm

    def idx_off(blk):
        return pl.multiple_of(lax.rem(blk, IDX_SLOTS) * tbl, tbl)

    def idx_copy(blk, sl=None):
        sl = lax.rem(blk, IDX_SLOTS) if sl is None else sl
        return pltpu.make_async_copy(idx_hbm.at[blk], idx_s.at[pl.ds(pl.multiple_of(sl * tbl, tbl), tbl)],
                                     sem_i.at[sl])

    def weight_copies(e):
        return [pltpu.make_async_copy(src.at[layer, e], dst, sem_w.at[0])
                for src, dst in ((wg_hbm, wg32), (wu_hbm, wu32), (wd_hbm, wd32))]

    def row_gather(half, base, r):
        src = pl.multiple_of(idx_s[base + r] * ns, ns)
        return pltpu.make_async_copy(h_hbm.at[pl.ds(src, ns)], xflat.at[pl.ds((half * bm + r) * ns, ns)],
                                     sem_g.at[half])

    def row_scatter(half, base, r):
        dst = pl.multiple_of(idx_s[base + bm + r] * ns, ns)
        return pltpu.make_async_copy(yflat.at[pl.ds((half * bm + r) * ns, ns)], y_hbm.at[pl.ds(dst, ns)],
                                     sem_s.at[half])

    def block_gather(half):
        return pltpu.make_async_copy(h_hbm.at[pl.ds(0, rows)], xflat.at[pl.ds(half * rows, rows)],
                                     sem_g.at[half])

    def block_scatter(half, dst0=0):
        return pltpu.make_async_copy(yflat.at[pl.ds(half * rows, rows)], y_hbm.at[pl.ds(dst0, rows)],
                                     sem_s.at[half])

    def block(blk, half):
        other = 1 - half
        if half == 0:
            @pl.when(blk == 0)
            def _():
                for cp in weight_copies(be_ref[0]):
                    cp.start()
                yflat[...] = jnp.zeros_like(yflat)
                block_scatter(0, y_hbm.shape[0] - 2 * rows).start()
                tables = [idx_copy(nb + 1, IDX_SLOTS - 1), idx_copy(0), idx_copy(1), idx_copy(2)]
                for cp in tables:
                    cp.start()
                for cp in tables[:2]:
                    cp.wait()
                for r in range(bm):
                    row_gather(0, 0, r).start()

        new_expert = (blk == 0) | (be_ref[blk] != be_ref[jnp.maximum(blk - 1, 0)])

        @pl.when(new_expert)
        def _():
            for cp in weight_copies(be_ref[blk]):
                cp.wait()
            wgb[...] = wg32[...].astype(_BF16)
            wub[...] = wu32[...].astype(_BF16)
            wdb[...] = wd32[...].astype(_BF16)
            for cp in weight_copies(ne_ref[blk]):
                cp.start()

        base_next = idx_off(blk + 1)
        base_prev = idx_off(blk + IDX_SLOTS - 1)
        idx_copy(blk + 3).start()
        block_gather(half).wait()
        idx_copy(blk + 1).wait()

        per_piece = bm // ns
        for j in range(ns):
            lo, hi = _load_slab_cols(xflat, half * bm, j, bm, ns)
            xmat[:, SLAB_COLS * j:SLAB_COLS * j + 128] = lo.astype(_BF16)
            xmat[:, SLAB_COLS * j + 128:SLAB_COLS * (j + 1)] = hi.astype(_BF16)
            for r in range(j * per_piece, (j + 1) * per_piece):
                row_gather(other, base_next, r).start()
                row_scatter(other, base_prev, r).start()

        x = xmat[...]
        n_fc = f // SLAB_COLS
        for c in range(n_fc):
            cs = slice(SLAB_COLS * c, SLAB_COLS * (c + 1))
            hg = jnp.dot(x, wgb[:, cs], preferred_element_type=_F32)
            hu = jnp.dot(x, wub[:, cs], preferred_element_type=_F32)
            hbuf[:, cs] = (_silu(hg) * hu).astype(_BF16)

        block_scatter(half).wait()
        hb = hbuf[...]
        for j in range(ns):
            o = jnp.dot(hb, wdb[:, SLAB_COLS * j:SLAB_COLS * (j + 1)], preferred_element_type=_F32)
            _store_slab_cols(yflat, half * bm, j, o, bm, ns)

        @pl.when(blk == nu - 1)
        def _():
            base_own = idx_off(blk)
            for r in range(bm):
                row_scatter(half, base_own, r).start()
            block_gather(other).wait()
            idx_copy(blk + 2).wait()
            idx_copy(blk + 3).wait()
            block_scatter(other).wait()
            block_scatter(half).wait()
            for cp in weight_copies(0):
                cp.wait()

    for half in range(2):
        blk = 2 * g + half
        pl.when(blk < nu)(functools.partial(block, blk, half))


def _routed_ffn(blk_e, next_e, n_used, idx, h_slabs, wg, wu, wd, *, layer, bm, n_out_rows):
    nb = idx.shape[0] - 2
    _, _, d, f = wg.shape
    ns = d // SLAB_COLS
    assert f % SLAB_COLS == 0 and bm % ns == 0 and nb % 2 == 0
    kern = functools.partial(_ffn_pair_kernel, bm=bm, ns=ns, layer=layer)
    hbm = pl.BlockSpec(memory_space=pl.ANY)
    grid_spec = pltpu.PrefetchScalarGridSpec(
        num_scalar_prefetch=3,
        grid=(nb // 2,),
        in_specs=[hbm, hbm, hbm, hbm, hbm],
        out_specs=hbm,
        scratch_shapes=[pltpu.SMEM((IDX_SLOTS * 2 * bm,), jnp.int32),
                        pltpu.VMEM((2 * bm * ns, 128), _U32),
                        pltpu.VMEM((2 * bm * ns, 128), _U32),
                        pltpu.VMEM((bm, d), _BF16),
                        pltpu.VMEM((bm, f), _BF16),
                        pltpu.VMEM((d, f), _F32),
                        pltpu.VMEM((d, f), _F32),
                        pltpu.VMEM((f, d), _F32),
                        pltpu.VMEM((d, f), _BF16),
                        pltpu.VMEM((d, f), _BF16),
                        pltpu.VMEM((f, d), _BF16),
                        pltpu.SemaphoreType.DMA((IDX_SLOTS,)),
                        pltpu.SemaphoreType.DMA((2,)),
                        pltpu.SemaphoreType.DMA((2,)),
                        pltpu.SemaphoreType.DMA((1,))])
    return pl.pallas_call(
        kern,
        grid_spec=grid_spec,
        out_shape=jax.ShapeDtypeStruct((n_out_rows * ns, 128), _U32),
        compiler_params=_cparams(1),
        name="routed_ffn",
    )(blk_e, next_e, n_used, idx, h_slabs, wg, wu, wd)


def _combine_kernel(*refs, d, gpt, n_prompt_tiles):
    x_ref, sh_ref, w_ref, mod_ref, gf_ref = refs[:5]
    y_refs = refs[5:5 + TOP_K]
    n_out = 1 if n_prompt_tiles is None else 2
    out_refs = refs[5 + TOP_K:5 + TOP_K + n_out]
    lo_scr, hi_scr = refs[5 + TOP_K + n_out:]
    i = pl.program_id(0)
    ns = d // SLAB_COLS
    tmc = x_ref.shape[0]
    acc_lo = jnp.zeros(lo_scr.shape, _F32)
    acc_hi = jnp.zeros(hi_scr.shape, _F32)
    for k in range(TOP_K):
        lo, hi = _unpack_pair(y_refs[k][...])
        wk = w_ref[:, k:k + 1]
        acc_lo = acc_lo + wk * lo
        acc_hi = acc_hi + wk * hi
    lo_scr[...] = acc_lo
    hi_scr[...] = acc_hi
    cols = []
    for j in range(ns):
        cols += [lo_scr[pl.ds(j, tmc, stride=ns), :], hi_scr[pl.ds(j, tmc, stride=ns), :]]
    acc = sh_ref[...].astype(_F32) + jnp.concatenate(cols, axis=1)
    groups_per_mod = mod_ref.shape[0]
    g0 = lax.rem(i, groups_per_mod // gpt) * gpt
    rows = []
    for s in range(gpt):
        sl = slice(s * GROUP, (s + 1) * GROUP)
        gate = mod_ref[pl.ds(g0 + s, 1), 2 * d:3 * d]
        xo = x_ref[sl, :] + gate * acc[sl, :]
        if n_prompt_tiles is not None:
            ms = jnp.mean(xo * xo, axis=-1, keepdims=True)
            xo = (xo * lax.rsqrt(ms + EPS)) * gf_ref[...]
        rows.append(xo)
    if n_prompt_tiles is None:
        for s in range(gpt):
            out_refs[0][s * GROUP:(s + 1) * GROUP, :] = rows[s]
    else:
        for o_ref, active in ((out_refs[0], i < n_prompt_tiles), (out_refs[1], i >= n_prompt_tiles)):
            @pl.when(active)
            def _(o_ref=o_ref):
                for s in range(gpt):
                    o_ref[s * GROUP:(s + 1) * GROUP, :] = rows[s]


def _combine(x, shared, wts, mod_g, g_final, y, *, tm, split_rows=None):
    t, d = x.shape
    tmc = min(COMBINE_ROWS, tm)
    gpt = tmc // GROUP
    gp = tm // GROUP
    nt = t // tmc
    row_spec = pl.BlockSpec((tmc, d), lambda i: (i, 0))
    if split_rows is None:
        npt = None
        out_specs = row_spec
        out_shape = jax.ShapeDtypeStruct((t, d), _F32)
    else:
        npt = split_rows[0] // tmc
        out_specs = [pl.BlockSpec((tmc, d), lambda i: (jnp.minimum(i, npt - 1), 0)),
                     pl.BlockSpec((tmc, d), lambda i: (jnp.maximum(i - npt, 0), 0))]
        out_shape = [jax.ShapeDtypeStruct((rows, d), _F32) for rows in split_rows]
    kern = functools.partial(_combine_kernel, d=d, gpt=gpt, n_prompt_tiles=npt)
    ns = d // SLAB_COLS
    y_specs = [pl.BlockSpec((tmc * ns, 128), functools.partial(lambda i, k: (k * nt + i, 0), k=k))
               for k in range(TOP_K)]
    return pl.pallas_call(
        kern,
        grid=(nt,),
        scratch_shapes=[pltpu.VMEM((tmc * ns, 128), _F32), pltpu.VMEM((tmc * ns, 128), _F32)],
        in_specs=[row_spec,
                  row_spec,
                  pl.BlockSpec((tmc * ns, TOP_K), lambda i: (i, 0)),
                  pl.BlockSpec((gp, 3 * d), lambda i: (i // (tm // tmc), 1)),
                  pl.BlockSpec((1, d), lambda i: (0, 0))] + y_specs,
        out_specs=out_specs,
        out_shape=out_shape,
        compiler_params=_cparams(1),
        name="combine" if split_rows is None else "combine_final",
    )(x, shared, wts, mod_g, g_final.reshape(1, d), *([y] * TOP_K))


def _dispatch_tables(eidx, counts, *, bm, nb, t):
    n_exp = counts.shape[0]
    low_bits = (t * TOP_K).bit_length()
    low_mask = (1 << low_bits) - 1
    assert t * TOP_K < low_mask and (n_exp + 1) << low_bits < 2 ** 31
    tok_k = jnp.arange(t, dtype=jnp.int32)[None, :] * TOP_K + jnp.arange(TOP_K, dtype=jnp.int32)[:, None]
    real = (eidx << low_bits) + tok_k
    n_pad = (-counts) % bm
    e_col = jnp.arange(n_exp, dtype=jnp.int32)[:, None]
    pad = jnp.where(jnp.arange(bm, dtype=jnp.int32)[None, :] < n_pad[:, None],
                    (e_col << low_bits) + low_mask, (n_exp << low_bits) + low_mask)
    keys = jnp.sort(jnp.concatenate([real.reshape(-1), pad.reshape(-1)]))
    low = keys & low_mask
    is_pad = low == low_mask
    tok = low // TOP_K
    r = jnp.arange(nb * bm, dtype=jnp.int32)
    row_src = jnp.where(is_pad, r % t, tok)
    row_dst = jnp.where(is_pad, TOP_K * t + ((r // bm) % 2) * bm + (r % bm),
                        (low % TOP_K) * t + tok)
    idx = jnp.concatenate([row_src.reshape(nb, bm), row_dst.reshape(nb, bm)], axis=1)
    before = jnp.concatenate([jnp.zeros((bm,), jnp.int32), TOP_K * t + bm + jnp.arange(bm, dtype=jnp.int32)])
    idx = jnp.concatenate([idx, jnp.zeros((1, 2 * bm), jnp.int32), before[None, :]], axis=0)
    blk_e = jnp.minimum(keys.reshape(nb, bm)[:, 0] >> low_bits, n_exp - 1)
    later = jnp.where(blk_e[None, :] > blk_e[:, None], blk_e[None, :], n_exp)
    next_e = jnp.min(later, axis=1)
    next_e = jnp.where(next_e == n_exp, blk_e, next_e).astype(jnp.int32)
    n_used = (jnp.sum(counts + n_pad) // bm).astype(jnp.int32).reshape(1)
    return blk_e, next_e, n_used, idx


def _moe(x, mod_g, g_ffn, g_final, router_w, router_b, wg, wu, wd, swg, swu, swd,
         *, layer, tm, bm, split_rows=None):
    t, d = x.shape
    n_exp = router_w.shape[1]
    h_slabs, shared, eidx, wts, cnt = _router(
        x, mod_g, g_ffn, router_w.T.astype(_BF16), router_b,
        swg.astype(_BF16), swu.astype(_BF16), swd.astype(_BF16), tm=tm)
    nb = (t * TOP_K) // bm + n_exp
    blk_e, next_e, n_used, idx = _dispatch_tables(eidx, cnt[:, 0], bm=bm, nb=nb, t=t)
    y = _routed_ffn(blk_e, next_e, n_used, idx, h_slabs, wg, wu, wd, layer=layer, bm=bm,
                    n_out_rows=TOP_K * t + 2 * bm)
    w_rows = jnp.repeat(wts.T, d // SLAB_COLS, axis=0)
    return _combine(x, shared, w_rows, mod_g, g_final, y, tm=tm, split_rows=split_rows)


def _tile_rows(seq, dec_rows):
    for tm in (512, 256, 128, 64):
        if seq % tm == 0 and dec_rows % tm == 0:
            return tm
    raise ValueError("sequence lengths must be multiples of 64")


def kernel(x_prompt, x_sample, state_conv, state_pool, c_prompt, c_sample, w_mod, b_mod, g_mix, g_ffn, g_final, conv_w_in, conv_w, conv_w_out, pool_w, pool_scale, router_w, router_b, exp_w_gate, exp_w_up, exp_w_down, sh_w_gate, sh_w_up, sh_w_down):
    bp, seq, d = x_prompt.shape
    bs, dseq, _ = x_sample.shape
    depth = w_mod.shape[0]
    n_exp = router_w.shape[2]
    assert dseq == GROUP and seq % GROUP == 0 and depth == 2
    assert state_conv.shape[1] == CONV_W - 1 and state_pool.shape[1] == POOL_HDR - 1
    assert n_exp % N_EXPERT_GROUPS == 0 and d % (128 * len(POOL_WINDOWS)) == 0
    tp, ts = bp * seq, bs * dseq
    t = tp + ts
    tm = _tile_rows(seq, ts)
    bm = min(FFN_BLOCK_ROWS, t * TOP_K // n_exp)
    assert (t * TOP_K) % bm == 0
    n_prompt_tiles, tiles_per_seq = tp // tm, seq // tm
    ng_p, ng = tp // GROUP, t // GROUP

    nb_pad = -(-(bp + bs) // 8) * 8
    c_all = jnp.concatenate([c_prompt, c_sample, jnp.zeros((nb_pad - bp - bs, d), _F32)], axis=0)
    mods = _mod_rows(c_all, w_mod, b_mod)
    gps = seq // GROUP
    mod_p = jnp.broadcast_to(mods[:, :bp, None, :], (depth, bp, gps, 6 * d)).reshape(depth, bp * gps, 6 * d)
    mod_g = jnp.concatenate([mod_p, mods[:, bp:bp + bs, :]], axis=1)

    conv_tbl = jnp.pad(state_conv, ((0, 0), (CONV_HDR - (CONV_W - 1), 0), (0, 0)))
    pool_tbl = jnp.pad(state_pool, ((0, 0), (1, 0), (0, 0)))

    statics = dict(tm=tm, n_prompt_tiles=n_prompt_tiles, tiles_per_seq=tiles_per_seq)
    x, conv_tail = _conv_mixer(x_prompt.reshape(tp, d), x_sample.reshape(ts, d), mod_g[0], conv_tbl, g_mix[0],
                               conv_w_in.astype(_BF16), conv_w, conv_w_out.astype(_BF16), **statics)
    x = _moe(x, mod_g[0], g_ffn[0], g_final, router_w[0], router_b[0], exp_w_gate, exp_w_up, exp_w_down,
             sh_w_gate[0], sh_w_up[0], sh_w_down[0], layer=0, tm=tm, bm=bm)
    x, pool_tail = _pool_mixer(x, mod_g[1], pool_tbl, g_mix[1], pool_w.astype(_BF16), pool_scale, **statics)
    y_p, y_s = _moe(x, mod_g[1], g_ffn[1], g_final, router_w[1], router_b[1], exp_w_gate, exp_w_up, exp_w_down,
                    sh_w_gate[1], sh_w_up[1], sh_w_down[1], layer=1, tm=tm, bm=bm, split_rows=(tp, ts))

    return (y_p.reshape(bp, seq, d),
            y_s.reshape(bs, dseq, d),
            conv_tail[gps - 1:ng_p:gps],
            conv_tail[ng_p:],
            pool_tail[gps - 1:ng_p:gps],
            pool_tail[ng_p:])
```

```python
import functools

import jax
import jax.numpy as jnp
from jax import lax
from jax.experimental import pallas as pl
from jax.experimental.pallas import tpu as pltpu

GROUP = 64
CONV_W = 3
POOL_WINDOWS = (2, 4, 8, 16)
POOL_HDR = 16
CONV_HDR = 8
TOP_K = 8
N_EXPERT_GROUPS = 8
TOPK_GROUPS = 4
ROUTED_SCALE = 2.5
PAST_LEN = 4096
EPS = 1e-6
VMEM_LIMIT_V7X = 56 * 1024 * 1024
NEG_INF = float("-inf")

_F32 = jnp.float32
_BF16 = jnp.bfloat16


def _silu(v):
    return v * jax.nn.sigmoid(v)


_U32 = jnp.uint32
_HI_MASK = 0xFFFF0000
SLAB_COLS = 256
IDX_SLOTS = 8
FFN_BLOCK_ROWS = 512
COMBINE_ROWS = 256


def _pack_pair(lo, hi):
    lo_b = lax.bitcast_convert_type(lo.astype(_BF16).astype(_F32), _U32)
    hi_b = lax.bitcast_convert_type(hi.astype(_BF16).astype(_F32), _U32)
    return hi_b | (lo_b >> 16)


def _unpack_pair(p):
    lo = lax.bitcast_convert_type(p << 16, _F32)
    hi = lax.bitcast_convert_type(p & _U32(_HI_MASK), _F32)
    return lo, hi


def _store_slab_cols(ref, row0, j, v, n_rows, ns):
    ref[pl.ds(row0 * ns + j, n_rows, stride=ns), :] = _pack_pair(v[:, :128], v[:, 128:])


def _load_slab_cols(ref, row0, j, n_rows, ns):
    return _unpack_pair(ref[pl.ds(row0 * ns + j, n_rows, stride=ns), :])


def _cparams(n_axes):
    return pltpu.CompilerParams(dimension_semantics=("arbitrary",) * n_axes,
                                vmem_limit_bytes=VMEM_LIMIT_V7X)


def _mod_kernel(c_ref, w_ref, b_ref, o_ref):
    c = c_ref[...]
    ca = _silu(c).astype(_BF16)
    o_ref[...] = jnp.dot(ca, w_ref[...].astype(_BF16), preferred_element_type=_F32) + b_ref[...]


def _mod_rows(c_all, w_mod, b_mod):
    depth, d, n6 = w_mod.shape
    nb = c_all.shape[0]
    tn = min(1024, n6)
    return pl.pallas_call(
        _mod_kernel,
        grid=(depth, n6 // tn),
        in_specs=[pl.BlockSpec((nb, d), lambda l, j: (0, 0)),
                  pl.BlockSpec((None, d, tn), lambda l, j: (l, 0, j)),
                  pl.BlockSpec((None, 1, tn), lambda l, j: (l, 0, j))],
        out_specs=pl.BlockSpec((None, nb, tn), lambda l, j: (l, 0, j)),
        out_shape=jax.ShapeDtypeStruct((depth, nb, n6), _F32),
        compiler_params=_cparams(2),
        name="adaln_rows",
    )(c_all, w_mod, b_mod.reshape(depth, 1, n6))


def _norm_mod_group(xs, mod_ref, g_ref, s, d):
    ms = jnp.mean(xs * xs, axis=-1, keepdims=True)
    xn = (xs * lax.rsqrt(ms + EPS)) * g_ref[...]
    return xn * (1.0 + mod_ref[s:s + 1, d:2 * d]) + mod_ref[s:s + 1, 0:d]


def _conv_mixer_kernel(xp_ref, xs_ref, mod_ref, st_ref, g_ref, wb_ref, wc_ref, wv_ref, cw_ref, wo_ref,
                       out_ref, tail_ref, h_scr, acc_scr, ubuf, gbuf, carry,
                       *, gp, d, n_prompt_tiles, tiles_per_seq):
    i = pl.program_id(0)
    j = pl.program_id(1)
    nj = pl.num_programs(1)
    is_sample = i >= n_prompt_tiles

    def x_rows(s):
        sl = slice(s * GROUP, (s + 1) * GROUP)
        return jnp.where(is_sample, xs_ref[sl, :], xp_ref[sl, :])

    @pl.when(j == 0)
    def _():
        for s in range(gp):
            h_scr[s * GROUP:(s + 1) * GROUP, :] = _norm_mod_group(x_rows(s), mod_ref, g_ref, s, d).astype(_BF16)
        acc_scr[...] = jnp.zeros_like(acc_scr)

    @pl.when(i == 0)
    def _():
        carry[j] = jnp.zeros((CONV_HDR, carry.shape[2]), _F32)

    hb = h_scr[...]
    bq = jnp.dot(hb, wb_ref[...], preferred_element_type=_F32)
    cq = jnp.dot(hb, wc_ref[...], preferred_element_type=_F32)
    vq = jnp.dot(hb, wv_ref[...], preferred_element_type=_F32)
    u = cq * vq

    stride = GROUP + CONV_HDR
    for s in range(gp):
        if s == 0:
            prev = jnp.where(lax.rem(i, tiles_per_seq) == 0, 0.0, carry[j])
        else:
            prev = u[s * GROUP - CONV_HDR:s * GROUP, :]
        ubuf[s * stride:s * stride + CONV_HDR, :] = jnp.where(is_sample, st_ref[s], prev)
        ubuf[s * stride + CONV_HDR:(s + 1) * stride, :] = u[s * GROUP:(s + 1) * GROUP, :]
        tail_ref[s] = u[(s + 1) * GROUP - (CONV_W - 1):(s + 1) * GROUP, :]
    carry[j] = u[gp * GROUP - CONV_HDR:gp * GROUP, :]

    w0 = cw_ref[0:1, :]
    w1 = cw_ref[1:2, :]
    w2 = cw_ref[2:3, :]
    for s in range(gp):
        base = s * stride + CONV_HDR
        u0 = ubuf[base:base + GROUP, :]
        u1 = ubuf[base - 1:base - 1 + GROUP, :]
        u2 = ubuf[base - 2:base - 2 + GROUP, :]
        conv = u2 * w0 + u1 * w1 + u0 * w2
        gbuf[s * GROUP:(s + 1) * GROUP, :] = (bq[s * GROUP:(s + 1) * GROUP, :] * conv).astype(_BF16)
    acc_scr[...] += jnp.dot(gbuf[...], wo_ref[...], preferred_element_type=_F32)

    @pl.when(j == nj - 1)
    def _():
        for s in range(gp):
            sl = slice(s * GROUP, (s + 1) * GROUP)
            out_ref[sl, :] = x_rows(s) + mod_ref[s:s + 1, 2 * d:3 * d] * acc_scr[sl, :]


def _conv_mixer(xp, xs, mod_g, st_tbl, g_mix, w_in_bf, conv_w, w_out_bf, *, tm, n_prompt_tiles, tiles_per_seq):
    d = xp.shape[1]
    t = xp.shape[0] + xs.shape[0]
    npt = n_prompt_tiles
    gp = tm // GROUP
    tn = min(512, d)
    nj = d // tn
    kern = functools.partial(_conv_mixer_kernel, gp=gp, d=d, n_prompt_tiles=n_prompt_tiles,
                             tiles_per_seq=tiles_per_seq)
    return pl.pallas_call(
        kern,
        grid=(t // tm, nj),
        in_specs=[pl.BlockSpec((tm, d), lambda i, j: (jnp.minimum(i, npt - 1), 0)),
                  pl.BlockSpec((tm, d), lambda i, j: (jnp.maximum(i - npt, 0), 0)),
                  pl.BlockSpec((gp, 3 * d), lambda i, j: (i, 0)),
                  pl.BlockSpec((gp, CONV_HDR, tn), lambda i, j: (jnp.maximum(i - npt, 0), 0, j)),
                  pl.BlockSpec((1, d), lambda i, j: (0, 0)),
                  pl.BlockSpec((d, tn), lambda i, j: (0, j)),
                  pl.BlockSpec((d, tn), lambda i, j: (0, nj + j)),
                  pl.BlockSpec((d, tn), lambda i, j: (0, 2 * nj + j)),
                  pl.BlockSpec((CONV_W, tn), lambda i, j: (0, j)),
                  pl.BlockSpec((tn, d), lambda i, j: (j, 0))],
        out_specs=[pl.BlockSpec((tm, d), lambda i, j: (i, 0)),
                   pl.BlockSpec((gp, CONV_W - 1, tn), lambda i, j: (i, 0, j))],
        out_shape=[jax.ShapeDtypeStruct((t, d), _F32),
                   jax.ShapeDtypeStruct((t // GROUP, CONV_W - 1, d), _F32)],
        scratch_shapes=[pltpu.VMEM((tm, d), _BF16),
                        pltpu.VMEM((tm, d), _F32),
                        pltpu.VMEM((gp * (GROUP + CONV_HDR), tn), _F32),
                        pltpu.VMEM((tm, tn), _BF16),
                        pltpu.VMEM((nj, CONV_HDR, tn), _F32)],
        compiler_params=_cparams(2),
        name="conv_mixer",
    )(xp, xs, mod_g, st_tbl, g_mix.reshape(1, d), w_in_bf, w_in_bf, w_in_bf, conv_w, w_out_bf)


def _pool_mixer_kernel(x_ref, mod_ref, st_ref, g_ref, pw_ref, ps_ref, out_ref, tail_ref,
                       dbuf, carry, *, gp, d, n_prompt_tiles, tiles_per_seq, tm):
    i = pl.program_id(0)
    pg = d // len(POOL_WINDOWS)
    is_sample = i >= n_prompt_tiles
    row = lax.broadcasted_iota(jnp.int32, (GROUP, 1), 0)

    @pl.when(i == 0)
    def _():
        carry[...] = jnp.zeros_like(carry)

    prev = jnp.where(lax.rem(i, tiles_per_seq) == 0, 0.0, carry[...])
    for s in range(gp):
        h = _norm_mod_group(x_ref[s * GROUP:(s + 1) * GROUP, :], mod_ref, g_ref, s, d)
        hdr = jnp.where(is_sample, st_ref[s], prev)
        he = jnp.concatenate([hdr, h], axis=0)
        pos0 = jnp.where(is_sample, PAST_LEN, lax.rem(i, tiles_per_seq) * tm + s * GROUP)
        pos1 = pos0 + row + 1
        for q, w in enumerate(POOL_WINDOWS):
            cs = slice(q * pg, (q + 1) * pg)
            sw = he[:, cs]
            span = 1
            while span < w:
                sw = sw + pltpu.roll(sw, span, 0)
                span *= 2
            inv_cnt = 1.0 / jnp.minimum(pos1, w).astype(_F32)
            pooled = sw[POOL_HDR:, :] * inv_cnt
            dbuf[s * GROUP:(s + 1) * GROUP, cs] = (pooled - h[:, cs]).astype(_BF16)
        prev = h[GROUP - POOL_HDR:, :]
        tail_ref[s] = h[GROUP - (POOL_HDR - 1):, :]
    carry[...] = prev
    for q in range(len(POOL_WINDOWS)):
        cs = slice(q * pg, (q + 1) * pg)
        y = jnp.dot(dbuf[:, cs], pw_ref[q], preferred_element_type=_F32) * ps_ref[:, cs]
        for s in range(gp):
            sl = slice(s * GROUP, (s + 1) * GROUP)
            out_ref[sl, cs] = x_ref[sl, cs] + mod_ref[s:s + 1, 2 * d + q * pg:2 * d + (q + 1) * pg] * y[sl, :]


def _pool_mixer(x, mod_g, st_tbl, g_mix, pool_w_bf, pool_scale, *, tm, n_prompt_tiles, tiles_per_seq):
    t, d = x.shape
    gp = tm // GROUP
    nq = len(POOL_WINDOWS)
    pg = d // nq
    kern = functools.partial(_pool_mixer_kernel, gp=gp, d=d, n_prompt_tiles=n_prompt_tiles,
                             tiles_per_seq=tiles_per_seq, tm=tm)
    return pl.pallas_call(
        kern,
        grid=(t // tm,),
        in_specs=[pl.BlockSpec((tm, d), lambda i: (i, 0)),
                  pl.BlockSpec((gp, 3 * d), lambda i: (i, 0)),
                  pl.BlockSpec((gp, POOL_HDR, d), lambda i: (jnp.maximum(i - n_prompt_tiles, 0), 0, 0)),
                  pl.BlockSpec((1, d), lambda i: (0, 0)),
                  pl.BlockSpec((nq, pg, pg), lambda i: (0, 0, 0)),
                  pl.BlockSpec((1, d), lambda i: (0, 0))],
        out_specs=[pl.BlockSpec((tm, d), lambda i: (i, 0)),
                   pl.BlockSpec((gp, POOL_HDR - 1, d), lambda i: (i, 0, 0))],
        out_shape=[jax.ShapeDtypeStruct((t, d), _F32),
                   jax.ShapeDtypeStruct((t // GROUP, POOL_HDR - 1, d), _F32)],
        scratch_shapes=[pltpu.VMEM((tm, d), _BF16),
                        pltpu.VMEM((POOL_HDR, d), _F32)],
        compiler_params=_cparams(1),
        name="pool_mixer",
    )(x, mod_g, st_tbl, g_mix.reshape(1, d), pool_w_bf, pool_scale.reshape(1, d))


def _router_kernel(x_ref, mod_ref, g_ref, wr_ref, br_ref, swg_ref, swu_ref, swd_ref,
                   hs_ref, sh_ref, eidx_ref, wts_ref, cnt_ref, run_scr, hb_scr,
                   *, gp, d, n_exp, tm):
    i = pl.program_id(0)
    per_group = n_exp // N_EXPERT_GROUPS
    ns = d // SLAB_COLS

    @pl.when(i == 0)
    def _():
        run_scr[...] = jnp.zeros_like(run_scr)

    for s in range(gp):
        sl = slice(s * GROUP, (s + 1) * GROUP)
        h = _norm_mod_group(x_ref[sl, :], mod_ref, g_ref, s, d)
        hb_scr[sl, :] = h.astype(_BF16)
        for j in range(ns):
            _store_slab_cols(hs_ref, s * GROUP, j, h[:, SLAB_COLS * j:SLAB_COLS * (j + 1)], GROUP, ns)

    logits = lax.dot_general(wr_ref[...], hb_scr[...], (((1,), (1,)), ((), ())),
                             preferred_element_type=_F32)
    sub = lax.broadcasted_iota(jnp.int32, (per_group, tm), 0)
    scores, biased, eids = [], [], []
    for g in range(N_EXPERT_GROUPS):
        rs = slice(g * per_group, (g + 1) * per_group)
        sc = jax.nn.sigmoid(logits[rs, :])
        scores.append(sc)
        biased.append(sc + br_ref[rs, :])
        eids.append(sub + g * per_group)

    gscore = []
    for g in range(N_EXPERT_GROUPS):
        m1 = jnp.max(biased[g], axis=0, keepdims=True)
        i1 = jnp.min(jnp.where(biased[g] == m1, sub, per_group), axis=0, keepdims=True)
        m2 = jnp.max(jnp.where(sub == i1, NEG_INF, biased[g]), axis=0, keepdims=True)
        gscore.append(m1 + m2)
    vals = []
    for g in range(N_EXPERT_GROUPS):
        beaten = jnp.zeros((1, tm), jnp.int32)
        for g2 in range(N_EXPERT_GROUPS):
            if g2 == g:
                continue
            beat = (gscore[g2] >= gscore[g]) if g2 < g else (gscore[g2] > gscore[g])
            beaten = beaten + beat.astype(jnp.int32)
        vals.append(jnp.where(beaten < TOPK_GROUPS, biased[g], NEG_INF))

    sel = [jnp.zeros((per_group, tm), _F32) for _ in range(N_EXPERT_GROUPS)]
    idxs, picked = [], []
    for _ in range(TOP_K):
        mx = vals[0]
        for g in range(1, N_EXPERT_GROUPS):
            mx = jnp.maximum(mx, vals[g])
        m = jnp.max(mx, axis=0, keepdims=True)
        ci = jnp.where(vals[0] == m, eids[0], n_exp)
        for g in range(1, N_EXPERT_GROUPS):
            ci = jnp.minimum(ci, jnp.where(vals[g] == m, eids[g], n_exp))
        idx = jnp.min(ci, axis=0, keepdims=True)
        sc_k = jnp.zeros((1, tm), _F32)
        for g in range(N_EXPERT_GROUPS):
            oh = eids[g] == idx
            sc_k = sc_k + jnp.sum(jnp.where(oh, scores[g], 0.0), axis=0, keepdims=True)
            vals[g] = jnp.where(oh, NEG_INF, vals[g])
            sel[g] = jnp.where(oh, 1.0, sel[g])
        idxs.append(idx)
        picked.append(sc_k)
    den = picked[0]
    for k in range(1, TOP_K):
        den = den + picked[k]

    for k in range(TOP_K):
        eidx_ref[k:k + 1, :] = idxs[k]
        wts_ref[k:k + 1, :] = picked[k] / den * ROUTED_SCALE
    sel_all = jnp.concatenate(sel, axis=0)
    run_scr[...] = run_scr[...] + jnp.sum(sel_all, axis=1, keepdims=True)
    cnt_ref[...] = run_scr[...].astype(jnp.int32)

    hb = hb_scr[...]
    sg = jnp.dot(hb, swg_ref[...], preferred_element_type=_F32)
    su = jnp.dot(hb, swu_ref[...], preferred_element_type=_F32)
    sh_ref[...] = jnp.dot((_silu(sg) * su).astype(_BF16), swd_ref[...],
                          preferred_element_type=_F32).astype(_BF16)


def _router(x, mod_g, g_ffn, wr_t, br, swg_bf, swu_bf, swd_bf, *, tm):
    t, d = x.shape
    gp = tm // GROUP
    n_exp = wr_t.shape[0]
    f = swg_bf.shape[1]
    ns = d // SLAB_COLS
    kern = functools.partial(_router_kernel, gp=gp, d=d, n_exp=n_exp, tm=tm)
    const = lambda i: (0, 0)
    return pl.pallas_call(
        kern,
        grid=(t // tm,),
        in_specs=[pl.BlockSpec((tm, d), lambda i: (i, 0)),
                  pl.BlockSpec((gp, 3 * d), lambda i: (i, 1)),
                  pl.BlockSpec((1, d), const),
                  pl.BlockSpec((n_exp, d), const),
                  pl.BlockSpec((n_exp, 1), const),
                  pl.BlockSpec((d, f), const),
                  pl.BlockSpec((d, f), const),
                  pl.BlockSpec((f, d), const)],
        out_specs=[pl.BlockSpec((tm * ns, 128), lambda i: (i, 0)),
                   pl.BlockSpec((tm, d), lambda i: (i, 0)),
                   pl.BlockSpec((TOP_K, tm), lambda i: (0, i)),
                   pl.BlockSpec((TOP_K, tm), lambda i: (0, i)),
                   pl.BlockSpec((n_exp, 128), const)],
        out_shape=[jax.ShapeDtypeStruct((t * ns, 128), _U32),
                   jax.ShapeDtypeStruct((t, d), _BF16),
                   jax.ShapeDtypeStruct((TOP_K, t), jnp.int32),
                   jax.ShapeDtypeStruct((TOP_K, t), _F32),
                   jax.ShapeDtypeStruct((n_exp, 128), jnp.int32)],
        scratch_shapes=[pltpu.VMEM((n_exp, 128), _F32),
                        pltpu.VMEM((tm, d), _BF16)],
        compiler_params=_cparams(1),
        name="router_shared",
    )(x, mod_g, g_ffn.reshape(1, d), wr_t, br.reshape(n_exp, 1), swg_bf, swu_bf, swd_bf)


def _ffn_pair_kernel(be_ref, ne_ref, nu_ref, idx_hbm, h_hbm, wg_hbm, wu_hbm, wd_hbm, y_hbm,
                     idx_s, xflat, yflat, xmat, hbuf, wg32, wu32, wd32, wgb, wub, wdb,
                     sem_i, sem_g, sem_s, sem_w, *, bm, ns, layer):
    g = pl.program_id(0)
    nu = nu_ref[0]
    nb = 2 * pl.num_programs(0)
    rows = bm * ns
    f = wgb.shape[1]
    tbl = 2 * bm

    def idx_off(blk):
        return pl.multiple_of(lax.rem(blk, IDX_SLOTS) * tbl, tbl)

    def idx_copy(blk, sl=None):
        sl = lax.rem(blk, IDX_SLOTS) if sl is None else sl
        return pltpu.make_async_copy(idx_hbm.at[blk], idx_s.at[pl.ds(pl.multiple_of(sl * tbl, tbl), tbl)],
                                     sem_i.at[sl])

    def weight_copies(e):
        return [pltpu.make_async_copy(src.at[layer, e], dst, sem_w.at[0])
                for src, dst in ((wg_hbm, wg32), (wu_hbm, wu32), (wd_hbm, wd32))]

    def row_gather(half, base, r):
        src = pl.multiple_of(idx_s[base + r] * ns, ns)
        return pltpu.make_async_copy(h_hbm.at[pl.ds(src, ns)], xflat.at[pl.ds((half * bm + r) * ns, ns)],
                                     sem_g.at[half])

    def row_scatter(half, base, r):
        dst = pl.multiple_of(idx_s[base + bm + r] * ns, ns)
        return pltpu.make_async_copy(yflat.at[pl.ds((half * bm + r) * ns, ns)], y_hbm.at[pl.ds(dst, ns)],
                                     sem_s.at[half])

    def block_gather(half):
        return pltpu.make_async_copy(h_hbm.at[pl.ds(0, rows)], xflat.at[pl.ds(half * rows, rows)],
                                     sem_g.at[half])

    def block_scatter(half, dst0=0):
        return pltpu.make_async_copy(yflat.at[pl.ds(half * rows, rows)], y_hbm.at[pl.ds(dst0, rows)],
                                     sem_s.at[half])

    def block(blk, half):
        other = 1 - half
        if half == 0:
            @pl.when(blk == 0)
            def _():
                for cp in weight_copies(be_ref[0]):
                    cp.start()
                yflat[...] = jnp.zeros_like(yflat)
                block_scatter(0, y_hbm.shape[0] - 2 * rows).start()
                tables = [idx_copy(nb + 1, IDX_SLOTS - 1), idx_copy(0), idx_copy(1), idx_copy(2)]
                for cp in tables:
                    cp.start()
                for cp in tables[:2]:
                    cp.wait()
                for r in range(bm):
                    row_gather(0, 0, r).start()

        new_expert = (blk == 0) | (be_ref[blk] != be_ref[jnp.maximum(blk - 1, 0)])

        @pl.when(new_expert)
        def _():
            for cp in weight_copies(be_ref[blk]):
                cp.wait()
            wgb[...] = wg32[...].astype(_BF16)
            wub[...] = wu32[...].astype(_BF16)
            wdb[...] = wd32[...].astype(_BF16)
            for cp in weight_copies(ne_ref[blk]):
                cp.start()

        base_next = idx_off(blk + 1)
        base_prev = idx_off(blk + IDX_SLOTS - 1)
        idx_copy(blk + 3).start()
        block_gather(half).wait()
        idx_copy(blk + 1).wait()

        per_piece = bm // ns
        for j in range(ns):
            lo, hi = _load_slab_cols(xflat, half * bm, j, bm, ns)
            xmat[:, SLAB_COLS * j:SLAB_COLS * j + 128] = lo.astype(_BF16)
            xmat[:, SLAB_COLS * j + 128:SLAB_COLS * (j + 1)] = hi.astype(_BF16)
            for r in range(j * per_piece, (j + 1) * per_piece):
                row_gather(other, base_next, r).start()
                row_scatter(other, base_prev, r).start()

        x = xmat[...]
        n_fc = f // SLAB_COLS
        for c in range(n_fc):
            cs = slice(SLAB_COLS * c, SLAB_COLS * (c + 1))
            hg = jnp.dot(x, wgb[:, cs], preferred_element_type=_F32)
            hu = jnp.dot(x, wub[:, cs], preferred_element_type=_F32)
            hbuf[:, cs] = (_silu(hg) * hu).astype(_BF16)

        block_scatter(half).wait()
        hb = hbuf[...]
        for j in range(ns):
            o = jnp.dot(hb, wdb[:, SLAB_COLS * j:SLAB_COLS * (j + 1)], preferred_element_type=_F32)
            _store_slab_cols(yflat, half * bm, j, o, bm, ns)

        @pl.when(blk == nu - 1)
        def _():
            base_own = idx_off(blk)
            for r in range(bm):
                row_scatter(half, base_own, r).start()
            block_gather(other).wait()
            idx_copy(blk + 2).wait()
            idx_copy(blk + 3).wait()
            block_scatter(other).wait()
            block_scatter(half).wait()
            for cp in weight_copies(0):
                cp.wait()

    for half in range(2):
        blk = 2 * g + half
        pl.when(blk < nu)(functools.partial(block, blk, half))


def _routed_ffn(blk_e, next_e, n_used, idx, h_slabs, wg, wu, wd, *, layer, bm, n_out_rows):
    nb = idx.shape[0] - 2
    _, _, d, f = wg.shape
    ns = d // SLAB_COLS
    assert f % SLAB_COLS == 0 and bm % ns == 0 and nb % 2 == 0
    kern = functools.partial(_ffn_pair_kernel, bm=bm, ns=ns, layer=layer)
    hbm = pl.BlockSpec(memory_space=pl.ANY)
    grid_spec = pltpu.PrefetchScalarGridSpec(
        num_scalar_prefetch=3,
        grid=(nb // 2,),
        in_specs=[hbm, hbm, hbm, hbm, hbm],
        out_specs=hbm,
        scratch_shapes=[pltpu.SMEM((IDX_SLOTS * 2 * bm,), jnp.int32),
                        pltpu.VMEM((2 * bm * ns, 128), _U32),
                        pltpu.VMEM((2 * bm * ns, 128), _U32),
                        pltpu.VMEM((bm, d), _BF16),
                        pltpu.VMEM((bm, f), _BF16),
                        pltpu.VMEM((d, f), _F32),
                        pltpu.VMEM((d, f), _F32),
                        pltpu.VMEM((f, d), _F32),
                        pltpu.VMEM((d, f), _BF16),
                        pltpu.VMEM((d, f), _BF16),
                        pltpu.VMEM((f, d), _BF16),
                        pltpu.SemaphoreType.DMA((IDX_SLOTS,)),
                        pltpu.SemaphoreType.DMA((2,)),
                        pltpu.SemaphoreType.DMA((2,)),
                        pltpu.SemaphoreType.DMA((1,))])
    return pl.pallas_call(
        kern,
        grid_spec=grid_spec,
        out_shape=jax.ShapeDtypeStruct((n_out_rows * ns, 128), _U32),
        compiler_params=_cparams(1),
        name="routed_ffn",
    )(blk_e, next_e, n_used, idx, h_slabs, wg, wu, wd)


def _combine_kernel(*refs, d, gpt, n_prompt_tiles):
    x_ref, sh_ref, w_ref, mod_ref, gf_ref = refs[:5]
    y_refs = refs[5:5 + TOP_K]
    n_out = 1 if n_prompt_tiles is None else 2
    out_refs = refs[5 + TOP_K:5 + TOP_K + n_out]
    lo_scr, hi_scr = refs[5 + TOP_K + n_out:]
    i = pl.program_id(0)
    ns = d // SLAB_COLS
    tmc = x_ref.shape[0]
    acc_lo = jnp.zeros(lo_scr.shape, _F32)
    acc_hi = jnp.zeros(hi_scr.shape, _F32)
    for k in range(TOP_K):
        lo, hi = _unpack_pair(y_refs[k][...])
        wk = w_ref[:, k:k + 1]
        acc_lo = acc_lo + wk * lo
        acc_hi = acc_hi + wk * hi
    lo_scr[...] = acc_lo
    hi_scr[...] = acc_hi
    cols = []
    for j in range(ns):
        cols += [lo_scr[pl.ds(j, tmc, stride=ns), :], hi_scr[pl.ds(j, tmc, stride=ns), :]]
    acc = sh_ref[...].astype(_F32) + jnp.concatenate(cols, axis=1)
    groups_per_mod = mod_ref.shape[0]
    g0 = lax.rem(i, groups_per_mod // gpt) * gpt
    rows = []
    for s in range(gpt):
        sl = slice(s * GROUP, (s + 1) * GROUP)
        gate = mod_ref[pl.ds(g0 + s, 1), 2 * d:3 * d]
        xo = x_ref[sl, :] + gate * acc[sl, :]
        if n_prompt_tiles is not None:
            ms = jnp.mean(xo * xo, axis=-1, keepdims=True)
            xo = (xo * lax.rsqrt(ms + EPS)) * gf_ref[...]
        rows.append(xo)
    if n_prompt_tiles is None:
        for s in range(gpt):
            out_refs[0][s * GROUP:(s + 1) * GROUP, :] = rows[s]
    else:
        for o_ref, active in ((out_refs[0], i < n_prompt_tiles), (out_refs[1], i >= n_prompt_tiles)):
            @pl.when(active)
            def _(o_ref=o_ref):
                for s in range(gpt):
                    o_ref[s * GROUP:(s + 1) * GROUP, :] = rows[s]


def _combine(x, shared, wts, mod_g, g_final, y, *, tm, split_rows=None):
    t, d = x.shape
    tmc = min(COMBINE_ROWS, tm)
    gpt = tmc // GROUP
    gp = tm // GROUP
    nt = t // tmc
    row_spec = pl.BlockSpec((tmc, d), lambda i: (i, 0))
    if split_rows is None:
        npt = None
        out_specs = row_spec
        out_shape = jax.ShapeDtypeStruct((t, d), _F32)
    else:
        npt = split_rows[0] // tmc
        out_specs = [pl.BlockSpec((tmc, d), lambda i: (jnp.minimum(i, npt - 1), 0)),
                     pl.BlockSpec((tmc, d), lambda i: (jnp.maximum(i - npt, 0), 0))]
        out_shape = [jax.ShapeDtypeStruct((rows, d), _F32) for rows in split_rows]
    kern = functools.partial(_combine_kernel, d=d, gpt=gpt, n_prompt_tiles=npt)
    ns = d // SLAB_COLS
    y_specs = [pl.BlockSpec((tmc * ns, 128), functools.partial(lambda i, k: (k * nt + i, 0), k=k))
               for k in range(TOP_K)]
    return pl.pallas_call(
        kern,
        grid=(nt,),
        scratch_shapes=[pltpu.VMEM((tmc * ns, 128), _F32), pltpu.VMEM((tmc * ns, 128), _F32)],
        in_specs=[row_spec,
                  row_spec,
                  pl.BlockSpec((tmc * ns, TOP_K), lambda i: (i, 0)),
                  pl.BlockSpec((gp, 3 * d), lambda i: (i // (tm // tmc), 1)),
                  pl.BlockSpec((1, d), lambda i: (0, 0))] + y_specs,
        out_specs=out_specs,
        out_shape=out_shape,
        compiler_params=_cparams(1),
        name="combine" if split_rows is None else "combine_final",
    )(x, shared, wts, mod_g, g_final.reshape(1, d), *([y] * TOP_K))


def _dispatch_tables(eidx, counts, *, bm, nb, t):
    n_exp = counts.shape[0]
    low_bits = (t * TOP_K).bit_length()
    low_mask = (1 << low_bits) - 1
    assert t * TOP_K < low_mask and (n_exp + 1) << low_bits < 2 ** 31
    tok_k = jnp.arange(t, dtype=jnp.int32)[None, :] * TOP_K + jnp.arange(TOP_K, dtype=jnp.int32)[:, None]
    real = (eidx << low_bits) + tok_k
    n_pad = (-counts) % bm
    e_col = jnp.arange(n_exp, dtype=jnp.int32)[:, None]
    pad = jnp.where(jnp.arange(bm, dtype=jnp.int32)[None, :] < n_pad[:, None],
                    (e_col << low_bits) + low_mask, (n_exp << low_bits) + low_mask)
    keys = jnp.sort(jnp.concatenate([real.reshape(-1), pad.reshape(-1)]), stable=False)
    low = keys & low_mask
    is_pad = low == low_mask
    tok = low // TOP_K
    r = jnp.arange(nb * bm, dtype=jnp.int32)
    row_src = jnp.where(is_pad, r % t, tok)
    row_dst = jnp.where(is_pad, TOP_K * t + ((r // bm) % 2) * bm + (r % bm),
                        (low % TOP_K) * t + tok)
    idx = jnp.concatenate([row_src.reshape(nb, bm), row_dst.reshape(nb, bm)], axis=1)
    before = jnp.concatenate([jnp.zeros((bm,), jnp.int32), TOP_K * t + bm + jnp.arange(bm, dtype=jnp.int32)])
    idx = jnp.concatenate([idx, jnp.zeros((1, 2 * bm), jnp.int32), before[None, :]], axis=0)
    blk_e = jnp.minimum(keys.reshape(nb, bm)[:, 0] >> low_bits, n_exp - 1)
    later = jnp.where(blk_e[None, :] > blk_e[:, None], blk_e[None, :], n_exp)
    next_e = jnp.min(later, axis=1)
    next_e = jnp.where(next_e == n_exp, blk_e, next_e).astype(jnp.int32)
    n_used = (jnp.sum(counts + n_pad) // bm).astype(jnp.int32).reshape(1)
    return blk_e, next_e, n_used, idx


def _moe(x, mod_g, g_ffn, g_final, router_w, router_b, wg, wu, wd, swg, swu, swd,
         *, layer, tm, bm, split_rows=None):
    t, d = x.shape
    n_exp = router_w.shape[1]
    h_slabs, shared, eidx, wts, cnt = _router(
        x, mod_g, g_ffn, router_w.T.astype(_BF16), router_b,
        swg.astype(_BF16), swu.astype(_BF16), swd.astype(_BF16), tm=tm)
    nb = (t * TOP_K) // bm + n_exp
    blk_e, next_e, n_used, idx = _dispatch_tables(eidx, cnt[:, 0], bm=bm, nb=nb, t=t)
    y = _routed_ffn(blk_e, next_e, n_used, idx, h_slabs, wg, wu, wd, layer=layer, bm=bm,
                    n_out_rows=TOP_K * t + 2 * bm)
    w_rows = jnp.repeat(wts.T, d // SLAB_COLS, axis=0)
    return _combine(x, shared, w_rows, mod_g, g_final, y, tm=tm, split_rows=split_rows)


def _tile_rows(seq, dec_rows):
    for tm in (512, 256, 128, 64):
        if seq % tm == 0 and dec_rows % tm == 0:
            return tm
    raise ValueError("sequence lengths must be multiples of 64")


def kernel(x_prompt, x_sample, state_conv, state_pool, c_prompt, c_sample, w_mod, b_mod, g_mix, g_ffn, g_final, conv_w_in, conv_w, conv_w_out, pool_w, pool_scale, router_w, router_b, exp_w_gate, exp_w_up, exp_w_down, sh_w_gate, sh_w_up, sh_w_down):
    bp, seq, d = x_prompt.shape
    bs, dseq, _ = x_sample.shape
    depth = w_mod.shape[0]
    n_exp = router_w.shape[2]
    assert dseq == GROUP and seq % GROUP == 0 and depth == 2
    assert state_conv.shape[1] == CONV_W - 1 and state_pool.shape[1] == POOL_HDR - 1
    assert n_exp % N_EXPERT_GROUPS == 0 and d % (128 * len(POOL_WINDOWS)) == 0
    tp, ts = bp * seq, bs * dseq
    t = tp + ts
    tm = _tile_rows(seq, ts)
    bm = min(FFN_BLOCK_ROWS, t * TOP_K // n_exp)
    assert (t * TOP_K) % bm == 0
    n_prompt_tiles, tiles_per_seq = tp // tm, seq // tm
    ng_p, ng = tp // GROUP, t // GROUP

    nb_pad = -(-(bp + bs) // 8) * 8
    c_all = jnp.concatenate([c_prompt, c_sample, jnp.zeros((nb_pad - bp - bs, d), _F32)], axis=0)
    mods = _mod_rows(c_all, w_mod, b_mod)
    gps = seq // GROUP
    mod_p = jnp.broadcast_to(mods[:, :bp, None, :], (depth, bp, gps, 6 * d)).reshape(depth, bp * gps, 6 * d)
    mod_g = jnp.concatenate([mod_p, mods[:, bp:bp + bs, :]], axis=1)

    conv_tbl = jnp.pad(state_conv, ((0, 0), (CONV_HDR - (CONV_W - 1), 0), (0, 0)))
    pool_tbl = jnp.pad(state_pool, ((0, 0), (1, 0), (0, 0)))

    statics = dict(tm=tm, n_prompt_tiles=n_prompt_tiles, tiles_per_seq=tiles_per_seq)
    x, conv_tail = _conv_mixer(x_prompt.reshape(tp, d), x_sample.reshape(ts, d), mod_g[0], conv_tbl, g_mix[0],
                               conv_w_in.astype(_BF16), conv_w, conv_w_out.astype(_BF16), **statics)
    x = _moe(x, mod_g[0], g_ffn[0], g_final, router_w[0], router_b[0], exp_w_gate, exp_w_up, exp_w_down,
             sh_w_gate[0], sh_w_up[0], sh_w_down[0], layer=0, tm=tm, bm=bm)
    x, pool_tail = _pool_mixer(x, mod_g[1], pool_tbl, g_mix[1], pool_w.astype(_BF16), pool_scale, **statics)
    y_p, y_s = _moe(x, mod_g[1], g_ffn[1], g_final, router_w[1], router_b[1], exp_w_gate, exp_w_up, exp_w_down,
                    sh_w_gate[1], sh_w_up[1], sh_w_down[1], layer=1, tm=tm, bm=bm, split_rows=(tp, ts))

    return (y_p.reshape(bp, seq, d),
            y_s.reshape(bs, dseq, d),
            conv_tail[gps - 1:ng_p:gps],
            conv_tail[ng_p:],
            pool_tail[gps - 1:ng_p:gps],
            pool_tail[ng_p:])
```
